```python
import jax, jax.numpy as jnp
from jax import lax
import numpy as np

D_MODEL = 1024
BATCH = 8
SEQ = 4096
DEPTH = 2

MEM_LEN = 256
HEAD_DIM = 64
NSA_HEADS = 8
NSA_KV_HEADS = 2
NSA_GROUP = NSA_HEADS // NSA_KV_HEADS
CMP_BLOCK = 32
CMP_STRIDE = 16
CMP_HIDDEN = 256
SLC_BLOCK = 64
SLC_TOP = 16
WINDOW = 512
NSA_QBLOCK = 64
SLC_FORCED_SCORE = 1e4
SGU_GROUPS = 4
SGU_CH = 128
SGU_CHUNK = 128
HGRN_HEADS = 4
HGRN_DK = 128
HGRN_DV = 128
HGRN_CHUNK = 64
CONV_CH = 512
CONV_WIDTH = 3
XATTN_HEADS = 4
XATTN_DIM = D_MODEL // XATTN_HEADS
N_EXPERTS = 32
TOP_K = 4
D_EXPERT = D_MODEL
SWIGLU_LIMIT = 7.0
SWIGLU_ALPHA = 1.702
MOE_BLOCK = 128

ROPE_THETA = 10000.0
LN_EPS = 1e-5
RMS_EPS = 1e-6
NEG_INF = -1e30
DEEPNORM_ALPHA = (2 * DEPTH) ** 0.25
DEEPNORM_BETA = (8 * DEPTH) ** -0.25

N_EVEN = (DEPTH + 1) // 2
N_ODD = DEPTH // 2
NSA_Q = NSA_HEADS * HEAD_DIM
NSA_KV = 3 * 2 * NSA_KV_HEADS * HEAD_DIM
NSA_GATES = 3 * NSA_HEADS
SGU_W = SGU_GROUPS * SGU_CH
EVEN_IN = NSA_Q + NSA_KV + NSA_GATES + 2 * SGU_W
EVEN_OUT = NSA_Q + SGU_W
HGRN_W = HGRN_HEADS * HGRN_DK
ODD_IN = 4 * HGRN_W + 3 * CONV_CH
ODD_OUT = HGRN_W + CONV_CH

kernel_name = "hybrid_nsa_gmlp_hgrn2_conv_moe_deepnorm"


def layer_norm(x, g, b):
    xf = x.astype(jnp.float32)
    mu = jnp.mean(xf, axis=-1, keepdims=True)
    var = jnp.mean(jnp.square(xf - mu), axis=-1, keepdims=True)
    return ((xf - mu) * lax.rsqrt(var + LN_EPS) * g + b).astype(x.dtype)


def rope_tables(positions):
    inv = 1.0 / (ROPE_THETA ** (jnp.arange(0, HEAD_DIM, 2, dtype=jnp.float32) / HEAD_DIM))
    ang = positions.astype(jnp.float32)[..., None] * inv
    return jnp.cos(ang)[:, :, None, :], jnp.sin(ang)[:, :, None, :]


def apply_rope(x, cos, sin):
    x1, x2 = jnp.split(x.astype(jnp.float32), 2, axis=-1)
    return jnp.concatenate([x1 * cos - x2 * sin, x2 * cos + x1 * sin], axis=-1).astype(x.dtype)


def masked_softmax(s, mask):
    s = jnp.where(mask, s.astype(jnp.float32), NEG_INF)
    return jax.nn.softmax(s, axis=-1) * mask


def cmp_to_slc_matrix(seq):
    nc = (seq - CMP_BLOCK) // CMP_STRIDE + 1
    ns = seq // SLC_BLOCK
    cs = np.arange(nc)[:, None] * CMP_STRIDE
    ss = np.arange(ns)[None, :] * SLC_BLOCK
    ov = np.clip(np.minimum(cs + CMP_BLOCK, ss + SLC_BLOCK) - np.maximum(cs, ss), 0, None)
    return jnp.asarray(ov / CMP_BLOCK, dtype=jnp.float32)


def nsa_mixer(q, kv, gates, cos, sin, cmp_pos, cmp_w1, cmp_w2):
    Bsz, S, _ = q.shape
    G, Hg, hd, QB = NSA_KV_HEADS, NSA_GROUP, HEAD_DIM, NSA_QBLOCK
    dt = q.dtype
    q = q.reshape(Bsz, S, NSA_HEADS, hd)
    q_rot = apply_rope(q, cos, sin)
    kv = kv.reshape(Bsz, S, 6, G, hd)
    k_cmp, v_cmp, k_slc, v_slc, k_win, v_win = [kv[:, :, i] for i in range(6)]
    k_slc = apply_rope(k_slc, cos, sin)
    k_win = apply_rope(k_win, cos, sin)

    nc = (S - CMP_BLOCK) // CMP_STRIDE + 1
    idx = np.arange(nc)[:, None] * CMP_STRIDE + np.arange(CMP_BLOCK)[None, :]

    def compress(t, pos, w1, w2):
        blocks = t[:, idx] + pos[None, None, :, None, :]
        blocks = blocks.transpose(0, 3, 1, 2, 4).reshape(Bsz, G, nc, CMP_BLOCK * hd)
        return jax.nn.gelu(blocks @ w1) @ w2

    kc = compress(k_cmp, cmp_pos[0], cmp_w1[0], cmp_w2[0])
    vc = compress(v_cmp, cmp_pos[1], cmp_w1[1], cmp_w2[1])
    cmp_end = np.arange(nc) * CMP_STRIDE + CMP_BLOCK - 1

    n_slc = S // SLC_BLOCK
    n_top = min(SLC_TOP, n_slc)
    ks_blk = k_slc.reshape(Bsz, n_slc, SLC_BLOCK, G, hd).transpose(0, 3, 1, 2, 4)
    vs_blk = v_slc.reshape(Bsz, n_slc, SLC_BLOCK, G, hd).transpose(0, 3, 1, 2, 4)
    m_cmp = cmp_to_slc_matrix(S)
    slc_ids = jnp.arange(n_slc)
    b_ix = jnp.arange(Bsz)[:, None, None, None]
    g_ix = jnp.arange(G)[None, None, :, None]

    kw_pad = jnp.pad(k_win, ((0, 0), (WINDOW, 0), (0, 0), (0, 0)))
    vw_pad = jnp.pad(v_win, ((0, 0), (WINDOW, 0), (0, 0), (0, 0)))

    gates = jax.nn.sigmoid(gates.astype(jnp.float32)).reshape(Bsz, S, G, Hg, 3)
    scale = hd ** -0.5

    def block(qi):
        s0 = qi * QB
        t = s0 + jnp.arange(QB)
        qb = lax.dynamic_slice_in_dim(q, s0, QB, axis=1).reshape(Bsz, QB, G, Hg, hd)
        qr = lax.dynamic_slice_in_dim(q_rot, s0, QB, axis=1).reshape(Bsz, QB, G, Hg, hd)
        gb = lax.dynamic_slice_in_dim(gates, s0, QB, axis=1)

        s = jnp.einsum('bqghd,bgnd->bqghn', qb, kc) * scale
        mask = (cmp_end[None, :] <= t[:, None])[None, :, None, None, :]
        p_cmp = masked_softmax(s, mask)
        o_cmp = jnp.einsum('bqghn,bgnd->bqghd', p_cmp.astype(dt), vc)

        imp = jnp.einsum('bqgn,nj->bqgj', p_cmp.sum(axis=3), m_cmp)
        cur = (t // SLC_BLOCK)[:, None]
        forced = ((slc_ids == 0) | (slc_ids == cur) | (slc_ids == cur - 1))[None, :, None, :]
        valid = (slc_ids <= cur)[None, :, None, :]
        score = jnp.where(forced, SLC_FORCED_SCORE, jnp.where(valid, imp, -1.0))
        _, sel = lax.top_k(score, n_top)
        k_sel = ks_blk[b_ix, g_ix, sel]
        v_sel = vs_blk[b_ix, g_ix, sel].reshape(Bsz, QB, G, n_top * SLC_BLOCK, hd)
        tok = sel[..., None] * SLC_BLOCK + jnp.arange(SLC_BLOCK)
        m_slc = (tok <= t[None, :, None, None, None]).reshape(Bsz, QB, G, 1, n_top * SLC_BLOCK)
        s = jnp.einsum('bqghd,bqgnkd->bqghnk', qr, k_sel).reshape(Bsz, QB, G, Hg, n_top * SLC_BLOCK) * scale
        p = masked_softmax(s, m_slc)
        o_slc = jnp.einsum('bqghm,bqgmd->bqghd', p.astype(dt), v_sel)

        kw = lax.dynamic_slice_in_dim(kw_pad, s0, QB + WINDOW, axis=1)
        vw = lax.dynamic_slice_in_dim(vw_pad, s0, QB + WINDOW, axis=1)
        kpos = s0 - WINDOW + jnp.arange(QB + WINDOW)
        dist = t[:, None] - kpos[None, :]
        m_win = (kpos[None, :] >= 0) & (dist >= 0) & (dist < WINDOW)
        s = jnp.einsum('bqghd,bkgd->bqghk', qr, kw) * scale
        p = masked_softmax(s, m_win[None, :, None, None, :])
        o_win = jnp.einsum('bqghk,bkgd->bqghd', p.astype(dt), vw)

        o = gb[..., 0:1] * o_cmp + gb[..., 1:2] * o_slc + gb[..., 2:3] * o_win
        return o.astype(dt).reshape(Bsz, QB, NSA_Q)

    out = lax.map(block, jnp.arange(S // QB))
    return out.transpose(1, 0, 2, 3).reshape(Bsz, S, NSA_Q)


def sgu_mixer(u, v, ln_g, ln_b, w_s, b_s):
    Bsz, S, _ = u.shape
    u = jax.nn.gelu(u)
    v = jax.nn.gelu(v).reshape(Bsz, S // SGU_CHUNK, SGU_CHUNK, SGU_GROUPS, SGU_CH)
    v = layer_norm(v, ln_g, ln_b)
    causal = jnp.tril(jnp.ones((SGU_CHUNK, SGU_CHUNK), dtype=w_s.dtype))
    mix = jnp.einsum('gts,bnsgc->bntgc', w_s * causal, v) + b_s.T[:, :, None]
    return u * mix.reshape(Bsz, S, SGU_W)


def hgrn2_mixer(q, f_raw, i, g, lb, norm_g):
    Bsz, S, _ = q.shape
    H, dk, dv, C = HGRN_HEADS, HGRN_DK, HGRN_DV, HGRN_CHUNK
    z = f_raw.astype(jnp.float32)
    log_f = jnp.logaddexp(jnp.log(lb), jnp.log1p(-lb) + jax.nn.log_sigmoid(z))
    k = (1.0 - lb) * jax.nn.sigmoid(-z)
    n = S // C

    def chunks(t, d):
        return t.astype(jnp.float32).reshape(Bsz, n, C, H, d).transpose(1, 0, 3, 2, 4)

    qc, kc, vc, lc = chunks(q, dk), chunks(k, dk), chunks(i, dv), chunks(log_f, dk)
    causal = jnp.tril(jnp.ones((C, C), dtype=bool))[None, None, :, :, None]

    def step(state, inp):
        qt, kt, vt, lt = inp
        b = jnp.cumsum(lt, axis=2)
        o_inter = jnp.einsum('bhtk,bhkv->bhtv', qt * jnp.exp(b), state)
        decay = jnp.exp(jnp.where(causal, b[:, :, :, None, :] - b[:, :, None, :, :], NEG_INF))
        attn = jnp.einsum('bhtk,bhsk,bhtsk->bhts', qt, kt, decay)
        o = o_inter + jnp.einsum('bhts,bhsv->bhtv', attn, vt)
        b_last = b[:, :, -1:, :]
        new_state = (jnp.exp(b_last[:, :, 0, :])[..., None] * state
                     + jnp.einsum('bhsk,bhsv->bhkv', kt * jnp.exp(b_last - b), vt))
        return new_state, o

    state0 = jnp.zeros((Bsz, H, dk, dv), jnp.float32)
    _, o = lax.scan(step, state0, (qc, kc, vc, lc))
    o = o.transpose(1, 0, 3, 2, 4).reshape(Bsz, S, H, dv)
    o = o * lax.rsqrt(jnp.mean(jnp.square(o), axis=-1, keepdims=True) + RMS_EPS) * norm_g
    return (o.reshape(Bsz, S, H * dv) * jax.nn.silu(g.astype(jnp.float32))).astype(q.dtype)


def short_conv_mixer(h, b_gate, c_gate, conv_w, conv_b):
    z = c_gate * h
    y = lax.conv_general_dilated(z, conv_w[:, None, :].astype(z.dtype), window_strides=(1,),
                                 padding=((CONV_WIDTH - 1, 0),),
                                 dimension_numbers=('NWC', 'WIO', 'NWC'),
                                 feature_group_count=CONV_CH) + conv_b
    return b_gate * y


def memory_cross_attention(x, mem, w_q, w_kv, w_o):
    Bsz, S, _ = x.shape
    q = (x @ w_q).reshape(Bsz, S, XATTN_HEADS, XATTN_DIM)
    kv = (mem @ w_kv).reshape(Bsz, mem.shape[1], 2, XATTN_HEADS, XATTN_DIM)
    s = jnp.einsum('bqhd,bkhd->bhqk', q, kv[:, :, 0]).astype(jnp.float32) * XATTN_DIM ** -0.5
    p = jax.nn.softmax(s, axis=-1).astype(x.dtype)
    o = jnp.einsum('bhqk,bkhd->bqhd', p, kv[:, :, 1]).reshape(Bsz, S, XATTN_HEADS * XATTN_DIM)
    return o @ w_o


def moe_ffn(x, w_router, b_router, w_gu, b_gu, w_dn, b_dn):
    Bsz, S, D = x.shape
    xt = x.reshape(-1, D)
    T = xt.shape[0]
    logits = (xt @ w_router).astype(jnp.float32) + b_router
    top_v, top_e = lax.top_k(logits, TOP_K)
    gate = jax.nn.softmax(top_v, axis=-1)
    TK = T * TOP_K
    e_flat = top_e.reshape(-1)
    tok_flat = jnp.arange(TK, dtype=jnp.int32) // TOP_K
    g_flat = gate.reshape(-1)
    order = jnp.argsort(e_flat)
    e_s, tok_s, g_s = e_flat[order], tok_flat[order], g_flat[order]
    counts = jnp.bincount(e_flat, length=N_EXPERTS)
    start = jnp.cumsum(counts) - counts
    padded = (counts + MOE_BLOCK - 1) // MOE_BLOCK * MOE_BLOCK
    pend = jnp.cumsum(padded)
    pstart = pend - padded
    dest = pstart[e_s] + (jnp.arange(TK) - start[e_s])
    R = (-(-TK // MOE_BLOCK) + N_EXPERTS) * MOE_BLOCK
    nb = R // MOE_BLOCK
    row_tok = jnp.zeros((R,), jnp.int32).at[dest].set(tok_s)
    row_gate = jnp.zeros((R,), jnp.float32).at[dest].set(g_s)
    blk_e = jnp.minimum(jnp.searchsorted(pend, jnp.arange(nb) * MOE_BLOCK, side='right'), N_EXPERTS - 1)

    def expert_block(args):
        e, toks = args
        h = xt[toks] @ w_gu[e] + b_gu[e]
        h_gate = jnp.minimum(h[:, :D_EXPERT], SWIGLU_LIMIT)
        h_up = jnp.clip(h[:, D_EXPERT:], -SWIGLU_LIMIT, SWIGLU_LIMIT)
        act = (h_up + 1.0) * (h_gate * jax.nn.sigmoid(h_gate * SWIGLU_ALPHA))
        return act @ w_dn[e] + b_dn[e]

    y = lax.map(expert_block, (blk_e, row_tok.reshape(nb, MOE_BLOCK))).reshape(R, D)
    out = jnp.zeros((T, D), jnp.float32).at[row_tok].add(y.astype(jnp.float32) * row_gate[:, None])
    return out.astype(x.dtype).reshape(Bsz, S, D)


def setup_inputs(seed: int = 0) -> dict:
    key = jax.random.key(seed)
    ks = jax.random.split(key, 32)
    f32 = jnp.float32

    def nrm(k, shape, fan_in, scale=1.0):
        return jax.random.normal(k, shape, f32) * (fan_in ** -0.5 * scale)

    def small(k, shape):
        return 0.01 * jax.random.normal(k, shape, f32)

    offset = jax.random.randint(ks[2], (BATCH, 1), 0, 2048, dtype=jnp.int32)
    positions = offset + jnp.arange(SEQ, dtype=jnp.int32)[None, :]
    return {
        "x": jax.random.normal(ks[0], (BATCH, SEQ, D_MODEL), f32),
        "mem": jax.random.normal(ks[1], (BATCH, MEM_LEN, D_MODEL), f32),
        "positions": positions,
        "w_in_even": nrm(ks[3], (N_EVEN, D_MODEL, EVEN_IN), D_MODEL),
        "nsa_cmp_pos": 0.1 * jax.random.normal(ks[4], (N_EVEN, 2, CMP_BLOCK, HEAD_DIM), f32),
        "nsa_cmp_w1": nrm(ks[5], (N_EVEN, 2, CMP_BLOCK * HEAD_DIM, CMP_HIDDEN), CMP_BLOCK * HEAD_DIM),
        "nsa_cmp_w2": nrm(ks[6], (N_EVEN, 2, CMP_HIDDEN, HEAD_DIM), CMP_HIDDEN),
        "sgu_ln_g": 1.0 + small(ks[7], (N_EVEN, SGU_GROUPS, SGU_CH)),
        "sgu_ln_b": small(ks[8], (N_EVEN, SGU_GROUPS, SGU_CH)),
        "sgu_w": nrm(ks[9], (N_EVEN, SGU_GROUPS, SGU_CHUNK, SGU_CHUNK), SGU_CHUNK),
        "sgu_b": 1.0 + small(ks[10], (N_EVEN, SGU_GROUPS, SGU_CHUNK)),
        "w_out_even": nrm(ks[11], (N_EVEN, EVEN_OUT, D_MODEL), EVEN_OUT, DEEPNORM_BETA),
        "w_in_odd": nrm(ks[12], (N_ODD, D_MODEL, ODD_IN), D_MODEL),
        "hgrn_lb_logits": 0.1 * jax.random.normal(ks[13], (DEPTH, HGRN_W), f32),
        "hgrn_norm_g": 1.0 + small(ks[14], (N_ODD, HGRN_HEADS, HGRN_DV)),
        "conv_w": nrm(ks[15], (N_ODD, CONV_WIDTH, CONV_CH), CONV_WIDTH),
        "conv_b": small(ks[16], (N_ODD, CONV_CH)),
        "w_out_odd": nrm(ks[17], (N_ODD, ODD_OUT, D_MODEL), ODD_OUT, DEEPNORM_BETA),
        "xattn_w_q": nrm(ks[18], (DEPTH, D_MODEL, XATTN_HEADS * XATTN_DIM), D_MODEL),
        "xattn_w_kv": nrm(ks[19], (DEPTH, D_MODEL, 2 * XATTN_HEADS * XATTN_DIM), D_MODEL),
        "xattn_w_o": nrm(ks[20], (DEPTH, XATTN_HEADS * XATTN_DIM, D_MODEL), XATTN_HEADS * XATTN_DIM, DEEPNORM_BETA),
        "ln_g": 1.0 + small(ks[21], (DEPTH, 3, D_MODEL)),
        "ln_b": small(ks[22], (DEPTH, 3, D_MODEL)),
        "router_w": nrm(ks[23], (DEPTH, D_MODEL, N_EXPERTS), D_MODEL),
        "router_b": small(ks[24], (DEPTH, N_EXPERTS)),
        "expert_w_gu": nrm(ks[25], (DEPTH, N_EXPERTS, D_MODEL, 2 * D_EXPERT), D_MODEL),
        "expert_b_gu": small(ks[26], (DEPTH, N_EXPERTS, 2 * D_EXPERT)),
        "expert_w_dn": nrm(ks[27], (DEPTH, N_EXPERTS, D_EXPERT, D_MODEL), D_EXPERT, DEEPNORM_BETA),
        "expert_b_dn": small(ks[28], (DEPTH, N_EXPERTS, D_MODEL)),
    }


def reference(x, mem, positions, w_in_even, nsa_cmp_pos, nsa_cmp_w1, nsa_cmp_w2, sgu_ln_g, sgu_ln_b,
              sgu_w, sgu_b, w_out_even, w_in_odd, hgrn_lb_logits, hgrn_norm_g, conv_w, conv_b,
              w_out_odd, xattn_w_q, xattn_w_kv, xattn_w_o, ln_g, ln_b, router_w, router_b,
              expert_w_gu, expert_b_gu, expert_w_dn, expert_b_dn):
    cos, sin = rope_tables(positions)
    lb_weights = jax.nn.softmax(hgrn_lb_logits.astype(jnp.float32), axis=0)
    even_split = [NSA_Q, NSA_Q + NSA_KV, NSA_Q + NSA_KV + NSA_GATES, NSA_Q + NSA_KV + NSA_GATES + SGU_W]
    for layer in range(DEPTH):
        j = layer // 2
        if layer % 2 == 0:
            proj = x @ w_in_even[j]
            q, kv, gts, u, v = jnp.split(proj, even_split, axis=-1)
            o_a = nsa_mixer(q, kv, gts, cos, sin, nsa_cmp_pos[j], nsa_cmp_w1[j], nsa_cmp_w2[j])
            o_b = sgu_mixer(u, v, sgu_ln_g[j], sgu_ln_b[j], sgu_w[j], sgu_b[j])
            mix = jnp.concatenate([o_a, o_b], axis=-1) @ w_out_even[j]
        else:
            proj = x @ w_in_odd[j]
            q, f_raw, i_in, g_out, h, b_gate, c_gate = jnp.split(proj, 7, axis=-1)
            lb = jnp.sum(lb_weights[1:layer + 1], axis=0)
            o_c = hgrn2_mixer(q, f_raw, i_in, g_out, lb, hgrn_norm_g[j])
            o_d = short_conv_mixer(h, b_gate, c_gate, conv_w[j], conv_b[j])
            mix = jnp.concatenate([o_c, o_d], axis=-1) @ w_out_odd[j]
        x = layer_norm(DEEPNORM_ALPHA * x + mix, ln_g[layer, 0], ln_b[layer, 0])
        xa = memory_cross_attention(x, mem, xattn_w_q[layer], xattn_w_kv[layer], xattn_w_o[layer])
        x = layer_norm(DEEPNORM_ALPHA * x + xa, ln_g[layer, 1], ln_b[layer, 1])
        ff = moe_ffn(x, router_w[layer], router_b[layer], expert_w_gu[layer], expert_b_gu[layer],
                     expert_w_dn[layer], expert_b_dn[layer])
        x = layer_norm(DEEPNORM_ALPHA * x + ff, ln_g[layer, 2], ln_b[layer, 2])
    return x
```

```python
import functools

import numpy as np
import jax
import jax.numpy as jnp
from jax import lax
from jax.experimental import pallas as pl
from jax.experimental.pallas import tpu as pltpu

F32 = jnp.float32
BF16 = jnp.bfloat16
I32 = jnp.int32

D_MODEL = 1024
DEPTH = 2
HEAD_DIM = 64
NSA_HEADS = 8
NSA_KV_HEADS = 2
NSA_GROUP = NSA_HEADS // NSA_KV_HEADS
CMP_BLOCK = 32
CMP_STRIDE = 16
CMP_HIDDEN = 256
SLC_BLOCK = 64
SLC_TOP = 16
WINDOW = 512
SLC_FORCED_SCORE = 1e4
SGU_GROUPS = 4
SGU_CH = 128
SGU_CHUNK = 128
HGRN_HEADS = 4
HGRN_DK = 128
HGRN_CHUNK = 64
CONV_CH = 512
XATTN_HEADS = 4
XATTN_DIM = D_MODEL // XATTN_HEADS
N_EXPERTS = 32
TOP_K = 4
D_EXPERT = D_MODEL
SWIGLU_LIMIT = 7.0
SWIGLU_ALPHA = 1.702
ROPE_THETA = 10000.0
LN_EPS = 1e-5
RMS_EPS = 1e-6
NEG_INF = -1e30
DEEPNORM_ALPHA = (2 * DEPTH) ** 0.25

NSA_Q = NSA_HEADS * HEAD_DIM
SGU_W = SGU_GROUPS * SGU_CH
HGRN_W = HGRN_HEADS * HGRN_DK

VMEM_LIMIT_BYTES = 56 * 1024 * 1024
LANES = 128

PROJ_TM = 512
NSA_TQ = 128
NSA_KT = 256
MOE_BM = 256
ROUTER_TM = 512


def _params(*sem):
    return pltpu.CompilerParams(dimension_semantics=sem, vmem_limit_bytes=VMEM_LIMIT_BYTES)


def _dot(a, b):
    return jnp.dot(a, b, preferred_element_type=F32)


def _dot_nt(a, b):
    return lax.dot_general(a, b, (((1,), (1,)), ((), ())), preferred_element_type=F32)


def _dot_tn(a, b):
    return lax.dot_general(a, b, (((0,), (0,)), ((), ())), preferred_element_type=F32)


def _gelu(x):
    return 0.5 * x * (1.0 + jnp.tanh(np.sqrt(2.0 / np.pi).astype(np.float32) * (x + 0.044715 * (x * x * x))))


def _sigmoid(x):
    return 1.0 / (1.0 + jnp.exp(-x))


def _layer_norm(y, g, b):
    mu = jnp.mean(y, axis=-1, keepdims=True)
    d = y - mu
    var = jnp.mean(d * d, axis=-1, keepdims=True)
    return d * lax.rsqrt(var + LN_EPS) * g + b


def _rope_kernel(pos_ref, inv_ref, cos_ref, sin_ref):
    ang = pos_ref[...].astype(F32) * inv_ref[...]
    cos_ref[...] = jnp.cos(ang)
    sin_ref[...] = jnp.sin(ang)


def rope_tables_t(positions):
    t = positions.size
    inv = 1.0 / (ROPE_THETA ** (jnp.arange(0, HEAD_DIM, 2, dtype=F32) / HEAD_DIM))
    tn = min(t, 4096)
    half = HEAD_DIM // 2
    return pl.pallas_call(
        _rope_kernel,
        grid=(t // tn,),
        in_specs=[pl.BlockSpec((1, tn), lambda i: (0, i)),
                  pl.BlockSpec((half, 1), lambda i: (0, 0))],
        out_specs=[pl.BlockSpec((half, tn), lambda i: (0, i))] * 2,
        out_shape=[jax.ShapeDtypeStruct((half, t), F32)] * 2,
        compiler_params=_params("parallel"),
        name="rope_tables",
    )(positions.reshape(1, t), inv.reshape(half, 1))


def _mm_kernel(x_ref, w_ref, o_ref):
    o_ref[...] = _dot(x_ref[...].astype(BF16), w_ref[...]).astype(o_ref.dtype)


def matmul(x, w, out_dtype, tm):
    m, k = x.shape
    n = w.shape[1]
    return pl.pallas_call(
        _mm_kernel,
        grid=(m // tm,),
        in_specs=[pl.BlockSpec((tm, k), lambda i: (i, 0)),
                  pl.BlockSpec((k, n), lambda i: (0, 0))],
        out_specs=pl.BlockSpec((tm, n), lambda i: (i, 0)),
        out_shape=jax.ShapeDtypeStruct((m, n), out_dtype),
        compiler_params=_params("parallel"),
        name="matmul",
    )(x, w)


def _outproj_ln_kernel(a_ref, b_ref, x_ref, wa_ref, wb_ref, g_ref, beta_ref, o_ref):
    mix = _dot(a_ref[...].astype(BF16), wa_ref[...]) + _dot(b_ref[...].astype(BF16), wb_ref[...])
    o_ref[...] = _layer_norm(DEEPNORM_ALPHA * x_ref[...] + mix, g_ref[...], beta_ref[...])


def outproj_ln(a, b, x, w_out, g, beta):
    t, d = x.shape
    na, nb = a.shape[1], b.shape[1]
    tm = PROJ_TM
    wa = w_out[:na].astype(BF16)
    wb = w_out[na:].astype(BF16)
    return pl.pallas_call(
        _outproj_ln_kernel,
        grid=(t // tm,),
        in_specs=[pl.BlockSpec((tm, na), lambda i: (i, 0)),
                  pl.BlockSpec((tm, nb), lambda i: (i, 0)),
                  pl.BlockSpec((tm, d), lambda i: (i, 0)),
                  pl.BlockSpec((na, d), lambda i: (0, 0)),
                  pl.BlockSpec((nb, d), lambda i: (0, 0)),
                  pl.BlockSpec((1, d), lambda i: (0, 0)),
                  pl.BlockSpec((1, d), lambda i: (0, 0))],
        out_specs=pl.BlockSpec((tm, d), lambda i: (i, 0)),
        out_shape=jax.ShapeDtypeStruct((t, d), F32),
        compiler_params=_params("parallel"),
        name="outproj_ln",
    )(a, b, x, wa, wb, g.reshape(1, d), beta.reshape(1, d))


def _xattn_kernel(x_ref, wq_ref, kv_ref, wo_ref, g_ref, beta_ref, o_ref):
    x = x_ref[...]
    q = _dot(x.astype(BF16), wq_ref[...])
    hw = XATTN_HEADS * XATTN_DIM
    heads = []
    for h in range(XATTN_HEADS):
        lo = h * XATTN_DIM
        qh = q[:, lo:lo + XATTN_DIM].astype(BF16)
        kh = kv_ref[0, :, lo:lo + XATTN_DIM]
        vh = kv_ref[0, :, hw + lo:hw + lo + XATTN_DIM]
        s = _dot_nt(qh, kh) * (XATTN_DIM ** -0.5)
        m = jnp.max(s, axis=-1, keepdims=True)
        e = jnp.exp(s - m)
        p = e / jnp.sum(e, axis=-1, keepdims=True)
        heads.append(_dot(p.astype(BF16), vh))
    o = jnp.concatenate(heads, axis=-1)
    xa = _dot(o.astype(BF16), wo_ref[...])
    o_ref[...] = _layer_norm(DEEPNORM_ALPHA * x + xa, g_ref[...], beta_ref[...])


def xattn_ln(x, kv, w_q, w_o, g, beta, seq):
    t, d = x.shape
    tm = PROJ_TM
    per_b = seq // tm
    mlen = kv.shape[1]
    return pl.pallas_call(
        _xattn_kernel,
        grid=(t // tm,),
        in_specs=[pl.BlockSpec((tm, d), lambda i: (i, 0)),
                  pl.BlockSpec((d, d), lambda i: (0, 0)),
                  pl.BlockSpec((1, mlen, 2 * d), lambda i: (i // per_b, 0, 0)),
                  pl.BlockSpec((d, d), lambda i: (0, 0)),
                  pl.BlockSpec((1, d), lambda i: (0, 0)),
                  pl.BlockSpec((1, d), lambda i: (0, 0))],
        out_specs=pl.BlockSpec((tm, d), lambda i: (i, 0)),
        out_shape=jax.ShapeDtypeStruct((t, d), F32),
        compiler_params=_params("parallel"),
        name="xattn_ln",
    )(x, w_q.astype(BF16), kv, w_o.astype(BF16), g.reshape(1, d), beta.reshape(1, d))


def _router_kernel(x_ref, wt_ref, b_ref, tri_ref, te_ref, gate_ref, pos_ref, cnt_ref, carry_ref):
    i = pl.program_id(0)

    @pl.when(i == 0)
    def _():
        carry_ref[...] = jnp.zeros_like(carry_ref)

    tm = x_ref.shape[0]
    logits = _dot_nt(wt_ref[...], x_ref[...].astype(BF16)) + b_ref[...]
    e_iota = lax.broadcasted_iota(I32, (N_EXPERTS, tm), 0)
    work = logits
    vals, idxs, hots = [], [], []
    for _ in range(TOP_K):
        m = jnp.max(work, axis=0, keepdims=True)
        idx = jnp.min(jnp.where(work == m, e_iota, N_EXPERTS), axis=0, keepdims=True)
        hot = e_iota == idx
        vals.append(m)
        idxs.append(idx)
        hots.append(hot)
        work = jnp.where(hot, -jnp.inf, work)
    exps = [jnp.exp(v - vals[0]) for v in vals]
    den = exps[0] + exps[1] + exps[2] + exps[3]
    gate_ref[...] = jnp.concatenate([e / den for e in exps], axis=0)
    te_ref[...] = jnp.concatenate(idxs, axis=0)

    hot_all = jnp.zeros((N_EXPERTS, tm), F32)
    for hot in hots:
        hot_all = hot_all + jnp.where(hot, 1.0, 0.0)
    before = _dot(hot_all.astype(BF16), tri_ref[...]) + carry_ref[:, 0:1]
    pos_ref[...] = jnp.concatenate(
        [jnp.sum(jnp.where(hot, before, 0.0), axis=0, keepdims=True) for hot in hots], axis=0).astype(I32)
    carry_ref[...] = carry_ref[...] + jnp.sum(hot_all, axis=1, keepdims=True)
    cnt_ref[...] = carry_ref[...]


def moe_route(x, w_router, b_router):
    t, d = x.shape
    tm = ROUTER_TM
    tri = jnp.asarray(np.triu(np.ones((tm, tm), np.float32), 1), BF16)
    return pl.pallas_call(
        _router_kernel,
        grid=(t // tm,),
        in_specs=[pl.BlockSpec((tm, d), lambda i: (i, 0)),
                  pl.BlockSpec((N_EXPERTS, d), lambda i: (0, 0)),
                  pl.BlockSpec((N_EXPERTS, 1), lambda i: (0, 0)),
                  pl.BlockSpec((tm, tm), lambda i: (0, 0))],
        out_specs=[pl.BlockSpec((TOP_K, tm), lambda i: (0, i)),
                   pl.BlockSpec((TOP_K, tm), lambda i: (0, i)),
                   pl.BlockSpec((TOP_K, tm), lambda i: (0, i)),
                   pl.BlockSpec((N_EXPERTS, LANES), lambda i: (0, 0))],
        out_shape=[jax.ShapeDtypeStruct((TOP_K, t), I32),
                   jax.ShapeDtypeStruct((TOP_K, t), F32),
                   jax.ShapeDtypeStruct((TOP_K, t), I32),
                   jax.ShapeDtypeStruct((N_EXPERTS, LANES), F32)],
        scratch_shapes=[pltpu.VMEM((N_EXPERTS, LANES), F32)],
        compiler_params=_params("arbitrary"),
        name="moe_router",
    )(x, w_router.T.astype(BF16), b_router.reshape(N_EXPERTS, 1), tri)


def _expert_kernel(blk_e_ref, blk_nv_ref, nused_ref, src_ref, srcn_ref, x_hbm, wgu_ref, bgu_ref,
                   wdn_ref, bdn_ref, y_hbm, xbuf, ybuf, gsem, ssem, *, n_tokens):
    i = pl.program_id(0)
    nused = nused_ref[0]
    slot = lax.rem(i, 2)

    def gather_copy(tok, r, s):
        return pltpu.make_async_copy(x_hbm.at[pl.ds(tok, 1)], xbuf.at[s, pl.ds(r, 1)], gsem.at[s])

    def scatter_copy(row, r):
        return pltpu.make_async_copy(ybuf.at[pl.ds(r, 1)], y_hbm.at[pl.ds(row, 1)], ssem.at[0])

    def start_gather(idx_ref, n, s):
        def body(r, c):
            gather_copy(lax.rem(idx_ref[0, 0, r], n_tokens), r, s).start()
            return c
        lax.fori_loop(0, n, body, 0)

    def wait_gather(n, s):
        def body(r, c):
            gather_copy(0, r, s).wait()
            return c
        lax.fori_loop(0, n, body, 0)

    def wait_scatter(n):
        def body(r, c):
            scatter_copy(0, r).wait()
            return c
        lax.fori_loop(0, n, body, 0)

    @pl.when(i == 0)
    def _():
        xbuf[...] = jnp.zeros_like(xbuf)
        start_gather(src_ref, blk_nv_ref[0], 0)

    @pl.when(i + 1 < nused)
    def _():
        start_gather(srcn_ref, blk_nv_ref[i + 1], 1 - slot)

    @pl.when(i < nused)
    def _():
        nv = blk_nv_ref[i]
        wait_gather(nv, slot)
        xb = xbuf[slot].astype(BF16)
        h = _dot(xb, wgu_ref[0]) + bgu_ref[0]
        h_gate = jnp.minimum(h[:, :D_EXPERT], SWIGLU_LIMIT)
        h_up = jnp.clip(h[:, D_EXPERT:], -SWIGLU_LIMIT, SWIGLU_LIMIT)
        act = (h_up + 1.0) * (h_gate * _sigmoid(h_gate * SWIGLU_ALPHA))
        y = _dot(act.astype(BF16), wdn_ref[0]) + bdn_ref[0]

        @pl.when(i > 0)
        def _():
            wait_scatter(blk_nv_ref[i - 1])

        ybuf[...] = y

        def body(r, c):
            scatter_copy(src_ref[0, 0, r], r).start()
            return c
        lax.fori_loop(0, nv, body, 0)

        @pl.when(i == nused - 1)
        def _():
            wait_scatter(nv)


def moe_experts(x, src, blk_e, blk_nv, nused, w_gu, b_gu, w_dn, b_dn):
    t, d = x.shape
    bm = MOE_BM
    nb = src.shape[0]
    grid_spec = pltpu.PrefetchScalarGridSpec(
        num_scalar_prefetch=3,
        grid=(nb,),
        in_specs=[
            pl.BlockSpec((1, 1, bm), lambda i, *_: (i, 0, 0), memory_space=pltpu.SMEM),
            pl.BlockSpec((1, 1, bm), lambda i, *_: (jnp.minimum(i + 1, nb - 1), 0, 0),
                         memory_space=pltpu.SMEM),
            pl.BlockSpec(memory_space=pl.ANY),
            pl.BlockSpec((1, d, 2 * D_EXPERT), lambda i, be, *_: (be[i], 0, 0)),
            pl.BlockSpec((1, 1, 2 * D_EXPERT), lambda i, be, *_: (be[i], 0, 0)),
            pl.BlockSpec((1, D_EXPERT, d), lambda i, be, *_: (be[i], 0, 0)),
            pl.BlockSpec((1, 1, d), lambda i, be, *_: (be[i], 0, 0)),
        ],
        out_specs=pl.BlockSpec(memory_space=pl.ANY),
        scratch_shapes=[pltpu.VMEM((2, bm, d), F32),
                        pltpu.VMEM((bm, d), F32),
                        pltpu.SemaphoreType.DMA((2,)),
                        pltpu.SemaphoreType.DMA((1,))],
    )
    return pl.pallas_call(
        functools.partial(_expert_kernel, n_tokens=t),
        grid_spec=grid_spec,
        out_shape=jax.ShapeDtypeStruct((TOP_K * t, d), F32),
        compiler_params=_params("arbitrary"),
        name="moe_experts",
    )(blk_e, blk_nv, nused, src, src, x, w_gu, b_gu.reshape(N_EXPERTS, 1, -1), w_dn,
      b_dn.reshape(N_EXPERTS, 1, -1))


def _combine_ln_kernel(y_ref, gate_ref, x_ref, g_ref, beta_ref, o_ref):
    ff = gate_ref[:, 0:1] * y_ref[0]
    for k in range(1, TOP_K):
        ff = ff + gate_ref[:, k:k + 1] * y_ref[k]
    o_ref[...] = _layer_norm(DEEPNORM_ALPHA * x_ref[...] + ff, g_ref[...], beta_ref[...])


def moe_combine_ln(y, gate_t, x, g, beta):
    t, d = x.shape
    tm = PROJ_TM
    return pl.pallas_call(
        _combine_ln_kernel,
        grid=(t // tm,),
        in_specs=[pl.BlockSpec((TOP_K, tm, d), lambda i: (0, i, 0)),
                  pl.BlockSpec((tm, TOP_K), lambda i: (i, 0)),
                  pl.BlockSpec((tm, d), lambda i: (i, 0)),
                  pl.BlockSpec((1, d), lambda i: (0, 0)),
                  pl.BlockSpec((1, d), lambda i: (0, 0))],
        out_specs=pl.BlockSpec((tm, d), lambda i: (i, 0)),
        out_shape=jax.ShapeDtypeStruct((t, d), F32),
        compiler_params=_params("parallel"),
        name="moe_combine_ln",
    )(y.reshape(TOP_K, t, d), gate_t, x, g.reshape(1, d), beta.reshape(1, d))


def moe_ln(x, w_router, b_router, w_gu, b_gu, w_dn, b_dn, g, beta):
    t, d = x.shape
    bm = MOE_BM
    top_e, gate, pos, cnt = moe_route(x, w_router, b_router)
    counts = cnt[:, 0].astype(I32)
    padded = (counts + bm - 1) // bm * bm
    pend = jnp.cumsum(padded)
    pstart = pend - padded
    nb = (t * TOP_K) // bm + N_EXPERTS
    blk_row = jnp.arange(nb, dtype=I32) * bm
    blk_e = jnp.minimum(jnp.searchsorted(pend, blk_row, side='right'), N_EXPERTS - 1).astype(I32)
    blk_nv = jnp.clip(counts[blk_e] - (blk_row - pstart[blk_e]), 0, bm).astype(I32)
    nused = (pend[-1] // bm).astype(I32).reshape(1)
    dest = pstart[top_e] + pos
    flat = jnp.arange(TOP_K * t, dtype=I32).reshape(TOP_K, t)
    src = jnp.zeros((nb * bm,), I32).at[dest.reshape(-1)].set(flat.reshape(-1)).reshape(nb, 1, bm)
    y = moe_experts(x, src, blk_e, blk_nv, nused, w_gu.astype(BF16), b_gu, w_dn.astype(BF16), b_dn)
    return moe_combine_ln(y, gate.T, x, g, beta)


def _sgu_kernel(u_ref, v_ref, lng_ref, lnb_ref, w_ref, bs_ref, o_ref):
    tm = u_ref.shape[0]
    row = lax.broadcasted_iota(I32, (SGU_CHUNK, SGU_CHUNK), 0)
    col = lax.broadcasted_iota(I32, (SGU_CHUNK, SGU_CHUNK), 1)
    causal = row >= col
    for g in range(SGU_GROUPS):
        lo = g * SGU_CH
        vg = _gelu(v_ref[:, lo:lo + SGU_CH])
        vg = _layer_norm(vg, lng_ref[g:g + 1, :], lnb_ref[g:g + 1, :]).astype(BF16)
        wg = jnp.where(causal, w_ref[g], 0.0).astype(BF16)
        bias = bs_ref[:, g:g + 1]
        for n in range(tm // SGU_CHUNK):
            r0 = n * SGU_CHUNK
            mix = _dot(wg, vg[r0:r0 + SGU_CHUNK]) + bias
            o_ref[r0:r0 + SGU_CHUNK, lo:lo + SGU_CH] = _gelu(u_ref[r0:r0 + SGU_CHUNK, lo:lo + SGU_CH]) * mix


def sgu_mixer(u, v, ln_g, ln_b, w_s, b_s):
    t, w = u.shape
    tm = PROJ_TM
    return pl.pallas_call(
        _sgu_kernel,
        grid=(t // tm,),
        in_specs=[pl.BlockSpec((tm, w), lambda i: (i, 0)),
                  pl.BlockSpec((tm, w), lambda i: (i, 0)),
                  pl.BlockSpec((SGU_GROUPS, SGU_CH), lambda i: (0, 0)),
                  pl.BlockSpec((SGU_GROUPS, SGU_CH), lambda i: (0, 0)),
                  pl.BlockSpec((SGU_GROUPS, SGU_CHUNK, SGU_CHUNK), lambda i: (0, 0, 0)),
                  pl.BlockSpec((SGU_CHUNK, SGU_GROUPS), lambda i: (0, 0))],
        out_specs=pl.BlockSpec((tm, w), lambda i: (i, 0)),
        out_shape=jax.ShapeDtypeStruct((t, w), F32),
        compiler_params=_params("parallel"),
        name="sgu_mixer",
    )(u, v, ln_g, ln_b, w_s, b_s.T)


def _conv_kernel(h_ref, bg_ref, cg_ref, w_ref, b_ref, o_ref, carry_ref):
    @pl.when(pl.program_id(1) == 0)
    def _():
        carry_ref[...] = jnp.zeros_like(carry_ref)

    z = cg_ref[...] * h_ref[...]
    tm = z.shape[0]
    row = lax.broadcasted_iota(I32, z.shape, 0)
    prev = carry_ref[...]
    z1 = jnp.where(row == 0, prev[7:8, :], pltpu.roll(z, 1, 0))
    z2 = jnp.where(row == 0, prev[6:7, :], jnp.where(row == 1, prev[7:8, :], pltpu.roll(z, 2, 0)))
    y = w_ref[0:1, :] * z2 + w_ref[1:2, :] * z1 + w_ref[2:3, :] * z + b_ref[...]
    o_ref[...] = bg_ref[...] * y
    carry_ref[...] = z[tm - 8:tm, :]


def conv_mixer(proj, col0, conv_w, conv_b, batch, seq):
    tm = PROJ_TM
    per_b = seq // tm
    c = CONV_CH

    def spec(j):
        return pl.BlockSpec((tm, c), lambda b, i: (b * per_b + i, j))

    return pl.pallas_call(
        _conv_kernel,
        grid=(batch, per_b),
        in_specs=[spec(col0), spec(col0 + 1), spec(col0 + 2),
                  pl.BlockSpec((3, c), lambda b, i: (0, 0)),
                  pl.BlockSpec((1, c), lambda b, i: (0, 0))],
        out_specs=pl.BlockSpec((tm, c), lambda b, i: (b * per_b + i, 0)),
        out_shape=jax.ShapeDtypeStruct((batch * seq, c), F32),
        scratch_shapes=[pltpu.VMEM((8, c), F32)],
        compiler_params=_params("arbitrary", "arbitrary"),
        name="conv_mixer",
    )(proj, proj, proj, conv_w, conv_b.reshape(1, c))


def _split3(x):
    hi = x.astype(BF16)
    r1 = x - hi.astype(F32)
    mid = r1.astype(BF16)
    lo = (r1 - mid.astype(F32)).astype(BF16)
    return hi, mid, lo


def _hgrn_kernel(q_ref, f_ref, i_ref, g_ref, lbl_ref, ng_ref, o_ref, state_ref, *, layer):
    c = HGRN_CHUNK
    n_chunks = q_ref.shape[0] // c
    lw = lbl_ref[...]
    lw = jnp.exp(lw - jnp.max(lw, axis=0, keepdims=True))
    lw = lw / jnp.sum(lw, axis=0, keepdims=True)
    lb = jnp.sum(lw[1:layer + 1], axis=0, keepdims=True)
    row = lax.broadcasted_iota(I32, (c, c), 0)
    col = lax.broadcasted_iota(I32, (c, c), 1)
    tril = jnp.where(row >= col, 1.0, 0.0).astype(BF16)
    sub = lax.broadcasted_iota(I32, (8, HGRN_DK), 0)
    state_ref[...] = jnp.zeros_like(state_ref)

    def chunk(ci, carry):
        r0 = pl.multiple_of(ci * c, c)
        q = q_ref[pl.ds(r0, c), :]
        z = f_ref[pl.ds(r0, c), :]
        v = i_ref[pl.ds(r0, c), :]
        k = (1.0 - lb) * _sigmoid(-z)
        log_f = jnp.log1p(-k)
        hi, mid, lo = _split3(log_f)
        b = _dot(tril, hi) + _dot(tril, mid) + _dot(tril, lo)
        state_t = state_ref[...]
        o = _dot_nt((q * jnp.exp(b)).astype(BF16), state_t.astype(BF16))

        acc = [o[8 * j:8 * j + 8] for j in range(c // 8)]
        for s in range(c):
            bs = b[s:s + 1, :]
            ks = k[s:s + 1, :]
            vs = v[s:s + 1, :]
            j0 = s // 8
            for j in range(j0, c // 8):
                dlt = b[8 * j:8 * j + 8] - bs
                if j == j0:
                    dlt = jnp.where(sub >= s - 8 * j0, dlt, NEG_INF)
                a = jnp.sum(q[8 * j:8 * j + 8] * ks * jnp.exp(dlt), axis=-1, keepdims=True)
                acc[j] = acc[j] + a * vs
        o = jnp.concatenate(acc, axis=0)

        b_last = b[c - 1:c, :]
        kd = (k * jnp.exp(b_last - b)).astype(BF16)
        state_ref[...] = jnp.exp(b_last) * state_t + _dot_tn(v.astype(BF16), kd)
        o = o * lax.rsqrt(jnp.mean(o * o, axis=-1, keepdims=True) + RMS_EPS) * ng_ref[...]
        gg = g_ref[pl.ds(r0, c), :]
        o_ref[pl.ds(r0, c), :] = o * (gg * _sigmoid(gg))
        return carry

    lax.fori_loop(0, n_chunks, chunk, 0)


def hgrn2_mixer(proj, lb_logits, norm_g, layer, batch, seq):
    dk = HGRN_DK
    h = HGRN_HEADS

    def spec(grp):
        return pl.BlockSpec((seq, dk), lambda b, hh: (b, grp * h + hh))

    return pl.pallas_call(
        functools.partial(_hgrn_kernel, layer=layer),
        grid=(batch, h),
        in_specs=[spec(0), spec(1), spec(2), spec(3),
                  pl.BlockSpec((DEPTH, dk), lambda b, hh: (0, hh)),
                  pl.BlockSpec((1, dk), lambda b, hh: (0, hh))],
        out_specs=pl.BlockSpec((seq, dk), lambda b, hh: (b, hh)),
        out_shape=jax.ShapeDtypeStruct((batch * seq, h * dk), F32),
        scratch_shapes=[pltpu.VMEM((dk, dk), F32)],
        compiler_params=_params("parallel", "parallel"),
        name="hgrn2_mixer",
    )(proj, proj, proj, proj, lb_logits, norm_g.reshape(1, h * dk))


EVEN_T_ROWS = NSA_Q + 4 * 2 * HEAD_DIM + 32
EVEN_S_COLS = 4 * LANES + 2 * SGU_W


def _even_proj_kernel(x_ref, wt_ref, ws_ref, cos_ref, sin_ref, qt_ref, qrt_ref, ks_ref, kw_ref,
                      vs_ref, vw_ref, gt_ref, kvc_ref, u_ref, v_ref, *, per_b):
    tm = x_ref.shape[0]
    xb = x_ref[...].astype(BF16)
    st = _dot_nt(wt_ref[...], xb)
    cos = cos_ref[...]
    sin = sin_ref[...]
    half = HEAD_DIM // 2
    scale = HEAD_DIM ** -0.5

    def rope(blk):
        x1, x2 = blk[:half], blk[half:]
        return jnp.concatenate([x1 * cos - x2 * sin, x2 * cos + x1 * sin], axis=0)

    for hh in range(NSA_HEADS):
        blk = st[hh * HEAD_DIM:(hh + 1) * HEAD_DIM]
        qt_ref[hh * HEAD_DIM:(hh + 1) * HEAD_DIM, :] = (blk * scale).astype(BF16)
        qrt_ref[hh * HEAD_DIM:(hh + 1) * HEAD_DIM, :] = (rope(blk) * scale).astype(BF16)

    kk = jnp.concatenate([rope(st[NSA_Q + j * HEAD_DIM:NSA_Q + (j + 1) * HEAD_DIM]) for j in range(4)], axis=0)
    kk = kk.T
    pos = lax.rem(pl.program_id(0), per_b) * tm + lax.broadcasted_iota(I32, (tm, LANES), 0)
    lane = lax.broadcasted_iota(I32, (tm, LANES), 1)
    member = jnp.where(lane == pos // SLC_BLOCK, 1.0, 0.0)
    ks_ref[...] = jnp.concatenate([kk[:, :LANES], member], axis=1).astype(BF16)
    kw_ref[...] = kk[:, LANES:].astype(BF16)
    v0 = NSA_Q + 4 * HEAD_DIM
    for j in range(tm // NSA_KT):
        vs_ref[j] = st[v0:v0 + LANES, j * NSA_KT:(j + 1) * NSA_KT].astype(BF16)
    for j in range(tm // LANES):
        vw_ref[j] = st[v0 + LANES:v0 + 2 * LANES, j * LANES:(j + 1) * LANES].astype(BF16)
    gt_ref[...] = _sigmoid(st[v0 + 2 * LANES:v0 + 2 * LANES + 32])

    ss = _dot(xb, ws_ref[...])
    for j in range(4):
        kvc_ref[j] = ss[:, j * LANES:j * LANES + HEAD_DIM]
    u_ref[...] = ss[:, 4 * LANES:4 * LANES + SGU_W]
    v_ref[...] = ss[:, 4 * LANES + SGU_W:]


def even_proj(x, w_in, cos_t, sin_t, seq):
    t, d = x.shape
    tm = PROJ_TM
    per_b = seq // tm
    hd = HEAD_DIM
    kv0 = NSA_Q

    def kvcols(i):
        return w_in[:, kv0 + i * 2 * hd:kv0 + (i + 1) * 2 * hd]

    g0 = kv0 + 6 * 2 * hd
    n_gates = 3 * NSA_HEADS
    wt = jnp.concatenate([w_in[:, :NSA_Q], kvcols(2), kvcols(4), kvcols(3), kvcols(5),
                          w_in[:, g0:g0 + n_gates], jnp.zeros((d, 32 - n_gates), F32)], axis=1).T.astype(BF16)
    zpad = jnp.zeros((d, LANES - hd), F32)
    cmp_cols = []
    for i in (0, 1):
        for g in range(NSA_KV_HEADS):
            cmp_cols += [w_in[:, kv0 + i * 2 * hd + g * hd:kv0 + i * 2 * hd + (g + 1) * hd], zpad]
    ws = jnp.concatenate(cmp_cols + [w_in[:, g0 + n_gates:]], axis=1).astype(BF16)
    half = hd // 2
    tok = lambda i: (i, 0)
    tok_t = lambda i: (0, i)
    return pl.pallas_call(
        functools.partial(_even_proj_kernel, per_b=per_b),
        grid=(t // tm,),
        in_specs=[pl.BlockSpec((tm, d), tok),
                  pl.BlockSpec((EVEN_T_ROWS, d), lambda i: (0, 0)),
                  pl.BlockSpec((d, EVEN_S_COLS), lambda i: (0, 0)),
                  pl.BlockSpec((half, tm), tok_t),
                  pl.BlockSpec((half, tm), tok_t)],
        out_specs=[pl.BlockSpec((NSA_Q, tm), tok_t),
                   pl.BlockSpec((NSA_Q, tm), tok_t),
                   pl.BlockSpec((tm, 2 * LANES), tok),
                   pl.BlockSpec((tm, LANES), tok),
                   pl.BlockSpec((tm // NSA_KT, LANES, NSA_KT), lambda i: (i, 0, 0)),
                   pl.BlockSpec((tm // LANES, LANES, LANES), lambda i: (i, 0, 0)),
                   pl.BlockSpec((32, tm), tok_t),
                   pl.BlockSpec((4, tm, hd), lambda i: (0, i, 0)),
                   pl.BlockSpec((tm, SGU_W), tok),
                   pl.BlockSpec((tm, SGU_W), tok)],
        out_shape=[jax.ShapeDtypeStruct((NSA_Q, t), BF16),
                   jax.ShapeDtypeStruct((NSA_Q, t), BF16),
                   jax.ShapeDtypeStruct((t, 2 * LANES), BF16),
                   jax.ShapeDtypeStruct((t, LANES), BF16),
                   jax.ShapeDtypeStruct((t // NSA_KT, LANES, NSA_KT), BF16),
                   jax.ShapeDtypeStruct((t // LANES, LANES, LANES), BF16),
                   jax.ShapeDtypeStruct((32, t), F32),
                   jax.ShapeDtypeStruct((4, t, hd), F32),
                   jax.ShapeDtypeStruct((t, SGU_W), F32),
                   jax.ShapeDtypeStruct((t, SGU_W), F32)],
        compiler_params=_params("parallel"),
        name="even_proj",
    )(x, wt, ws, cos_t, sin_t)


def _compress_kernel(kvc_ref, pos_ref, w1_ref, w2_ref, kc_ref, vct_ref):
    ncp = kc_ref.shape[1]
    assert CMP_BLOCK == 2 * CMP_STRIDE
    for i in range(2):
        outs = []
        for g in range(NSA_KV_HEADS):
            first = jnp.zeros((ncp, CMP_HIDDEN), F32)
            second = jnp.zeros((ncp, CMP_HIDDEN), F32)
            for l in range(CMP_STRIDE):
                rows = kvc_ref[i * NSA_KV_HEADS + g, pl.ds(l, ncp, stride=CMP_STRIDE), :]
                first = first + _dot((rows + pos_ref[i, l:l + 1, :]).astype(BF16), w1_ref[i, l])
                l2 = l + CMP_STRIDE
                second = second + _dot((rows + pos_ref[i, l2:l2 + 1, :]).astype(BF16), w1_ref[i, l2])
            hid = _gelu(first + pltpu.roll(second, ncp - 1, 0))
            outs.append(_dot(hid.astype(BF16), w2_ref[i]))
        if i == 0:
            kc_ref[0] = jnp.concatenate(outs, axis=1).astype(BF16)
        else:
            vct_ref[0] = jnp.concatenate(outs, axis=1).T.astype(BF16)


def nsa_compress(kvc, cmp_pos, cmp_w1, cmp_w2, batch, seq):
    hd = HEAD_DIM
    ncp = seq // CMP_STRIDE
    w1 = cmp_w1.reshape(2, CMP_BLOCK, hd, CMP_HIDDEN).astype(BF16)
    return pl.pallas_call(
        _compress_kernel,
        grid=(batch,),
        in_specs=[pl.BlockSpec((4, seq, hd), lambda b: (0, b, 0)),
                  pl.BlockSpec((2, CMP_BLOCK, hd), lambda b: (0, 0, 0)),
                  pl.BlockSpec((2, CMP_BLOCK, hd, CMP_HIDDEN), lambda b: (0, 0, 0, 0)),
                  pl.BlockSpec((2, CMP_HIDDEN, hd), lambda b: (0, 0, 0))],
        out_specs=[pl.BlockSpec((1, ncp, 2 * hd), lambda b: (b, 0, 0)),
                   pl.BlockSpec((1, 2 * hd, ncp), lambda b: (b, 0, 0))],
        out_shape=[jax.ShapeDtypeStruct((batch, ncp, 2 * hd), BF16),
                   jax.ShapeDtypeStruct((batch, 2 * hd, ncp), BF16)],
        compiler_params=_params("parallel"),
        name="nsa_compress",
    )(kvc, cmp_pos, w1, cmp_w2.astype(BF16))


def _softmax_cols(s, mask):
    sm = jnp.where(mask, s, NEG_INF)
    m = jnp.max(sm, axis=0, keepdims=True)
    e = jnp.exp(sm - m)
    return jnp.where(mask, e / jnp.sum(e, axis=0, keepdims=True), 0.0)


def _nsa_kernel(qt_ref, qrt_ref, gt_ref, kc_ref, vct_ref, mct_ref, ks_ref, kw_ref, vs_ref, vw_ref, o_ref,
                *, n_top):
    tq = qt_ref.shape[1]
    hg = NSA_GROUP
    hd = HEAD_DIM
    w = hg * tq
    ncp = kc_ref.shape[1]
    n_slc = mct_ref.shape[0]
    s0 = pl.program_id(1) * tq
    t_lane = s0 + lax.rem(lax.broadcasted_iota(I32, (1, w), 1), tq)
    t_q = s0 + lax.broadcasted_iota(I32, (1, tq), 1)
    zeros_g = jnp.zeros((hd, w), BF16)
    n_win = WINDOW // LANES + tq // LANES
    wt0 = jnp.maximum(s0 // LANES - WINDOW // LANES, 0)

    heads = []
    for g in range(NSA_KV_HEADS):
        def grp(ref):
            rows = jnp.concatenate([ref[(g * hg + h) * hd:(g * hg + h + 1) * hd, :] for h in range(hg)], axis=1)
            return jnp.concatenate([rows, zeros_g] if g == 0 else [zeros_g, rows], axis=0)

        q_raw = grp(qt_ref)
        q_rot = grp(qrt_ref)
        vrows = slice(g * hd, (g + 1) * hd)

        sc = _dot(kc_ref[0], q_raw)
        c_end = lax.broadcasted_iota(I32, (ncp, w), 0) * CMP_STRIDE + (CMP_BLOCK - 1)
        p = _softmax_cols(sc, c_end <= t_lane)
        o_cmp = _dot(vct_ref[0, vrows, :], p.astype(BF16))
        psum = p[:, 0:tq]
        for h in range(1, hg):
            psum = psum + p[:, h * tq:(h + 1) * tq]
        imp = _dot(mct_ref[...], psum.astype(BF16))

        j_blk = lax.broadcasted_iota(I32, (n_slc, tq), 0)
        cur = t_q // SLC_BLOCK
        forced = (j_blk == 0) | (j_blk == cur) | (j_blk == cur - 1)
        valid = j_blk <= cur
        score = jnp.where(forced, SLC_FORCED_SCORE, jnp.where(valid, imp, -1.0))
        nv = n_slc // 8
        sblk = [score[8 * v:8 * v + 8] for v in range(nv)]
        rank = [jnp.zeros((8, tq), F32) for _ in range(nv)]
        sub = lax.broadcasted_iota(I32, (8, tq), 0)
        for k in range(n_slc):
            sk = score[k:k + 1, :]
            kv_ = k // 8
            for v in range(nv):
                ge = jnp.where(sk >= sblk[v], 1.0, 0.0)
                gt = jnp.where(sk > sblk[v], 1.0, 0.0)
                if v > kv_:
                    beats = ge
                elif v < kv_:
                    beats = gt
                else:
                    beats = jnp.where(sub > k - 8 * kv_, ge, gt)
                rank[v] = rank[v] + beats
        rank = jnp.concatenate(rank, axis=0)
        bias = jnp.where((rank < n_top) & valid, 0.0, NEG_INF)
        bias = jnp.concatenate([bias] * hg, axis=1)
        if n_slc < 2 * hd:
            bias = jnp.concatenate([bias, jnp.zeros((2 * hd - n_slc, w), F32)], axis=0)
        q_aug = jnp.concatenate([q_rot, bias.astype(BF16)], axis=0)

        n_kv = (s0 + tq + NSA_KT - 1) // NSA_KT

        def slc_step(kt, carry):
            m, l, acc = carry
            k0 = pl.multiple_of(kt * NSA_KT, NSA_KT)
            s = _dot(ks_ref[pl.ds(k0, NSA_KT), :], q_aug)
            kpos = k0 + lax.broadcasted_iota(I32, (NSA_KT, w), 0)
            s = jnp.where(kpos <= t_lane, s, NEG_INF)
            m_new = jnp.maximum(m, jnp.max(s, axis=0, keepdims=True))
            alpha = jnp.exp(m - m_new)
            pp = jnp.exp(s - m_new)
            l = l * alpha + jnp.sum(pp, axis=0, keepdims=True)
            acc = acc * alpha + _dot(vs_ref[kt, vrows, :], pp.astype(BF16))
            return m_new, l, acc

        m0 = jnp.full((1, w), NEG_INF, F32)
        _, l_s, acc_s = lax.fori_loop(0, n_kv, slc_step, (m0, jnp.zeros((1, w), F32), jnp.zeros((hd, w), F32)))
        o_slc = acc_s / l_s

        kw0 = pl.multiple_of(wt0 * LANES, LANES)
        s = _dot(kw_ref[pl.ds(kw0, n_win * LANES), :], q_rot)
        dist = t_lane - (kw0 + lax.broadcasted_iota(I32, (n_win * LANES, w), 0))
        sm = jnp.where((dist >= 0) & (dist < WINDOW), s, NEG_INF)
        e = jnp.exp(sm - jnp.max(sm, axis=0, keepdims=True))
        den = jnp.sum(e, axis=0, keepdims=True)
        eb = e.astype(BF16)
        o_win = _dot(vw_ref[wt0, vrows, :], eb[0:LANES])
        for j in range(1, n_win):
            o_win = o_win + _dot(vw_ref[wt0 + j, vrows, :], eb[j * LANES:(j + 1) * LANES])
        o_win = o_win / den

        for h in range(hg):
            r = (g * hg + h) * 3
            cols = slice(h * tq, (h + 1) * tq)
            heads.append(gt_ref[r:r + 1, :] * o_cmp[:, cols] + gt_ref[r + 1:r + 2, :] * o_slc[:, cols]
                         + gt_ref[r + 2:r + 3, :] * o_win[:, cols])
    o_ref[...] = jnp.concatenate(heads, axis=0).T


def _cmp_to_slc_t(seq):
    ncp = seq // CMP_STRIDE
    ns = seq // SLC_BLOCK
    cs = np.arange(ncp)[None, :] * CMP_STRIDE
    ss = np.arange(ns)[:, None] * SLC_BLOCK
    ov = np.clip(np.minimum(cs + CMP_BLOCK, ss + SLC_BLOCK) - np.maximum(cs, ss), 0, None) / CMP_BLOCK
    ov[:, ncp - 1] = 0.0
    return jnp.asarray(ov, BF16)


def nsa_mixer(qt, qrt, gt, kc, vct, ks, kw, vs, vw, batch, seq):
    tq = NSA_TQ
    nq = seq // tq
    ncp = seq // CMP_STRIDE
    n_slc = seq // SLC_BLOCK
    assert n_slc <= 2 * HEAD_DIM and n_slc % 8 == 0 and seq >= WINDOW + tq
    col = lambda b, i: (0, b * nq + i)
    return pl.pallas_call(
        functools.partial(_nsa_kernel, n_top=min(SLC_TOP, n_slc)),
        grid=(batch, nq),
        in_specs=[pl.BlockSpec((NSA_Q, tq), col),
                  pl.BlockSpec((NSA_Q, tq), col),
                  pl.BlockSpec((32, tq), col),
                  pl.BlockSpec((1, ncp, 2 * HEAD_DIM), lambda b, i: (b, 0, 0)),
                  pl.BlockSpec((1, 2 * HEAD_DIM, ncp), lambda b, i: (b, 0, 0)),
                  pl.BlockSpec((n_slc, ncp), lambda b, i: (0, 0)),
                  pl.BlockSpec((seq, 2 * LANES), lambda b, i: (b, 0)),
                  pl.BlockSpec((seq, LANES), lambda b, i: (b, 0)),
                  pl.BlockSpec((seq // NSA_KT, LANES, NSA_KT), lambda b, i: (b, 0, 0)),
                  pl.BlockSpec((seq // LANES, LANES, LANES), lambda b, i: (b, 0, 0))],
        out_specs=pl.BlockSpec((tq, NSA_Q), lambda b, i: (b * nq + i, 0)),
        out_shape=jax.ShapeDtypeStruct((batch * seq, NSA_Q), F32),
        compiler_params=_params("parallel", "parallel"),
        name="nsa_mixer",
    )(qt, qrt, gt, kc, vct, _cmp_to_slc_t(seq), ks, kw, vs, vw)


def kernel(x, mem, positions, w_in_even, nsa_cmp_pos, nsa_cmp_w1, nsa_cmp_w2, sgu_ln_g, sgu_ln_b, sgu_w, sgu_b, w_out_even, w_in_odd, hgrn_lb_logits, hgrn_norm_g, conv_w, conv_b, w_out_odd, xattn_w_q, xattn_w_kv, xattn_w_o, ln_g, ln_b, router_w, router_b, expert_w_gu, expert_b_gu, expert_w_dn, expert_b_dn):
    batch, seq, d = x.shape
    t = batch * seq
    cos_t, sin_t = rope_tables_t(positions)
    mem2 = mem.reshape(-1, d)
    xf = x.reshape(t, d)
    for layer in range(DEPTH):
        j = layer // 2
        if layer % 2 == 0:
            qt, qrt, ks, kw, vs, vw, gt, kvc, u, v = even_proj(xf, w_in_even[j], cos_t, sin_t, seq)
            kc, vct = nsa_compress(kvc, nsa_cmp_pos[j], nsa_cmp_w1[j], nsa_cmp_w2[j], batch, seq)
            o_a = nsa_mixer(qt, qrt, gt, kc, vct, ks, kw, vs, vw, batch, seq)
            o_b = sgu_mixer(u, v, sgu_ln_g[j], sgu_ln_b[j], sgu_w[j], sgu_b[j])
            w_out = w_out_even[j]
        else:
            proj = matmul(xf, w_in_odd[j].astype(BF16), F32, PROJ_TM)
            o_a = hgrn2_mixer(proj, hgrn_lb_logits, hgrn_norm_g[j], layer, batch, seq)
            o_b = conv_mixer(proj, 4 * HGRN_W // CONV_CH, conv_w[j], conv_b[j], batch, seq)
            w_out = w_out_odd[j]
        xf = outproj_ln(o_a, o_b, xf, w_out, ln_g[layer, 0], ln_b[layer, 0])
        kv = matmul(mem2, xattn_w_kv[layer].astype(BF16), BF16, mem.shape[1]).reshape(batch, mem.shape[1], 2 * d)
        xf = xattn_ln(xf, kv, xattn_w_q[layer], xattn_w_o[layer], ln_g[layer, 1], ln_b[layer, 1], seq)
        xf = moe_ln(xf, router_w[layer], router_b[layer], expert_w_gu[layer], expert_b_gu[layer],
                    expert_w_dn[layer], expert_b_dn[layer], ln_g[layer, 2], ln_b[layer, 2])
    return xf.reshape(batch, seq, d)
```

```python
import functools

import numpy as np
import jax
import jax.numpy as jnp
from jax import lax
from jax.experimental import pallas as pl
from jax.experimental.pallas import tpu as pltpu

F32 = jnp.float32
BF16 = jnp.bfloat16
I32 = jnp.int32

D_MODEL = 1024
DEPTH = 2
HEAD_DIM = 64
NSA_HEADS = 8
NSA_KV_HEADS = 2
NSA_GROUP = NSA_HEADS // NSA_KV_HEADS
CMP_BLOCK = 32
CMP_STRIDE = 16
CMP_HIDDEN = 256
SLC_BLOCK = 64
SLC_TOP = 16
WINDOW = 512
SLC_FORCED_SCORE = 1e4
SGU_GROUPS = 4
SGU_CH = 128
SGU_CHUNK = 128
HGRN_HEADS = 4
HGRN_DK = 128
HGRN_CHUNK = 64
CONV_CH = 512
XATTN_HEADS = 4
XATTN_DIM = D_MODEL // XATTN_HEADS
N_EXPERTS = 32
TOP_K = 4
D_EXPERT = D_MODEL
SWIGLU_LIMIT = 7.0
SWIGLU_ALPHA = 1.702
ROPE_THETA = 10000.0
LN_EPS = 1e-5
RMS_EPS = 1e-6
NEG_INF = -1e30
DEEPNORM_ALPHA = (2 * DEPTH) ** 0.25

NSA_Q = NSA_HEADS * HEAD_DIM
SGU_W = SGU_GROUPS * SGU_CH
HGRN_W = HGRN_HEADS * HGRN_DK

VMEM_LIMIT_BYTES = 56 * 1024 * 1024
LANES = 128

PROJ_TM = 512
NSA_TQ = 128
NSA_KT = 256
MOE_TM = 512
MOE_BM = 256
SEG_ALIGN = 8
MOE_RT = TOP_K * MOE_TM + SEG_ALIGN * N_EXPERTS
MOE_CHUNK = 256


def _params(*sem):
    return pltpu.CompilerParams(dimension_semantics=sem, vmem_limit_bytes=VMEM_LIMIT_BYTES)


def _dot(a, b):
    return jnp.dot(a, b, preferred_element_type=F32)


def _dot_nt(a, b):
    return lax.dot_general(a, b, (((1,), (1,)), ((), ())), preferred_element_type=F32)


def _dot_tn(a, b):
    return lax.dot_general(a, b, (((0,), (0,)), ((), ())), preferred_element_type=F32)


def _gelu(x):
    return 0.5 * x * (1.0 + jnp.tanh(np.sqrt(2.0 / np.pi).astype(np.float32) * (x + 0.044715 * (x * x * x))))


def _sigmoid(x):
    return 1.0 / (1.0 + jnp.exp(-x))


def _layer_norm(y, g, b):
    mu = jnp.mean(y, axis=-1, keepdims=True)
    d = y - mu
    var = jnp.mean(d * d, axis=-1, keepdims=True)
    return d * lax.rsqrt(var + LN_EPS) * g + b


def _rope_kernel(pos_ref, inv_ref, cos_ref, sin_ref):
    ang = pos_ref[...].astype(F32) * inv_ref[...]
    cos_ref[...] = jnp.cos(ang)
    sin_ref[...] = jnp.sin(ang)


def rope_tables_t(positions):
    t = positions.size
    inv = 1.0 / (ROPE_THETA ** (jnp.arange(0, HEAD_DIM, 2, dtype=F32) / HEAD_DIM))
    tn = min(t, 4096)
    half = HEAD_DIM // 2
    return pl.pallas_call(
        _rope_kernel,
        grid=(t // tn,),
        in_specs=[pl.BlockSpec((1, tn), lambda i: (0, i)),
                  pl.BlockSpec((half, 1), lambda i: (0, 0))],
        out_specs=[pl.BlockSpec((half, tn), lambda i: (0, i))] * 2,
        out_shape=[jax.ShapeDtypeStruct((half, t), F32)] * 2,
        compiler_params=_params("parallel"),
        name="rope_tables",
    )(positions.reshape(1, t), inv.reshape(half, 1))


def _mm_kernel(x_ref, w_ref, o_ref):
    o_ref[...] = _dot(x_ref[...].astype(BF16), w_ref[...]).astype(o_ref.dtype)


def matmul(x, w, out_dtype, tm):
    m, k = x.shape
    n = w.shape[1]
    return pl.pallas_call(
        _mm_kernel,
        grid=(m // tm,),
        in_specs=[pl.BlockSpec((tm, k), lambda i: (i, 0)),
                  pl.BlockSpec((k, n), lambda i: (0, 0))],
        out_specs=pl.BlockSpec((tm, n), lambda i: (i, 0)),
        out_shape=jax.ShapeDtypeStruct((m, n), out_dtype),
        compiler_params=_params("parallel"),
        name="matmul",
    )(x, w)


def _outproj_ln_kernel(a_ref, b_ref, x_ref, wa_ref, wb_ref, g_ref, beta_ref, o_ref):
    mix = _dot(a_ref[...].astype(BF16), wa_ref[...]) + _dot(b_ref[...].astype(BF16), wb_ref[...])
    o_ref[...] = _layer_norm(DEEPNORM_ALPHA * x_ref[...] + mix, g_ref[...], beta_ref[...])


def outproj_ln(a, b, x, w_out, g, beta):
    t, d = x.shape
    na, nb = a.shape[1], b.shape[1]
    tm = PROJ_TM
    wa = w_out[:na].astype(BF16)
    wb = w_out[na:].astype(BF16)
    return pl.pallas_call(
        _outproj_ln_kernel,
        grid=(t // tm,),
        in_specs=[pl.BlockSpec((tm, na), lambda i: (i, 0)),
                  pl.BlockSpec((tm, nb), lambda i: (i, 0)),
                  pl.BlockSpec((tm, d), lambda i: (i, 0)),
                  pl.BlockSpec((na, d), lambda i: (0, 0)),
                  pl.BlockSpec((nb, d), lambda i: (0, 0)),
                  pl.BlockSpec((1, d), lambda i: (0, 0)),
                  pl.BlockSpec((1, d), lambda i: (0, 0))],
        out_specs=pl.BlockSpec((tm, d), lambda i: (i, 0)),
        out_shape=jax.ShapeDtypeStruct((t, d), F32),
        compiler_params=_params("parallel"),
        name="outproj_ln",
    )(a, b, x, wa, wb, g.reshape(1, d), beta.reshape(1, d))


def _xattn_kernel(x_ref, wq_ref, kv_ref, wo_ref, g_ref, beta_ref, o_ref):
    x = x_ref[...]
    q = _dot(x.astype(BF16), wq_ref[...])
    hw = XATTN_HEADS * XATTN_DIM
    heads = []
    for h in range(XATTN_HEADS):
        lo = h * XATTN_DIM
        qh = q[:, lo:lo + XATTN_DIM].astype(BF16)
        kh = kv_ref[0, :, lo:lo + XATTN_DIM]
        vh = kv_ref[0, :, hw + lo:hw + lo + XATTN_DIM]
        s = _dot_nt(qh, kh) * (XATTN_DIM ** -0.5)
        m = jnp.max(s, axis=-1, keepdims=True)
        e = jnp.exp(s - m)
        p = e / jnp.sum(e, axis=-1, keepdims=True)
        heads.append(_dot(p.astype(BF16), vh))
    o = jnp.concatenate(heads, axis=-1)
    xa = _dot(o.astype(BF16), wo_ref[...])
    o_ref[...] = _layer_norm(DEEPNORM_ALPHA * x + xa, g_ref[...], beta_ref[...])


def xattn_ln(x, kv, w_q, w_o, g, beta, seq):
    t, d = x.shape
    tm = PROJ_TM
    per_b = seq // tm
    mlen = kv.shape[1]
    return pl.pallas_call(
        _xattn_kernel,
        grid=(t // tm,),
        in_specs=[pl.BlockSpec((tm, d), lambda i: (i, 0)),
                  pl.BlockSpec((d, d), lambda i: (0, 0)),
                  pl.BlockSpec((1, mlen, 2 * d), lambda i: (i // per_b, 0, 0)),
                  pl.BlockSpec((d, d), lambda i: (0, 0)),
                  pl.BlockSpec((1, d), lambda i: (0, 0)),
                  pl.BlockSpec((1, d), lambda i: (0, 0))],
        out_specs=pl.BlockSpec((tm, d), lambda i: (i, 0)),
        out_shape=jax.ShapeDtypeStruct((t, d), F32),
        compiler_params=_params("parallel"),
        name="xattn_ln",
    )(x, w_q.astype(BF16), kv, w_o.astype(BF16), g.reshape(1, d), beta.reshape(1, d))


def _router_kernel(x_ref, wt_ref, b_ref, tri_ref, ltri_ref, row_ref, gate_ref, off_ref, cnt_ref, base_ref,
                   carry_ref):
    i = pl.program_id(0)

    @pl.when(i == 0)
    def _():
        carry_ref[...] = jnp.zeros_like(carry_ref)

    tm = x_ref.shape[0]
    logits = _dot_nt(wt_ref[...], x_ref[...].astype(BF16)) + b_ref[...]
    e_iota = lax.broadcasted_iota(I32, (N_EXPERTS, tm), 0)
    work = logits
    vals, hots = [], []
    for _ in range(TOP_K):
        m = jnp.max(work, axis=0, keepdims=True)
        idx = jnp.min(jnp.where(work == m, e_iota, N_EXPERTS), axis=0, keepdims=True)
        hot = e_iota == idx
        vals.append(m)
        hots.append(hot)
        work = jnp.where(hot, -jnp.inf, work)
    exps = [jnp.exp(v - vals[0]) for v in vals]
    den = exps[0] + exps[1] + exps[2] + exps[3]
    gate_ref[...] = jnp.concatenate([e / den for e in exps], axis=0)

    hot_all = jnp.zeros((N_EXPERTS, tm), F32)
    for hot in hots:
        hot_all = hot_all + jnp.where(hot, 1.0, 0.0)
    rank = _dot(hot_all.astype(BF16), tri_ref[...])
    n = jnp.sum(hot_all, axis=1, keepdims=True)
    seg = jnp.floor((n + (SEG_ALIGN - 1)) * (1.0 / SEG_ALIGN))
    seg = jnp.broadcast_to(seg, (N_EXPERTS, LANES))
    off = _dot(ltri_ref[...], seg.astype(BF16))
    where_row = off[:, 0:1] * SEG_ALIGN + rank
    row_ref[...] = jnp.concatenate(
        [jnp.sum(jnp.where(hot, where_row, 0.0), axis=0, keepdims=True) for hot in hots], axis=0).astype(I32)
    off_ref[0] = off * SEG_ALIGN
    cnt_ref[0] = seg * SEG_ALIGN
    base_ref[0] = carry_ref[...]
    carry_ref[...] = carry_ref[...] + seg * SEG_ALIGN


def moe_route(x, w_router, b_router):
    t, d = x.shape
    tm = MOE_TM
    nt = t // tm
    tri = jnp.asarray(np.triu(np.ones((tm, tm), np.float32), 1), BF16)
    ltri = jnp.asarray(np.tril(np.ones((N_EXPERTS, N_EXPERTS), np.float32), -1), BF16)
    tab = pl.BlockSpec((1, N_EXPERTS, LANES), lambda i: (i, 0, 0))
    tab_shape = jax.ShapeDtypeStruct((nt, N_EXPERTS, LANES), F32)
    return pl.pallas_call(
        _router_kernel,
        grid=(nt,),
        in_specs=[pl.BlockSpec((tm, d), lambda i: (i, 0)),
                  pl.BlockSpec((N_EXPERTS, d), lambda i: (0, 0)),
                  pl.BlockSpec((N_EXPERTS, 1), lambda i: (0, 0)),
                  pl.BlockSpec((tm, tm), lambda i: (0, 0)),
                  pl.BlockSpec((N_EXPERTS, N_EXPERTS), lambda i: (0, 0))],
        out_specs=[pl.BlockSpec((TOP_K, tm), lambda i: (0, i)),
                   pl.BlockSpec((TOP_K, tm), lambda i: (0, i)),
                   tab, tab, tab],
        out_shape=[jax.ShapeDtypeStruct((TOP_K, t), I32),
                   jax.ShapeDtypeStruct((TOP_K, t), F32),
                   tab_shape, tab_shape, tab_shape],
        scratch_shapes=[pltpu.VMEM((N_EXPERTS, LANES), F32)],
        compiler_params=_params("arbitrary"),
        name="moe_router",
    )(x, w_router.T.astype(BF16), b_router.reshape(N_EXPERTS, 1), tri, ltri)


def _segment_copies(off_ref, cnt_ref, dst_ref, make_copy, wait):
    bits = []
    b = MOE_TM
    while b >= SEG_ALIGN:
        bits.append(b)
        b //= 2

    def per_expert(e, c):
        n = cnt_ref[0, 0, e]
        off = off_ref[0, 0, e]
        dst = dst_ref[0, 0, e]
        for b in bits:
            @pl.when((n & b) != 0)
            def _():
                done = n & (-2 * b)
                cp = make_copy(pl.multiple_of(off + done, SEG_ALIGN), pl.multiple_of(dst + done, SEG_ALIGN), b)
                if wait:
                    cp.wait()
                else:
                    cp.start()
        return c

    lax.fori_loop(0, N_EXPERTS, per_expert, 0)


def _dispatch_kernel(off_ref, cnt_ref, dst_ref, zero_ref, tail_cnt_ref, tail_dst_ref, rest_ref, x_ref, row_ref,
                     xs_hbm, buf, sem):
    tm = x_ref.shape[0]
    xb = x_ref[...].astype(BF16)
    rows = [row_ref[k:k + 1, :] for k in range(TOP_K)]
    for c in range(MOE_RT // MOE_CHUNK):
        rr = c * MOE_CHUNK + lax.broadcasted_iota(I32, (MOE_CHUNK, tm), 0)
        perm = jnp.where(rr == rows[0], 1.0, 0.0)
        for k in range(1, TOP_K):
            perm = perm + jnp.where(rr == rows[k], 1.0, 0.0)
        buf[c * MOE_CHUNK:(c + 1) * MOE_CHUNK, :] = _dot(perm.astype(BF16), xb)

    def copy(src_row, dst_row, n):
        return pltpu.make_async_copy(buf.at[pl.ds(src_row, n)], xs_hbm.at[pl.ds(dst_row, n)], sem.at[0])

    _segment_copies(off_ref, cnt_ref, dst_ref, copy, wait=False)
    _segment_copies(off_ref, cnt_ref, dst_ref, copy, wait=True)

    @pl.when(pl.program_id(0) == pl.num_programs(0) - 1)
    def _():
        buf[0:MOE_BM, :] = jnp.zeros((MOE_BM, buf.shape[1]), F32)
        for wait in (False, True):
            _segment_copies(zero_ref, tail_cnt_ref, tail_dst_ref, copy, wait=wait)

            def rest(j, c):
                cp = copy(0, pl.multiple_of(rest_ref[0] + j * MOE_BM, MOE_BM), MOE_BM)
                cp.wait() if wait else cp.start()
                return c
            lax.fori_loop(0, rest_ref[1], rest, 0)


def _seg_spec():
    return pl.BlockSpec((1, 1, N_EXPERTS), lambda i: (i, 0, 0), memory_space=pltpu.SMEM)


def _smem_whole(shape):
    return pl.BlockSpec(shape, lambda i: (0,) * len(shape), memory_space=pltpu.SMEM)


def moe_dispatch(x, row, seg_off, seg_cnt, seg_dst, tail_cnt, tail_dst, rest, n_rows):
    t, d = x.shape
    tm = MOE_TM
    tab = (1, 1, N_EXPERTS)
    return pl.pallas_call(
        _dispatch_kernel,
        grid=(t // tm,),
        in_specs=[_seg_spec(), _seg_spec(), _seg_spec(),
                  _smem_whole(tab), _smem_whole(tab), _smem_whole(tab), _smem_whole((2,)),
                  pl.BlockSpec((tm, d), lambda i: (i, 0)),
                  pl.BlockSpec((TOP_K, tm), lambda i: (0, i))],
        out_specs=pl.BlockSpec(memory_space=pl.ANY),
        out_shape=jax.ShapeDtypeStruct((n_rows, d), F32),
        scratch_shapes=[pltpu.VMEM((MOE_RT, d), F32), pltpu.SemaphoreType.DMA((1,))],
        compiler_params=_params("arbitrary"),
        name="moe_dispatch",
    )(seg_off, seg_cnt, seg_dst, jnp.zeros(tab, I32), tail_cnt.reshape(tab), tail_dst.reshape(tab), rest, x, row)


def _expert_kernel(blk_e_ref, nused_ref, xs_ref, wgu_ref, bgu_ref, wdn_ref, bdn_ref, ys_ref):
    used = pl.program_id(0) < nused_ref[0]

    @pl.when(used)
    def _():
        h = _dot(xs_ref[...].astype(BF16), wgu_ref[0]) + bgu_ref[0]
        h_gate = jnp.minimum(h[:, :D_EXPERT], SWIGLU_LIMIT)
        h_up = jnp.clip(h[:, D_EXPERT:], -SWIGLU_LIMIT, SWIGLU_LIMIT)
        act = (h_up + 1.0) * (h_gate * _sigmoid(h_gate * SWIGLU_ALPHA))
        ys_ref[...] = _dot(act.astype(BF16), wdn_ref[0]) + bdn_ref[0]

    @pl.when(jnp.logical_not(used))
    def _():
        ys_ref[...] = jnp.zeros_like(ys_ref)


def moe_experts(xs, blk_e, nused, w_gu, b_gu, w_dn, b_dn):
    n_rows, d = xs.shape
    bm = MOE_BM
    nb = n_rows // bm
    grid_spec = pltpu.PrefetchScalarGridSpec(
        num_scalar_prefetch=2,
        grid=(nb,),
        in_specs=[pl.BlockSpec((bm, d), lambda i, be, nu: (i, 0)),
                  pl.BlockSpec((1, d, 2 * D_EXPERT), lambda i, be, nu: (be[i], 0, 0)),
                  pl.BlockSpec((1, 1, 2 * D_EXPERT), lambda i, be, nu: (be[i], 0, 0)),
                  pl.BlockSpec((1, D_EXPERT, d), lambda i, be, nu: (be[i], 0, 0)),
                  pl.BlockSpec((1, 1, d), lambda i, be, nu: (be[i], 0, 0))],
        out_specs=pl.BlockSpec((bm, d), lambda i, be, nu: (i, 0)),
    )
    return pl.pallas_call(
        _expert_kernel,
        grid_spec=grid_spec,
        out_shape=jax.ShapeDtypeStruct((n_rows, d), F32),
        compiler_params=_params("arbitrary"),
        name="moe_experts",
    )(blk_e, nused, xs, w_gu, b_gu.reshape(N_EXPERTS, 1, -1), w_dn, b_dn.reshape(N_EXPERTS, 1, -1))


def _combine_ln_kernel(off_ref, cnt_ref, dst_ref, ys_hbm, row_ref, gate_ref, x_ref, g_ref, beta_ref, o_ref,
                       buf, sem):
    tm = x_ref.shape[0]

    @pl.when(pl.program_id(0) == 0)
    def _():
        buf[...] = jnp.zeros_like(buf)

    def copy(buf_row, ys_row, n):
        return pltpu.make_async_copy(ys_hbm.at[pl.ds(ys_row, n)], buf.at[pl.ds(buf_row, n)], sem.at[0])

    _segment_copies(off_ref, cnt_ref, dst_ref, copy, wait=False)
    rows = [jnp.broadcast_to(row_ref[:, k:k + 1], (tm, MOE_CHUNK)) for k in range(TOP_K)]
    gates = [jnp.broadcast_to(gate_ref[:, k:k + 1], (tm, MOE_CHUNK)) for k in range(TOP_K)]
    lane = lax.broadcasted_iota(I32, (tm, MOE_CHUNK), 1)
    _segment_copies(off_ref, cnt_ref, dst_ref, copy, wait=True)
    ff = jnp.zeros((tm, x_ref.shape[1]), F32)
    for c in range(MOE_RT // MOE_CHUNK):
        rr = lane + c * MOE_CHUNK
        mix = jnp.where(rr == rows[0], gates[0], 0.0)
        for k in range(1, TOP_K):
            mix = mix + jnp.where(rr == rows[k], gates[k], 0.0)
        ff = ff + _dot(mix.astype(BF16), buf[c * MOE_CHUNK:(c + 1) * MOE_CHUNK, :].astype(BF16))
    o_ref[...] = _layer_norm(DEEPNORM_ALPHA * x_ref[...] + ff, g_ref[...], beta_ref[...])


def moe_combine_ln(ys, row_t, gate_t, seg_off, seg_cnt, seg_dst, x, g, beta):
    t, d = x.shape
    tm = MOE_TM
    return pl.pallas_call(
        _combine_ln_kernel,
        grid=(t // tm,),
        in_specs=[_seg_spec(), _seg_spec(), _seg_spec(),
                  pl.BlockSpec(memory_space=pl.ANY),
                  pl.BlockSpec((tm, TOP_K), lambda i: (i, 0)),
                  pl.BlockSpec((tm, TOP_K), lambda i: (i, 0)),
                  pl.BlockSpec((tm, d), lambda i: (i, 0)),
                  pl.BlockSpec((1, d), lambda i: (0, 0)),
                  pl.BlockSpec((1, d), lambda i: (0, 0))],
        out_specs=pl.BlockSpec((tm, d), lambda i: (i, 0)),
        out_shape=jax.ShapeDtypeStruct((t, d), F32),
        scratch_shapes=[pltpu.VMEM((MOE_RT, d), F32), pltpu.SemaphoreType.DMA((1,))],
        compiler_params=_params("arbitrary"),
        name="moe_combine_ln",
    )(seg_off, seg_cnt, seg_dst, ys, row_t, gate_t, x, g.reshape(1, d), beta.reshape(1, d))


def moe_ln(x, w_router, b_router, w_gu, b_gu, w_dn, b_dn, g, beta):
    t, d = x.shape
    bm = MOE_BM
    nt = t // MOE_TM
    row, gate, off, cnt, base = moe_route(x, w_router, b_router)
    seg_off = off[:, :, 0].astype(I32)
    seg_cnt = cnt[:, :, 0].astype(I32)
    seg_base = base[:, :, 0].astype(I32)
    total = seg_base[-1] + seg_cnt[-1]
    padded = (total + bm - 1) // bm * bm
    pend = jnp.cumsum(padded)
    pstart = pend - padded
    n_rows = (t * TOP_K + SEG_ALIGN * N_EXPERTS * nt) // bm * bm + N_EXPERTS * bm
    nb = n_rows // bm
    blk_row = jnp.arange(nb, dtype=I32) * bm
    blk_e = jnp.minimum(jnp.sum((pend[None, :] <= blk_row[:, None]).astype(I32), axis=1), N_EXPERTS - 1)
    nused = (pend[-1] // bm).astype(I32).reshape(1)
    seg_dst = (pstart[None, :] + seg_base).reshape(nt, 1, N_EXPERTS)
    seg_off = seg_off.reshape(nt, 1, N_EXPERTS)
    seg_cnt = seg_cnt.reshape(nt, 1, N_EXPERTS)
    rest = jnp.stack([pend[-1], nb - nused[0]]).astype(I32)
    xs = moe_dispatch(x, row, seg_off, seg_cnt, seg_dst, padded - total, pstart + total, rest, n_rows)
    ys = moe_experts(xs, blk_e, nused, w_gu.astype(BF16), b_gu, w_dn.astype(BF16), b_dn)
    return moe_combine_ln(ys, row.T, gate.T, seg_off, seg_cnt, seg_dst, x, g, beta)


def _sgu_kernel(u_ref, v_ref, lng_ref, lnb_ref, w_ref, bs_ref, o_ref):
    tm = u_ref.shape[0]
    row = lax.broadcasted_iota(I32, (SGU_CHUNK, SGU_CHUNK), 0)
    col = lax.broadcasted_iota(I32, (SGU_CHUNK, SGU_CHUNK), 1)
    causal = row >= col
    for g in range(SGU_GROUPS):
        lo = g * SGU_CH
        vg = _gelu(v_ref[:, lo:lo + SGU_CH])
        vg = _layer_norm(vg, lng_ref[g:g + 1, :], lnb_ref[g:g + 1, :]).astype(BF16)
        wg = jnp.where(causal, w_ref[g], 0.0).astype(BF16)
        bias = bs_ref[:, g:g + 1]
        for n in range(tm // SGU_CHUNK):
            r0 = n * SGU_CHUNK
            mix = _dot(wg, vg[r0:r0 + SGU_CHUNK]) + bias
            o_ref[r0:r0 + SGU_CHUNK, lo:lo + SGU_CH] = _gelu(u_ref[r0:r0 + SGU_CHUNK, lo:lo + SGU_CH]) * mix


def sgu_mixer(u, v, ln_g, ln_b, w_s, b_s):
    t, w = u.shape
    tm = PROJ_TM
    return pl.pallas_call(
        _sgu_kernel,
        grid=(t // tm,),
        in_specs=[pl.BlockSpec((tm, w), lambda i: (i, 0)),
                  pl.BlockSpec((tm, w), lambda i: (i, 0)),
                  pl.BlockSpec((SGU_GROUPS, SGU_CH), lambda i: (0, 0)),
                  pl.BlockSpec((SGU_GROUPS, SGU_CH), lambda i: (0, 0)),
                  pl.BlockSpec((SGU_GROUPS, SGU_CHUNK, SGU_CHUNK), lambda i: (0, 0, 0)),
                  pl.BlockSpec((SGU_CHUNK, SGU_GROUPS), lambda i: (0, 0))],
        out_specs=pl.BlockSpec((tm, w), lambda i: (i, 0)),
        out_shape=jax.ShapeDtypeStruct((t, w), F32),
        compiler_params=_params("parallel"),
        name="sgu_mixer",
    )(u, v, ln_g, ln_b, w_s, b_s.T)


def _conv_kernel(h_ref, bg_ref, cg_ref, w_ref, b_ref, o_ref, carry_ref):
    @pl.when(pl.program_id(1) == 0)
    def _():
        carry_ref[...] = jnp.zeros_like(carry_ref)

    z = cg_ref[...] * h_ref[...]
    tm = z.shape[0]
    row = lax.broadcasted_iota(I32, z.shape, 0)
    prev = carry_ref[...]
    z1 = jnp.where(row == 0, prev[7:8, :], pltpu.roll(z, 1, 0))
    z2 = jnp.where(row == 0, prev[6:7, :], jnp.where(row == 1, prev[7:8, :], pltpu.roll(z, 2, 0)))
    y = w_ref[0:1, :] * z2 + w_ref[1:2, :] * z1 + w_ref[2:3, :] * z + b_ref[...]
    o_ref[...] = bg_ref[...] * y
    carry_ref[...] = z[tm - 8:tm, :]


def conv_mixer(proj, col0, conv_w, conv_b, batch, seq):
    tm = PROJ_TM
    per_b = seq // tm
    c = CONV_CH

    def spec(j):
        return pl.BlockSpec((tm, c), lambda b, i: (b * per_b + i, j))

    return pl.pallas_call(
        _conv_kernel,
        grid=(batch, per_b),
        in_specs=[spec(col0), spec(col0 + 1), spec(col0 + 2),
                  pl.BlockSpec((3, c), lambda b, i: (0, 0)),
                  pl.BlockSpec((1, c), lambda b, i: (0, 0))],
        out_specs=pl.BlockSpec((tm, c), lambda b, i: (b * per_b + i, 0)),
        out_shape=jax.ShapeDtypeStruct((batch * seq, c), F32),
        scratch_shapes=[pltpu.VMEM((8, c), F32)],
        compiler_params=_params("arbitrary", "arbitrary"),
        name="conv_mixer",
    )(proj, proj, proj, conv_w, conv_b.reshape(1, c))


def _split3(x):
    hi = x.astype(BF16)
    r1 = x - hi.astype(F32)
    mid = r1.astype(BF16)
    lo = (r1 - mid.astype(F32)).astype(BF16)
    return hi, mid, lo


def _hgrn_kernel(q_ref, f_ref, i_ref, g_ref, lbl_ref, ng_ref, o_ref, state_ref, *, layer):
    c = HGRN_CHUNK
    n_chunks = q_ref.shape[0] // c
    lw = lbl_ref[...]
    lw = jnp.exp(lw - jnp.max(lw, axis=0, keepdims=True))
    lw = lw / jnp.sum(lw, axis=0, keepdims=True)
    lb = jnp.sum(lw[1:layer + 1], axis=0, keepdims=True)
    row = lax.broadcasted_iota(I32, (c, c), 0)
    col = lax.broadcasted_iota(I32, (c, c), 1)
    tril = jnp.where(row >= col, 1.0, 0.0).astype(BF16)
    sub = lax.broadcasted_iota(I32, (8, HGRN_DK), 0)
    state_ref[...] = jnp.zeros_like(state_ref)

    def chunk(ci, carry):
        r0 = pl.multiple_of(ci * c, c)
        q = q_ref[pl.ds(r0, c), :]
        z = f_ref[pl.ds(r0, c), :]
        v = i_ref[pl.ds(r0, c), :]
        k = (1.0 - lb) * _sigmoid(-z)
        log_f = jnp.log1p(-k)
        hi, mid, lo = _split3(log_f)
        b = _dot(tril, hi) + _dot(tril, mid) + _dot(tril, lo)
        state_t = state_ref[...]
        o = _dot_nt((q * jnp.exp(b)).astype(BF16), state_t.astype(BF16))

        acc = [o[8 * j:8 * j + 8] for j in range(c // 8)]
        for s in range(c):
            bs = b[s:s + 1, :]
            ks = k[s:s + 1, :]
            vs = v[s:s + 1, :]
            j0 = s // 8
            for j in range(j0, c // 8):
                dlt = b[8 * j:8 * j + 8] - bs
                if j == j0:
                    dlt = jnp.where(sub >= s - 8 * j0, dlt, NEG_INF)
                a = jnp.sum(q[8 * j:8 * j + 8] * ks * jnp.exp(dlt), axis=-1, keepdims=True)
                acc[j] = acc[j] + a * vs
        o = jnp.concatenate(acc, axis=0)

        b_last = b[c - 1:c, :]
        kd = (k * jnp.exp(b_last - b)).astype(BF16)
        state_ref[...] = jnp.exp(b_last) * state_t + _dot_tn(v.astype(BF16), kd)
        o = o * lax.rsqrt(jnp.mean(o * o, axis=-1, keepdims=True) + RMS_EPS) * ng_ref[...]
        gg = g_ref[pl.ds(r0, c), :]
        o_ref[pl.ds(r0, c), :] = o * (gg * _sigmoid(gg))
        return carry

    lax.fori_loop(0, n_chunks, chunk, 0)


def hgrn2_mixer(proj, lb_logits, norm_g, layer, batch, seq):
    dk = HGRN_DK
    h = HGRN_HEADS

    def spec(grp):
        return pl.BlockSpec((seq, dk), lambda b, hh: (b, grp * h + hh))

    return pl.pallas_call(
        functools.partial(_hgrn_kernel, layer=layer),
        grid=(batch, h),
        in_specs=[spec(0), spec(1), spec(2), spec(3),
                  pl.BlockSpec((DEPTH, dk), lambda b, hh: (0, hh)),
                  pl.BlockSpec((1, dk), lambda b, hh: (0, hh))],
        out_specs=pl.BlockSpec((seq, dk), lambda b, hh: (b, hh)),
        out_shape=jax.ShapeDtypeStruct((batch * seq, h * dk), F32),
        scratch_shapes=[pltpu.VMEM((dk, dk), F32)],
        compiler_params=_params("parallel", "parallel"),
        name="hgrn2_mixer",
    )(proj, proj, proj, proj, lb_logits, norm_g.reshape(1, h * dk))


EVEN_T_ROWS = NSA_Q + 4 * 2 * HEAD_DIM + 32
EVEN_S_COLS = 4 * LANES + 2 * SGU_W


def _even_proj_kernel(x_ref, wt_ref, ws_ref, cos_ref, sin_ref, qt_ref, qrt_ref, ks_ref, kw_ref,
                      vs_ref, vw_ref, gt_ref, kvc_ref, u_ref, v_ref, *, per_b):
    tm = x_ref.shape[0]
    xb = x_ref[...].astype(BF16)
    st = _dot_nt(wt_ref[...], xb)
    cos = cos_ref[...]
    sin = sin_ref[...]
    half = HEAD_DIM // 2
    scale = HEAD_DIM ** -0.5

    def rope(blk):
        x1, x2 = blk[:half], blk[half:]
        return jnp.concatenate([x1 * cos - x2 * sin, x2 * cos + x1 * sin], axis=0)

    for hh in range(NSA_HEADS):
        blk = st[hh * HEAD_DIM:(hh + 1) * HEAD_DIM]
        qt_ref[hh * HEAD_DIM:(hh + 1) * HEAD_DIM, :] = (blk * scale).astype(BF16)
        qrt_ref[hh * HEAD_DIM:(hh + 1) * HEAD_DIM, :] = (rope(blk) * scale).astype(BF16)

    kk = jnp.concatenate([rope(st[NSA_Q + j * HEAD_DIM:NSA_Q + (j + 1) * HEAD_DIM]) for j in range(4)], axis=0)
    kk = kk.T
    pos = lax.rem(pl.program_id(0), per_b) * tm + lax.broadcasted_iota(I32, (tm, LANES), 0)
    lane = lax.broadcasted_iota(I32, (tm, LANES), 1)
    member = jnp.where(lane == pos // SLC_BLOCK, 1.0, 0.0)
    ks_ref[...] = jnp.concatenate([kk[:, :LANES], member], axis=1).astype(BF16)
    kw_ref[...] = kk[:, LANES:].astype(BF16)
    v0 = NSA_Q + 4 * HEAD_DIM
    for j in range(tm // NSA_KT):
        vs_ref[j] = st[v0:v0 + LANES, j * NSA_KT:(j + 1) * NSA_KT].astype(BF16)
    for j in range(tm // LANES):
        vw_ref[j] = st[v0 + LANES:v0 + 2 * LANES, j * LANES:(j + 1) * LANES].astype(BF16)
    gt_ref[...] = _sigmoid(st[v0 + 2 * LANES:v0 + 2 * LANES + 32])

    ss = _dot(xb, ws_ref[...])
    for j in range(4):
        kvc_ref[j] = ss[:, j * LANES:j * LANES + HEAD_DIM]
    u_ref[...] = ss[:, 4 * LANES:4 * LANES + SGU_W]
    v_ref[...] = ss[:, 4 * LANES + SGU_W:]


def even_proj(x, w_in, cos_t, sin_t, seq):
    t, d = x.shape
    tm = PROJ_TM
    per_b = seq // tm
    hd = HEAD_DIM
    kv0 = NSA_Q

    def kvcols(i):
        return w_in[:, kv0 + i * 2 * hd:kv0 + (i + 1) * 2 * hd]

    g0 = kv0 + 6 * 2 * hd
    n_gates = 3 * NSA_HEADS
    wt = jnp.concatenate([w_in[:, :NSA_Q], kvcols(2), kvcols(4), kvcols(3), kvcols(5),
                          w_in[:, g0:g0 + n_gates], jnp.zeros((d, 32 - n_gates), F32)], axis=1).T.astype(BF16)
    zpad = jnp.zeros((d, LANES - hd), F32)
    cmp_cols = []
    for i in (0, 1):
        for g in range(NSA_KV_HEADS):
            cmp_cols += [w_in[:, kv0 + i * 2 * hd + g * hd:kv0 + i * 2 * hd + (g + 1) * hd], zpad]
    ws = jnp.concatenate(cmp_cols + [w_in[:, g0 + n_gates:]], axis=1).astype(BF16)
    half = hd // 2
    tok = lambda i: (i, 0)
    tok_t = lambda i: (0, i)
    return pl.pallas_call(
        functools.partial(_even_proj_kernel, per_b=per_b),
        grid=(t // tm,),
        in_specs=[pl.BlockSpec((tm, d), tok),
                  pl.BlockSpec((EVEN_T_ROWS, d), lambda i: (0, 0)),
                  pl.BlockSpec((d, EVEN_S_COLS), lambda i: (0, 0)),
                  pl.BlockSpec((half, tm), tok_t),
                  pl.BlockSpec((half, tm), tok_t)],
        out_specs=[pl.BlockSpec((NSA_Q, tm), tok_t),
                   pl.BlockSpec((NSA_Q, tm), tok_t),
                   pl.BlockSpec((tm, 2 * LANES), tok),
                   pl.BlockSpec((tm, LANES), tok),
                   pl.BlockSpec((tm // NSA_KT, LANES, NSA_KT), lambda i: (i, 0, 0)),
                   pl.BlockSpec((tm // LANES, LANES, LANES), lambda i: (i, 0, 0)),
                   pl.BlockSpec((32, tm), tok_t),
                   pl.BlockSpec((4, tm, hd), lambda i: (0, i, 0)),
                   pl.BlockSpec((tm, SGU_W), tok),
                   pl.BlockSpec((tm, SGU_W), tok)],
        out_shape=[jax.ShapeDtypeStruct((NSA_Q, t), BF16),
                   jax.ShapeDtypeStruct((NSA_Q, t), BF16),
                   jax.ShapeDtypeStruct((t, 2 * LANES), BF16),
                   jax.ShapeDtypeStruct((t, LANES), BF16),
                   jax.ShapeDtypeStruct((t // NSA_KT, LANES, NSA_KT), BF16),
                   jax.ShapeDtypeStruct((t // LANES, LANES, LANES), BF16),
                   jax.ShapeDtypeStruct((32, t), F32),
                   jax.ShapeDtypeStruct((4, t, hd), F32),
                   jax.ShapeDtypeStruct((t, SGU_W), F32),
                   jax.ShapeDtypeStruct((t, SGU_W), F32)],
        compiler_params=_params("parallel"),
        name="even_proj",
    )(x, wt, ws, cos_t, sin_t)


def _compress_kernel(kvc_ref, pos_ref, w1_ref, w2_ref, kc_ref, vct_ref):
    ncp = kc_ref.shape[1]
    assert CMP_BLOCK == 2 * CMP_STRIDE
    for i in range(2):
        outs = []
        for g in range(NSA_KV_HEADS):
            first = jnp.zeros((ncp, CMP_HIDDEN), F32)
            second = jnp.zeros((ncp, CMP_HIDDEN), F32)
            for l in range(CMP_STRIDE):
                rows = kvc_ref[i * NSA_KV_HEADS + g, pl.ds(l, ncp, stride=CMP_STRIDE), :]
                first = first + _dot((rows + pos_ref[i, l:l + 1, :]).astype(BF16), w1_ref[i, l])
                l2 = l + CMP_STRIDE
                second = second + _dot((rows + pos_ref[i, l2:l2 + 1, :]).astype(BF16), w1_ref[i, l2])
            hid = _gelu(first + pltpu.roll(second, ncp - 1, 0))
            outs.append(_dot(hid.astype(BF16), w2_ref[i]))
        if i == 0:
            kc_ref[0] = jnp.concatenate(outs, axis=1).astype(BF16)
        else:
            vct_ref[0] = jnp.concatenate(outs, axis=1).T.astype(BF16)


def nsa_compress(kvc, cmp_pos, cmp_w1, cmp_w2, batch, seq):
    hd = HEAD_DIM
    ncp = seq // CMP_STRIDE
    w1 = cmp_w1.reshape(2, CMP_BLOCK, hd, CMP_HIDDEN).astype(BF16)
    return pl.pallas_call(
        _compress_kernel,
        grid=(batch,),
        in_specs=[pl.BlockSpec((4, seq, hd), lambda b: (0, b, 0)),
                  pl.BlockSpec((2, CMP_BLOCK, hd), lambda b: (0, 0, 0)),
                  pl.BlockSpec((2, CMP_BLOCK, hd, CMP_HIDDEN), lambda b: (0, 0, 0, 0)),
                  pl.BlockSpec((2, CMP_HIDDEN, hd), lambda b: (0, 0, 0))],
        out_specs=[pl.BlockSpec((1, ncp, 2 * hd), lambda b: (b, 0, 0)),
                   pl.BlockSpec((1, 2 * hd, ncp), lambda b: (b, 0, 0))],
        out_shape=[jax.ShapeDtypeStruct((batch, ncp, 2 * hd), BF16),
                   jax.ShapeDtypeStruct((batch, 2 * hd, ncp), BF16)],
        compiler_params=_params("parallel"),
        name="nsa_compress",
    )(kvc, cmp_pos, w1, cmp_w2.astype(BF16))


def _softmax_cols(s, mask):
    sm = jnp.where(mask, s, NEG_INF)
    m = jnp.max(sm, axis=0, keepdims=True)
    e = jnp.exp(sm - m)
    return jnp.where(mask, e / jnp.sum(e, axis=0, keepdims=True), 0.0)


def _nsa_kernel(qt_ref, qrt_ref, gt_ref, kc_ref, vct_ref, mct_ref, ks_ref, kw_ref, vs_ref, vw_ref, o_ref,
                *, n_top):
    tq = qt_ref.shape[1]
    hg = NSA_GROUP
    hd = HEAD_DIM
    w = hg * tq
    ncp = kc_ref.shape[1]
    n_slc = mct_ref.shape[0]
    s0 = pl.program_id(1) * tq
    t_lane = s0 + lax.rem(lax.broadcasted_iota(I32, (1, w), 1), tq)
    t_q = s0 + lax.broadcasted_iota(I32, (1, tq), 1)
    zeros_g = jnp.zeros((hd, w), BF16)
    n_win = WINDOW // LANES + tq // LANES
    wt0 = jnp.maximum(s0 // LANES - WINDOW // LANES, 0)

    heads = []
    for g in range(NSA_KV_HEADS):
        def grp(ref):
            rows = jnp.concatenate([ref[(g * hg + h) * hd:(g * hg + h + 1) * hd, :] for h in range(hg)], axis=1)
            return jnp.concatenate([rows, zeros_g] if g == 0 else [zeros_g, rows], axis=0)

        q_raw = grp(qt_ref)
        q_rot = grp(qrt_ref)
        vrows = slice(g * hd, (g + 1) * hd)

        sc = _dot(kc_ref[0], q_raw)
        c_end = lax.broadcasted_iota(I32, (ncp, w), 0) * CMP_STRIDE + (CMP_BLOCK - 1)
        p = _softmax_cols(sc, c_end <= t_lane)
        o_cmp = _dot(vct_ref[0, vrows, :], p.astype(BF16))
        psum = p[:, 0:tq]
        for h in range(1, hg):
            psum = psum + p[:, h * tq:(h + 1) * tq]
        imp = _dot(mct_ref[...], psum.astype(BF16))

        j_blk = lax.broadcasted_iota(I32, (n_slc, tq), 0)
        cur = t_q // SLC_BLOCK
        forced = (j_blk == 0) | (j_blk == cur) | (j_blk == cur - 1)
        valid = j_blk <= cur
        score = jnp.where(forced, SLC_FORCED_SCORE, jnp.where(valid, imp, -1.0))
        nv = n_slc // 8
        sblk = [score[8 * v:8 * v + 8] for v in range(nv)]
        rank = [jnp.zeros((8, tq), F32) for _ in range(nv)]
        sub = lax.broadcasted_iota(I32, (8, tq), 0)
        for k in range(n_slc):
            sk = score[k:k + 1, :]
            kv_ = k // 8
            for v in range(nv):
                ge = jnp.where(sk >= sblk[v], 1.0, 0.0)
                gt = jnp.where(sk > sblk[v], 1.0, 0.0)
                if v > kv_:
                    beats = ge
                elif v < kv_:
                    beats = gt
                else:
                    beats = jnp.where(sub > k - 8 * kv_, ge, gt)
                rank[v] = rank[v] + beats
        rank = jnp.concatenate(rank, axis=0)
        bias = jnp.where((rank < n_top) & valid, 0.0, NEG_INF)
        bias = jnp.concatenate([bias] * hg, axis=1)
        if n_slc < 2 * hd:
            bias = jnp.concatenate([bias, jnp.zeros((2 * hd - n_slc, w), F32)], axis=0)
        q_aug = jnp.concatenate([q_rot, bias.astype(BF16)], axis=0)

        n_kv = (s0 + tq + NSA_KT - 1) // NSA_KT

        def slc_step(kt, carry):
            m, l, acc = carry
            k0 = pl.multiple_of(kt * NSA_KT, NSA_KT)
            s = _dot(ks_ref[pl.ds(k0, NSA_KT), :], q_aug)
            kpos = k0 + lax.broadcasted_iota(I32, (NSA_KT, w), 0)
            s = jnp.where(kpos <= t_lane, s, NEG_INF)
            m_new = jnp.maximum(m, jnp.max(s, axis=0, keepdims=True))
            alpha = jnp.exp(m - m_new)
            pp = jnp.exp(s - m_new)
            l = l * alpha + jnp.sum(pp, axis=0, keepdims=True)
            acc = acc * alpha + _dot(vs_ref[kt, vrows, :], pp.astype(BF16))
            return m_new, l, acc

        m0 = jnp.full((1, w), NEG_INF, F32)
        _, l_s, acc_s = lax.fori_loop(0, n_kv, slc_step, (m0, jnp.zeros((1, w), F32), jnp.zeros((hd, w), F32)))
        o_slc = acc_s / l_s

        kw0 = pl.multiple_of(wt0 * LANES, LANES)
        s = _dot(kw_ref[pl.ds(kw0, n_win * LANES), :], q_rot)
        dist = t_lane - (kw0 + lax.broadcasted_iota(I32, (n_win * LANES, w), 0))
        sm = jnp.where((dist >= 0) & (dist < WINDOW), s, NEG_INF)
        e = jnp.exp(sm - jnp.max(sm, axis=0, keepdims=True))
        den = jnp.sum(e, axis=0, keepdims=True)
        eb = e.astype(BF16)
        o_win = _dot(vw_ref[wt0, vrows, :], eb[0:LANES])
        for j in range(1, n_win):
            o_win = o_win + _dot(vw_ref[wt0 + j, vrows, :], eb[j * LANES:(j + 1) * LANES])
        o_win = o_win / den

        for h in range(hg):
            r = (g * hg + h) * 3
            cols = slice(h * tq, (h + 1) * tq)
            heads.append(gt_ref[r:r + 1, :] * o_cmp[:, cols] + gt_ref[r + 1:r + 2, :] * o_slc[:, cols]
                         + gt_ref[r + 2:r + 3, :] * o_win[:, cols])
    o_ref[...] = jnp.concatenate(heads, axis=0).T


def _cmp_to_slc_t(seq):
    ncp = seq // CMP_STRIDE
    ns = seq // SLC_BLOCK
    cs = np.arange(ncp)[None, :] * CMP_STRIDE
    ss = np.arange(ns)[:, None] * SLC_BLOCK
    ov = np.clip(np.minimum(cs + CMP_BLOCK, ss + SLC_BLOCK) - np.maximum(cs, ss), 0, None) / CMP_BLOCK
    ov[:, ncp - 1] = 0.0
    return jnp.asarray(ov, BF16)


def nsa_mixer(qt, qrt, gt, kc, vct, ks, kw, vs, vw, batch, seq):
    tq = NSA_TQ
    nq = seq // tq
    ncp = seq // CMP_STRIDE
    n_slc = seq // SLC_BLOCK
    assert n_slc <= 2 * HEAD_DIM and n_slc % 8 == 0 and seq >= WINDOW + tq
    col = lambda b, i: (0, b * nq + i)
    return pl.pallas_call(
        functools.partial(_nsa_kernel, n_top=min(SLC_TOP, n_slc)),
        grid=(batch, nq),
        in_specs=[pl.BlockSpec((NSA_Q, tq), col),
                  pl.BlockSpec((NSA_Q, tq), col),
                  pl.BlockSpec((32, tq), col),
                  pl.BlockSpec((1, ncp, 2 * HEAD_DIM), lambda b, i: (b, 0, 0)),
                  pl.BlockSpec((1, 2 * HEAD_DIM, ncp), lambda b, i: (b, 0, 0)),
                  pl.BlockSpec((n_slc, ncp), lambda b, i: (0, 0)),
                  pl.BlockSpec((seq, 2 * LANES), lambda b, i: (b, 0)),
                  pl.BlockSpec((seq, LANES), lambda b, i: (b, 0)),
                  pl.BlockSpec((seq // NSA_KT, LANES, NSA_KT), lambda b, i: (b, 0, 0)),
                  pl.BlockSpec((seq // LANES, LANES, LANES), lambda b, i: (b, 0, 0))],
        out_specs=pl.BlockSpec((tq, NSA_Q), lambda b, i: (b * nq + i, 0)),
        out_shape=jax.ShapeDtypeStruct((batch * seq, NSA_Q), F32),
        compiler_params=_params("parallel", "parallel"),
        name="nsa_mixer",
    )(qt, qrt, gt, kc, vct, _cmp_to_slc_t(seq), ks, kw, vs, vw)


def kernel(x, mem, positions, w_in_even, nsa_cmp_pos, nsa_cmp_w1, nsa_cmp_w2, sgu_ln_g, sgu_ln_b, sgu_w, sgu_b, w_out_even, w_in_odd, hgrn_lb_logits, hgrn_norm_g, conv_w, conv_b, w_out_odd, xattn_w_q, xattn_w_kv, xattn_w_o, ln_g, ln_b, router_w, router_b, expert_w_gu, expert_b_gu, expert_w_dn, expert_b_dn):
    batch, seq, d = x.shape
    t = batch * seq
    cos_t, sin_t = rope_tables_t(positions)
    mem2 = mem.reshape(-1, d)
    xf = x.reshape(t, d)
    for layer in range(DEPTH):
        j = layer // 2
        if layer % 2 == 0:
            qt, qrt, ks, kw, vs, vw, gt, kvc, u, v = even_proj(xf, w_in_even[j], cos_t, sin_t, seq)
            kc, vct = nsa_compress(kvc, nsa_cmp_pos[j], nsa_cmp_w1[j], nsa_cmp_w2[j], batch, seq)
            o_a = nsa_mixer(qt, qrt, gt, kc, vct, ks, kw, vs, vw, batch, seq)
            o_b = sgu_mixer(u, v, sgu_ln_g[j], sgu_ln_b[j], sgu_w[j], sgu_b[j])
            w_out = w_out_even[j]
        else:
            proj = matmul(xf, w_in_odd[j].astype(BF16), F32, PROJ_TM)
            o_a = hgrn2_mixer(proj, hgrn_lb_logits, hgrn_norm_g[j], layer, batch, seq)
            o_b = conv_mixer(proj, 4 * HGRN_W // CONV_CH, conv_w[j], conv_b[j], batch, seq)
            w_out = w_out_odd[j]
        xf = outproj_ln(o_a, o_b, xf, w_out, ln_g[layer, 0], ln_b[layer, 0])
        kv = matmul(mem2, xattn_w_kv[layer].astype(BF16), BF16, mem.shape[1]).reshape(batch, mem.shape[1], 2 * d)
        xf = xattn_ln(xf, kv, xattn_w_q[layer], xattn_w_o[layer], ln_g[layer, 1], ln_b[layer, 1], seq)
        xf = moe_ln(xf, router_w[layer], router_b[layer], expert_w_gu[layer], expert_b_gu[layer],
                    expert_w_dn[layer], expert_b_dn[layer], ln_g[layer, 2], ln_b[layer, 2])
    return xf.reshape(batch, seq, d)
```

```python
import functools

import numpy as np
import jax
import jax.numpy as jnp
from jax import lax
from jax.experimental import pallas as pl
from jax.experimental.pallas import tpu as pltpu

F32 = jnp.float32
BF16 = jnp.bfloat16
I32 = jnp.int32

D_MODEL = 1024
DEPTH = 2
HEAD_DIM = 64
NSA_HEADS = 8
NSA_KV_HEADS = 2
NSA_GROUP = NSA_HEADS // NSA_KV_HEADS
CMP_BLOCK = 32
CMP_STRIDE = 16
CMP_HIDDEN = 256
SLC_BLOCK = 64
SLC_TOP = 16
WINDOW = 512
SLC_FORCED_SCORE = 1e4
SGU_GROUPS = 4
SGU_CH = 128
SGU_CHUNK = 128
HGRN_HEADS = 4
HGRN_DK = 128
HGRN_CHUNK = 64
CONV_CH = 512
XATTN_HEADS = 4
XATTN_DIM = D_MODEL // XATTN_HEADS
N_EXPERTS = 32
TOP_K = 4
D_EXPERT = D_MODEL
SWIGLU_LIMIT = 7.0
SWIGLU_ALPHA = 1.702
ROPE_THETA = 10000.0
LN_EPS = 1e-5
RMS_EPS = 1e-6
NEG_INF = -1e30
DEEPNORM_ALPHA = (2 * DEPTH) ** 0.25
LOG2_E = 1.4426950408889634

NSA_Q = NSA_HEADS * HEAD_DIM
SGU_W = SGU_GROUPS * SGU_CH
HGRN_W = HGRN_HEADS * HGRN_DK

VMEM_LIMIT_BYTES = 56 * 1024 * 1024
LANES = 128

PROJ_TM = 512
NSA_TQ = 128
NSA_KT = 256
HGRN_STEP = 32
HGRN_SEG = 1024
MOE_TM = 512
MOE_BM = 256
SEG_ALIGN = 8
MOE_RT = TOP_K * MOE_TM + SEG_ALIGN * N_EXPERTS
MOE_CHUNK = 256


def _params(*sem):
    return pltpu.CompilerParams(dimension_semantics=sem, vmem_limit_bytes=VMEM_LIMIT_BYTES)


def _dot(a, b):
    return jnp.dot(a, b, preferred_element_type=F32)


def _dot_nt(a, b):
    return lax.dot_general(a, b, (((1,), (1,)), ((), ())), preferred_element_type=F32)


def _dot_tn(a, b):
    return lax.dot_general(a, b, (((0,), (0,)), ((), ())), preferred_element_type=F32)


def _gelu(x):
    return 0.5 * x * (1.0 + jnp.tanh(np.sqrt(2.0 / np.pi).astype(np.float32) * (x + 0.044715 * (x * x * x))))


def _sigmoid(x):
    return 1.0 / (1.0 + jnp.exp(-x))


def _layer_norm(y, g, b):
    mu = jnp.mean(y, axis=-1, keepdims=True)
    d = y - mu
    var = jnp.mean(d * d, axis=-1, keepdims=True)
    return d * lax.rsqrt(var + LN_EPS) * g + b


def _rope_kernel(pos_ref, inv_ref, cos_ref, sin_ref):
    ang = pos_ref[...].astype(F32) * inv_ref[...]
    cos_ref[...] = jnp.cos(ang)
    sin_ref[...] = jnp.sin(ang)


def rope_tables_t(positions):
    t = positions.size
    inv = 1.0 / (ROPE_THETA ** (jnp.arange(0, HEAD_DIM, 2, dtype=F32) / HEAD_DIM))
    tn = min(t, 4096)
    half = HEAD_DIM // 2
    return pl.pallas_call(
        _rope_kernel,
        grid=(t // tn,),
        in_specs=[pl.BlockSpec((1, tn), lambda i: (0, i)),
                  pl.BlockSpec((half, 1), lambda i: (0, 0))],
        out_specs=[pl.BlockSpec((half, tn), lambda i: (0, i))] * 2,
        out_shape=[jax.ShapeDtypeStruct((half, t), F32)] * 2,
        compiler_params=_params("parallel"),
        name="rope_tables",
    )(positions.reshape(1, t), inv.reshape(half, 1))


def _mm_kernel(x_ref, w_ref, o_ref):
    o_ref[...] = _dot(x_ref[...].astype(BF16), w_ref[...]).astype(o_ref.dtype)


def matmul(x, w, out_dtype, tm):
    m, k = x.shape
    n = w.shape[1]
    return pl.pallas_call(
        _mm_kernel,
        grid=(m // tm,),
        in_specs=[pl.BlockSpec((tm, k), lambda i: (i, 0)),
                  pl.BlockSpec((k, n), lambda i: (0, 0))],
        out_specs=pl.BlockSpec((tm, n), lambda i: (i, 0)),
        out_shape=jax.ShapeDtypeStruct((m, n), out_dtype),
        compiler_params=_params("parallel"),
        name="matmul",
    )(x, w)


def _outproj_ln_kernel(a_ref, b_ref, x_ref, wa_ref, wb_ref, g_ref, beta_ref, o_ref):
    mix = _dot(a_ref[...].astype(BF16), wa_ref[...]) + _dot(b_ref[...].astype(BF16), wb_ref[...])
    o_ref[...] = _layer_norm(DEEPNORM_ALPHA * x_ref[...] + mix, g_ref[...], beta_ref[...])


def outproj_ln(a, b, x, w_out, g, beta):
    t, d = x.shape
    na, nb = a.shape[1], b.shape[1]
    tm = PROJ_TM
    wa = w_out[:na].astype(BF16)
    wb = w_out[na:].astype(BF16)
    return pl.pallas_call(
        _outproj_ln_kernel,
        grid=(t // tm,),
        in_specs=[pl.BlockSpec((tm, na), lambda i: (i, 0)),
                  pl.BlockSpec((tm, nb), lambda i: (i, 0)),
                  pl.BlockSpec((tm, d), lambda i: (i, 0)),
                  pl.BlockSpec((na, d), lambda i: (0, 0)),
                  pl.BlockSpec((nb, d), lambda i: (0, 0)),
                  pl.BlockSpec((1, d), lambda i: (0, 0)),
                  pl.BlockSpec((1, d), lambda i: (0, 0))],
        out_specs=pl.BlockSpec((tm, d), lambda i: (i, 0)),
        out_shape=jax.ShapeDtypeStruct((t, d), F32),
        compiler_params=_params("parallel"),
        name="outproj_ln",
    )(a, b, x, wa, wb, g.reshape(1, d), beta.reshape(1, d))


def _xattn_kernel(x_ref, wq_ref, kv_ref, wo_ref, g_ref, beta_ref, o_ref):
    x = x_ref[...]
    q = _dot(x.astype(BF16), wq_ref[...])
    hw = XATTN_HEADS * XATTN_DIM
    heads = []
    for h in range(XATTN_HEADS):
        lo = h * XATTN_DIM
        qh = q[:, lo:lo + XATTN_DIM].astype(BF16)
        kh = kv_ref[0, :, lo:lo + XATTN_DIM]
        vh = kv_ref[0, :, hw + lo:hw + lo + XATTN_DIM]
        s = _dot_nt(qh, kh) * (XATTN_DIM ** -0.5)
        m = jnp.max(s, axis=-1, keepdims=True)
        e = jnp.exp(s - m)
        p = e / jnp.sum(e, axis=-1, keepdims=True)
        heads.append(_dot(p.astype(BF16), vh))
    o = jnp.concatenate(heads, axis=-1)
    xa = _dot(o.astype(BF16), wo_ref[...])
    o_ref[...] = _layer_norm(DEEPNORM_ALPHA * x + xa, g_ref[...], beta_ref[...])


def xattn_ln(x, kv, w_q, w_o, g, beta, seq):
    t, d = x.shape
    tm = PROJ_TM
    per_b = seq // tm
    mlen = kv.shape[1]
    return pl.pallas_call(
        _xattn_kernel,
        grid=(t // tm,),
        in_specs=[pl.BlockSpec((tm, d), lambda i: (i, 0)),
                  pl.BlockSpec((d, d), lambda i: (0, 0)),
                  pl.BlockSpec((1, mlen, 2 * d), lambda i: (i // per_b, 0, 0)),
                  pl.BlockSpec((d, d), lambda i: (0, 0)),
                  pl.BlockSpec((1, d), lambda i: (0, 0)),
                  pl.BlockSpec((1, d), lambda i: (0, 0))],
        out_specs=pl.BlockSpec((tm, d), lambda i: (i, 0)),
        out_shape=jax.ShapeDtypeStruct((t, d), F32),
        compiler_params=_params("parallel"),
        name="xattn_ln",
    )(x, w_q.astype(BF16), kv, w_o.astype(BF16), g.reshape(1, d), beta.reshape(1, d))


def _router_kernel(x_ref, wt_ref, b_ref, tri_ref, ltri_ref, row_ref, gate_ref, off_ref, cnt_ref, base_ref,
                   carry_ref):
    i = pl.program_id(0)

    @pl.when(i == 0)
    def _():
        carry_ref[...] = jnp.zeros_like(carry_ref)

    tm = x_ref.shape[0]
    logits = _dot_nt(wt_ref[...], x_ref[...].astype(BF16)) + b_ref[...]
    e_iota = lax.broadcasted_iota(I32, (N_EXPERTS, tm), 0)
    work = logits
    vals, hots = [], []
    for _ in range(TOP_K):
        m = jnp.max(work, axis=0, keepdims=True)
        idx = jnp.min(jnp.where(work == m, e_iota, N_EXPERTS), axis=0, keepdims=True)
        hot = e_iota == idx
        vals.append(m)
        hots.append(hot)
        work = jnp.where(hot, -jnp.inf, work)
    exps = [jnp.exp(v - vals[0]) for v in vals]
    den = exps[0] + exps[1] + exps[2] + exps[3]
    gate_ref[...] = jnp.concatenate([e / den for e in exps], axis=0)

    hot_all = jnp.zeros((N_EXPERTS, tm), F32)
    for hot in hots:
        hot_all = hot_all + jnp.where(hot, 1.0, 0.0)
    rank = _dot(hot_all.astype(BF16), tri_ref[...])
    n = jnp.sum(hot_all, axis=1, keepdims=True)
    seg = jnp.floor((n + (SEG_ALIGN - 1)) * (1.0 / SEG_ALIGN))
    seg = jnp.broadcast_to(seg, (N_EXPERTS, LANES))
    off = _dot(ltri_ref[...], seg.astype(BF16))
    where_row = off[:, 0:1] * SEG_ALIGN + rank
    row_ref[...] = jnp.concatenate(
        [jnp.sum(jnp.where(hot, where_row, 0.0), axis=0, keepdims=True) for hot in hots], axis=0).astype(I32)
    off_ref[0] = off * SEG_ALIGN
    cnt_ref[0] = seg * SEG_ALIGN
    base_ref[0] = carry_ref[...]
    carry_ref[...] = carry_ref[...] + seg * SEG_ALIGN


def moe_route(x, w_router, b_router):
    t, d = x.shape
    tm = MOE_TM
    nt = t // tm
    tri = jnp.asarray(np.triu(np.ones((tm, tm), np.float32), 1), BF16)
    ltri = jnp.asarray(np.tril(np.ones((N_EXPERTS, N_EXPERTS), np.float32), -1), BF16)
    tab = pl.BlockSpec((1, N_EXPERTS, LANES), lambda i: (i, 0, 0))
    tab_shape = jax.ShapeDtypeStruct((nt, N_EXPERTS, LANES), F32)
    return pl.pallas_call(
        _router_kernel,
        grid=(nt,),
        in_specs=[pl.BlockSpec((tm, d), lambda i: (i, 0)),
                  pl.BlockSpec((N_EXPERTS, d), lambda i: (0, 0)),
                  pl.BlockSpec((N_EXPERTS, 1), lambda i: (0, 0)),
                  pl.BlockSpec((tm, tm), lambda i: (0, 0)),
                  pl.BlockSpec((N_EXPERTS, N_EXPERTS), lambda i: (0, 0))],
        out_specs=[pl.BlockSpec((TOP_K, tm), lambda i: (0, i)),
                   pl.BlockSpec((TOP_K, tm), lambda i: (0, i)),
                   tab, tab, tab],
        out_shape=[jax.ShapeDtypeStruct((TOP_K, t), I32),
                   jax.ShapeDtypeStruct((TOP_K, t), F32),
                   tab_shape, tab_shape, tab_shape],
        scratch_shapes=[pltpu.VMEM((N_EXPERTS, LANES), F32)],
        compiler_params=_params("arbitrary"),
        name="moe_router",
    )(x, w_router.T.astype(BF16), b_router.reshape(N_EXPERTS, 1), tri, ltri)


def _segment_copies(off_ref, cnt_ref, dst_ref, make_copy, wait):
    bits = []
    b = MOE_TM
    while b >= SEG_ALIGN:
        bits.append(b)
        b //= 2

    def per_expert(e, c):
        n = cnt_ref[0, 0, e]
        off = off_ref[0, 0, e]
        dst = dst_ref[0, 0, e]
        for b in bits:
            @pl.when((n & b) != 0)
            def _():
                done = n & (-2 * b)
                cp = make_copy(pl.multiple_of(off + done, SEG_ALIGN), pl.multiple_of(dst + done, SEG_ALIGN), b)
                if wait:
                    cp.wait()
                else:
                    cp.start()
        return c

    lax.fori_loop(0, N_EXPERTS, per_expert, 0)


def _dispatch_kernel(off_ref, cnt_ref, dst_ref, poff_ref, pcnt_ref, pdst_ref, zero_ref, tail_cnt_ref, tail_dst_ref,
                     rest_ref, x_ref, row_ref, xs_hbm, buf, sem):
    i = pl.program_id(0)
    slot = lax.rem(i, 2)
    tm = x_ref.shape[0]
    xb = x_ref[...].astype(BF16)
    rows = [row_ref[k:k + 1, :] for k in range(TOP_K)]
    for c in range(MOE_RT // MOE_CHUNK):
        rr = c * MOE_CHUNK + lax.broadcasted_iota(I32, (MOE_CHUNK, tm), 0)
        perm = jnp.where(rr == rows[0], 1.0, 0.0)
        for k in range(1, TOP_K):
            perm = perm + jnp.where(rr == rows[k], 1.0, 0.0)
        buf[slot, c * MOE_CHUNK:(c + 1) * MOE_CHUNK, :] = _dot(perm.astype(BF16), xb)

    def copy_from(s):
        def copy(src_row, dst_row, n):
            return pltpu.make_async_copy(buf.at[s, pl.ds(src_row, n)], xs_hbm.at[pl.ds(dst_row, n)], sem.at[s])
        return copy

    @pl.when(i > 0)
    def _():
        _segment_copies(poff_ref, pcnt_ref, pdst_ref, copy_from(1 - slot), wait=True)

    _segment_copies(off_ref, cnt_ref, dst_ref, copy_from(slot), wait=False)

    @pl.when(i == pl.num_programs(0) - 1)
    def _():
        _segment_copies(off_ref, cnt_ref, dst_ref, copy_from(slot), wait=True)
        buf[0, 0:MOE_BM, :] = jnp.zeros((MOE_BM, buf.shape[2]), F32)
        zeros = copy_from(0)
        for wait in (False, True):
            _segment_copies(zero_ref, tail_cnt_ref, tail_dst_ref, zeros, wait=wait)

            def rest(j, c):
                cp = zeros(0, pl.multiple_of(rest_ref[0] + j * MOE_BM, MOE_BM), MOE_BM)
                cp.wait() if wait else cp.start()
                return c
            lax.fori_loop(0, rest_ref[1], rest, 0)


def _seg_spec(n_tiles, shift=0):
    def index(i):
        return (jnp.clip(i + shift, 0, n_tiles - 1), 0, 0)
    return pl.BlockSpec((1, 1, N_EXPERTS), index, memory_space=pltpu.SMEM)


def _smem_whole(shape):
    return pl.BlockSpec(shape, lambda i: (0,) * len(shape), memory_space=pltpu.SMEM)


def moe_dispatch(x, row, seg_off, seg_cnt, seg_dst, tail_cnt, tail_dst, rest, n_rows):
    t, d = x.shape
    tm = MOE_TM
    nt = t // tm
    tab = (1, 1, N_EXPERTS)
    return pl.pallas_call(
        _dispatch_kernel,
        grid=(nt,),
        in_specs=[_seg_spec(nt), _seg_spec(nt), _seg_spec(nt),
                  _seg_spec(nt, -1), _seg_spec(nt, -1), _seg_spec(nt, -1),
                  _smem_whole(tab), _smem_whole(tab), _smem_whole(tab), _smem_whole((2,)),
                  pl.BlockSpec((tm, d), lambda i: (i, 0)),
                  pl.BlockSpec((TOP_K, tm), lambda i: (0, i))],
        out_specs=pl.BlockSpec(memory_space=pl.ANY),
        out_shape=jax.ShapeDtypeStruct((n_rows, d), F32),
        scratch_shapes=[pltpu.VMEM((2, MOE_RT, d), F32), pltpu.SemaphoreType.DMA((2,))],
        compiler_params=_params("arbitrary"),
        name="moe_dispatch",
    )(seg_off, seg_cnt, seg_dst, seg_off, seg_cnt, seg_dst, jnp.zeros(tab, I32), tail_cnt.reshape(tab),
      tail_dst.reshape(tab), rest, x, row)


def _expert_kernel(blk_e_ref, nused_ref, xs_ref, wgu_ref, bgu_ref, wdn_ref, bdn_ref, ys_ref, wgu_bf, wdn_bf):
    i = pl.program_id(0)
    used = i < nused_ref[0]
    new_expert = (i == 0) | (blk_e_ref[i] != blk_e_ref[jnp.maximum(i - 1, 0)])

    @pl.when(used & new_expert)
    def _():
        rows = 128
        for r in range(0, wgu_bf.shape[0], rows):
            wgu_bf[r:r + rows, :] = wgu_ref[0, r:r + rows, :].astype(BF16)
        for r in range(0, wdn_bf.shape[0], rows):
            wdn_bf[r:r + rows, :] = wdn_ref[0, r:r + rows, :].astype(BF16)

    @pl.when(used)
    def _():
        h = _dot(xs_ref[...].astype(BF16), wgu_bf[...]) + bgu_ref[0]
        h_gate = jnp.minimum(h[:, :D_EXPERT], SWIGLU_LIMIT)
        h_up = jnp.clip(h[:, D_EXPERT:], -SWIGLU_LIMIT, SWIGLU_LIMIT)
        act = (h_up + 1.0) * (h_gate * _sigmoid(h_gate * SWIGLU_ALPHA))
        ys_ref[...] = _dot(act.astype(BF16), wdn_bf[...]) + bdn_ref[0]

    @pl.when(jnp.logical_not(used))
    def _():
        ys_ref[...] = jnp.zeros_like(ys_ref)


def moe_experts(xs, blk_e, nused, w_gu, b_gu, w_dn, b_dn):
    n_rows, d = xs.shape
    bm = MOE_BM
    nb = n_rows // bm
    grid_spec = pltpu.PrefetchScalarGridSpec(
        num_scalar_prefetch=2,
        grid=(nb,),
        in_specs=[pl.BlockSpec((bm, d), lambda i, be, nu: (i, 0)),
                  pl.BlockSpec((1, d, 2 * D_EXPERT), lambda i, be, nu: (be[i], 0, 0)),
                  pl.BlockSpec((1, 1, 2 * D_EXPERT), lambda i, be, nu: (be[i], 0, 0)),
                  pl.BlockSpec((1, D_EXPERT, d), lambda i, be, nu: (be[i], 0, 0)),
                  pl.BlockSpec((1, 1, d), lambda i, be, nu: (be[i], 0, 0))],
        out_specs=pl.BlockSpec((bm, d), lambda i, be, nu: (i, 0)),
        scratch_shapes=[pltpu.VMEM((d, 2 * D_EXPERT), BF16), pltpu.VMEM((D_EXPERT, d), BF16)],
    )
    return pl.pallas_call(
        _expert_kernel,
        grid_spec=grid_spec,
        out_shape=jax.ShapeDtypeStruct((n_rows, d), F32),
        compiler_params=_params("arbitrary"),
        name="moe_experts",
    )(blk_e, nused, xs, w_gu, b_gu.reshape(N_EXPERTS, 1, -1), w_dn, b_dn.reshape(N_EXPERTS, 1, -1))


def _combine_ln_kernel(off_ref, cnt_ref, dst_ref, noff_ref, ncnt_ref, ndst_ref, ys_hbm, row_ref, gate_ref, x_ref,
                       g_ref, beta_ref, o_ref, buf, sem):
    i = pl.program_id(0)
    slot = lax.rem(i, 2)
    tm = x_ref.shape[0]

    def copy_into(s):
        def copy(buf_row, ys_row, n):
            return pltpu.make_async_copy(ys_hbm.at[pl.ds(ys_row, n)], buf.at[s, pl.ds(buf_row, n)], sem.at[s])
        return copy

    @pl.when(i == 0)
    def _():
        buf[...] = jnp.zeros_like(buf)
        _segment_copies(off_ref, cnt_ref, dst_ref, copy_into(0), wait=False)

    @pl.when(i + 1 < pl.num_programs(0))
    def _():
        _segment_copies(noff_ref, ncnt_ref, ndst_ref, copy_into(1 - slot), wait=False)

    rows = [jnp.broadcast_to(row_ref[:, k:k + 1], (tm, MOE_CHUNK)) for k in range(TOP_K)]
    gates = [jnp.broadcast_to(gate_ref[:, k:k + 1], (tm, MOE_CHUNK)) for k in range(TOP_K)]
    lane = lax.broadcasted_iota(I32, (tm, MOE_CHUNK), 1)
    _segment_copies(off_ref, cnt_ref, dst_ref, copy_into(slot), wait=True)
    ff = jnp.zeros((tm, x_ref.shape[1]), F32)
    for c in range(MOE_RT // MOE_CHUNK):
        rr = lane + c * MOE_CHUNK
        mix = jnp.where(rr == rows[0], gates[0], 0.0)
        for k in range(1, TOP_K):
            mix = mix + jnp.where(rr == rows[k], gates[k], 0.0)
        ff = ff + _dot(mix.astype(BF16), buf[slot, c * MOE_CHUNK:(c + 1) * MOE_CHUNK, :].astype(BF16))
    o_ref[...] = _layer_norm(DEEPNORM_ALPHA * x_ref[...] + ff, g_ref[...], beta_ref[...])


def moe_combine_ln(ys, row_t, gate_t, seg_off, seg_cnt, seg_dst, x, g, beta):
    t, d = x.shape
    tm = MOE_TM
    nt = t // tm
    return pl.pallas_call(
        _combine_ln_kernel,
        grid=(nt,),
        in_specs=[_seg_spec(nt), _seg_spec(nt), _seg_spec(nt),
                  _seg_spec(nt, 1), _seg_spec(nt, 1), _seg_spec(nt, 1),
                  pl.BlockSpec(memory_space=pl.ANY),
                  pl.BlockSpec((tm, TOP_K), lambda i: (i, 0)),
                  pl.BlockSpec((tm, TOP_K), lambda i: (i, 0)),
                  pl.BlockSpec((tm, d), lambda i: (i, 0)),
                  pl.BlockSpec((1, d), lambda i: (0, 0)),
                  pl.BlockSpec((1, d), lambda i: (0, 0))],
        out_specs=pl.BlockSpec((tm, d), lambda i: (i, 0)),
        out_shape=jax.ShapeDtypeStruct((t, d), F32),
        scratch_shapes=[pltpu.VMEM((2, MOE_RT, d), F32), pltpu.SemaphoreType.DMA((2,))],
        compiler_params=_params("arbitrary"),
        name="moe_combine_ln",
    )(seg_off, seg_cnt, seg_dst, seg_off, seg_cnt, seg_dst, ys, row_t, gate_t, x, g.reshape(1, d),
      beta.reshape(1, d))


def moe_ln(x, w_router, b_router, w_gu, b_gu, w_dn, b_dn, g, beta):
    t, d = x.shape
    bm = MOE_BM
    nt = t // MOE_TM
    row, gate, off, cnt, base = moe_route(x, w_router, b_router)
    seg_off = off[:, :, 0].astype(I32)
    seg_cnt = cnt[:, :, 0].astype(I32)
    seg_base = base[:, :, 0].astype(I32)
    total = seg_base[-1] + seg_cnt[-1]
    padded = (total + bm - 1) // bm * bm
    pend = jnp.cumsum(padded)
    pstart = pend - padded
    n_rows = (t * TOP_K + SEG_ALIGN * N_EXPERTS * nt) // bm * bm + N_EXPERTS * bm
    nb = n_rows // bm
    blk_row = jnp.arange(nb, dtype=I32) * bm
    blk_e = jnp.minimum(jnp.sum((pend[None, :] <= blk_row[:, None]).astype(I32), axis=1), N_EXPERTS - 1)
    nused = (pend[-1] // bm).astype(I32).reshape(1)
    seg_dst = (pstart[None, :] + seg_base).reshape(nt, 1, N_EXPERTS)
    seg_off = seg_off.reshape(nt, 1, N_EXPERTS)
    seg_cnt = seg_cnt.reshape(nt, 1, N_EXPERTS)
    rest = jnp.stack([pend[-1], nb - nused[0]]).astype(I32)
    xs = moe_dispatch(x, row, seg_off, seg_cnt, seg_dst, padded - total, pstart + total, rest, n_rows)
    ys = moe_experts(xs, blk_e, nused, w_gu, b_gu, w_dn, b_dn)
    return moe_combine_ln(ys, row.T, gate.T, seg_off, seg_cnt, seg_dst, x, g, beta)


def _sgu_kernel(u_ref, v_ref, lng_ref, lnb_ref, w_ref, bs_ref, o_ref):
    tm = u_ref.shape[0]
    row = lax.broadcasted_iota(I32, (SGU_CHUNK, SGU_CHUNK), 0)
    col = lax.broadcasted_iota(I32, (SGU_CHUNK, SGU_CHUNK), 1)
    causal = row >= col
    for g in range(SGU_GROUPS):
        lo = g * SGU_CH
        vg = _gelu(v_ref[:, lo:lo + SGU_CH])
        vg = _layer_norm(vg, lng_ref[g:g + 1, :], lnb_ref[g:g + 1, :]).astype(BF16)
        wg = jnp.where(causal, w_ref[g], 0.0).astype(BF16)
        bias = bs_ref[:, g:g + 1]
        for n in range(tm // SGU_CHUNK):
            r0 = n * SGU_CHUNK
            mix = _dot(wg, vg[r0:r0 + SGU_CHUNK]) + bias
            o_ref[r0:r0 + SGU_CHUNK, lo:lo + SGU_CH] = _gelu(u_ref[r0:r0 + SGU_CHUNK, lo:lo + SGU_CH]) * mix


def sgu_mixer(u, v, ln_g, ln_b, w_s, b_s):
    t, w = u.shape
    tm = PROJ_TM
    return pl.pallas_call(
        _sgu_kernel,
        grid=(t // tm,),
        in_specs=[pl.BlockSpec((tm, w), lambda i: (i, 0)),
                  pl.BlockSpec((tm, w), lambda i: (i, 0)),
                  pl.BlockSpec((SGU_GROUPS, SGU_CH), lambda i: (0, 0)),
                  pl.BlockSpec((SGU_GROUPS, SGU_CH), lambda i: (0, 0)),
                  pl.BlockSpec((SGU_GROUPS, SGU_CHUNK, SGU_CHUNK), lambda i: (0, 0, 0)),
                  pl.BlockSpec((SGU_CHUNK, SGU_GROUPS), lambda i: (0, 0))],
        out_specs=pl.BlockSpec((tm, w), lambda i: (i, 0)),
        out_shape=jax.ShapeDtypeStruct((t, w), F32),
        compiler_params=_params("parallel"),
        name="sgu_mixer",
    )(u, v, ln_g, ln_b, w_s, b_s.T)


def _conv_kernel(h_ref, bg_ref, cg_ref, w_ref, b_ref, o_ref, carry_ref):
    @pl.when(pl.program_id(1) == 0)
    def _():
        carry_ref[...] = jnp.zeros_like(carry_ref)

    z = cg_ref[...] * h_ref[...]
    tm = z.shape[0]
    row = lax.broadcasted_iota(I32, z.shape, 0)
    prev = carry_ref[...]
    z1 = jnp.where(row == 0, prev[7:8, :], pltpu.roll(z, 1, 0))
    z2 = jnp.where(row == 0, prev[6:7, :], jnp.where(row == 1, prev[7:8, :], pltpu.roll(z, 2, 0)))
    y = w_ref[0:1, :] * z2 + w_ref[1:2, :] * z1 + w_ref[2:3, :] * z + b_ref[...]
    o_ref[...] = bg_ref[...] * y
    carry_ref[...] = z[tm - 8:tm, :]


def conv_mixer(proj, col0, conv_w, conv_b, batch, seq):
    tm = PROJ_TM
    per_b = seq // tm
    c = CONV_CH

    def spec(j):
        return pl.BlockSpec((tm, c), lambda b, i: (b * per_b + i, j))

    return pl.pallas_call(
        _conv_kernel,
        grid=(batch, per_b),
        in_specs=[spec(col0), spec(col0 + 1), spec(col0 + 2),
                  pl.BlockSpec((3, c), lambda b, i: (0, 0)),
                  pl.BlockSpec((1, c), lambda b, i: (0, 0))],
        out_specs=pl.BlockSpec((tm, c), lambda b, i: (b * per_b + i, 0)),
        out_shape=jax.ShapeDtypeStruct((batch * seq, c), F32),
        scratch_shapes=[pltpu.VMEM((8, c), F32)],
        compiler_params=_params("arbitrary", "arbitrary"),
        name="conv_mixer",
    )(proj, proj, proj, conv_w, conv_b.reshape(1, c))


def _split3(x):
    hi = x.astype(BF16)
    r1 = x - hi.astype(F32)
    mid = r1.astype(BF16)
    lo = (r1 - mid.astype(F32)).astype(BF16)
    return hi, mid, lo


def _hgrn_kernel(q_ref, f_ref, i_ref, g_ref, lbl_ref, ng_ref, o_ref, state_ref, *, layer):
    c = HGRN_STEP
    dk = HGRN_DK
    n_chunks = q_ref.shape[0] // c
    lw = lbl_ref[...]
    lw = jnp.exp(lw - jnp.max(lw, axis=0, keepdims=True))
    lw = lw / jnp.sum(lw, axis=0, keepdims=True)
    lb = jnp.sum(lw[1:layer + 1], axis=0, keepdims=True)
    row = lax.broadcasted_iota(I32, (c, c), 0)
    col = lax.broadcasted_iota(I32, (c, c), 1)
    tril = jnp.where(row >= col, 1.0, 0.0).astype(BF16)
    sub = lax.broadcasted_iota(I32, (8, dk), 0)

    @pl.when(pl.program_id(1) == 0)
    def _():
        state_ref[...] = jnp.zeros_like(state_ref)

    def chunk(ci, carry):
        r0 = pl.multiple_of(ci * c, c)
        z = f_ref[pl.ds(r0, c), :]
        k_all = (1.0 - lb) * _sigmoid(-z)
        log_f = jnp.log1p(-k_all)
        hi, mid, lo = _split3(log_f)
        b_all = _dot(tril, hi) + _dot(tril, mid) + _dot(tril, lo)
        q_all = q_ref[pl.ds(r0, c), :]
        v_all = i_ref[pl.ds(r0, c), :]
        g_all = g_ref[pl.ds(r0, c), :]
        outs = []
        for h in range(HGRN_HEADS):
            cols = slice(h * dk, (h + 1) * dk)
            q, k, v, b = q_all[:, cols], k_all[:, cols], v_all[:, cols], b_all[:, cols]
            state_t = state_ref[h]
            o = _dot_nt((q * jnp.exp(b)).astype(BF16), state_t.astype(BF16))

            acc = [o[8 * j:8 * j + 8] for j in range(c // 8)]
            for s in range(c):
                bs = b[s:s + 1, :]
                ks = k[s:s + 1, :]
                vs = v[s:s + 1, :]
                j0 = s // 8
                for j in range(j0, c // 8):
                    dlt = b[8 * j:8 * j + 8] - bs
                    if j == j0:
                        dlt = jnp.where(sub >= s - 8 * j0, dlt, NEG_INF)
                    a = jnp.sum(q[8 * j:8 * j + 8] * ks * jnp.exp(dlt), axis=-1, keepdims=True)
                    acc[j] = acc[j] + a * vs
            o = jnp.concatenate(acc, axis=0)

            b_last = b[c - 1:c, :]
            kd = (k * jnp.exp(b_last - b)).astype(BF16)
            state_ref[h] = jnp.exp(b_last) * state_t + _dot_tn(v.astype(BF16), kd)
            outs.append(o * lax.rsqrt(jnp.mean(o * o, axis=-1, keepdims=True) + RMS_EPS))
        o_ref[pl.ds(r0, c), :] = jnp.concatenate(outs, axis=1) * ng_ref[...] * (g_all * _sigmoid(g_all))
        return carry

    lax.fori_loop(0, n_chunks, chunk, 0)


def hgrn2_mixer(proj, lb_logits, norm_g, layer, batch, seq):
    w = HGRN_W
    seg = min(seq, HGRN_SEG)
    per_b = seq // seg

    def spec(grp):
        return pl.BlockSpec((seg, w), lambda b, i: (b * per_b + i, grp))

    return pl.pallas_call(
        functools.partial(_hgrn_kernel, layer=layer),
        grid=(batch, per_b),
        in_specs=[spec(0), spec(1), spec(2), spec(3),
                  pl.BlockSpec((DEPTH, w), lambda b, i: (0, 0)),
                  pl.BlockSpec((1, w), lambda b, i: (0, 0))],
        out_specs=pl.BlockSpec((seg, w), lambda b, i: (b * per_b + i, 0)),
        out_shape=jax.ShapeDtypeStruct((batch * seq, w), F32),
        scratch_shapes=[pltpu.VMEM((HGRN_HEADS, HGRN_DK, HGRN_DK), F32)],
        compiler_params=_params("arbitrary", "arbitrary"),
        name="hgrn2_mixer",
    )(proj, proj, proj, proj, lb_logits, norm_g.reshape(1, w))


EVEN_T_ROWS = NSA_Q + 4 * 2 * HEAD_DIM + 32
EVEN_S_COLS = 4 * LANES + 2 * SGU_W


def _even_proj_kernel(x_ref, wt_ref, ws_ref, cos_ref, sin_ref, qt_ref, qrt_ref, ks_ref, kw_ref,
                      vs_ref, vw_ref, gt_ref, kvc_ref, u_ref, v_ref, *, per_b):
    tm = x_ref.shape[0]
    xb = x_ref[...].astype(BF16)
    st = _dot_nt(wt_ref[...], xb)
    cos = cos_ref[...]
    sin = sin_ref[...]
    half = HEAD_DIM // 2
    scale = HEAD_DIM ** -0.5 * LOG2_E

    def rope(blk):
        x1, x2 = blk[:half], blk[half:]
        return jnp.concatenate([x1 * cos - x2 * sin, x2 * cos + x1 * sin], axis=0)

    for hh in range(NSA_HEADS):
        blk = st[hh * HEAD_DIM:(hh + 1) * HEAD_DIM]
        qt_ref[hh * HEAD_DIM:(hh + 1) * HEAD_DIM, :] = (blk * scale).astype(BF16)
        qrt_ref[hh * HEAD_DIM:(hh + 1) * HEAD_DIM, :] = (rope(blk) * scale).astype(BF16)

    kk = jnp.concatenate([rope(st[NSA_Q + j * HEAD_DIM:NSA_Q + (j + 1) * HEAD_DIM]) for j in range(4)], axis=0)
    kk = kk.T
    pos = lax.rem(pl.program_id(0), per_b) * tm + lax.broadcasted_iota(I32, (tm, LANES), 0)
    lane = lax.broadcasted_iota(I32, (tm, LANES), 1)
    member = jnp.where(lane == pos // SLC_BLOCK, 1.0, 0.0)
    ks_ref[...] = jnp.concatenate([kk[:, :LANES], member], axis=1).astype(BF16)
    kw_ref[...] = kk[:, LANES:].astype(BF16)
    v0 = NSA_Q + 4 * HEAD_DIM
    for j in range(tm // NSA_KT):
        vs_ref[j] = st[v0:v0 + LANES, j * NSA_KT:(j + 1) * NSA_KT].astype(BF16)
    for j in range(tm // LANES):
        vw_ref[j] = st[v0 + LANES:v0 + 2 * LANES, j * LANES:(j + 1) * LANES].astype(BF16)
    gt_ref[...] = _sigmoid(st[v0 + 2 * LANES:v0 + 2 * LANES + 32])

    ss = _dot(xb, ws_ref[...])
    for j in range(4):
        kvc_ref[j] = ss[:, j * LANES:j * LANES + HEAD_DIM]
    u_ref[...] = ss[:, 4 * LANES:4 * LANES + SGU_W]
    v_ref[...] = ss[:, 4 * LANES + SGU_W:]


def even_proj(x, w_in, cos_t, sin_t, seq):
    t, d = x.shape
    tm = PROJ_TM
    per_b = seq // tm
    hd = HEAD_DIM
    kv0 = NSA_Q

    def kvcols(i):
        return w_in[:, kv0 + i * 2 * hd:kv0 + (i + 1) * 2 * hd]

    g0 = kv0 + 6 * 2 * hd
    n_gates = 3 * NSA_HEADS
    wt = jnp.concatenate([w_in[:, :NSA_Q], kvcols(2), kvcols(4), kvcols(3), kvcols(5),
                          w_in[:, g0:g0 + n_gates], jnp.zeros((d, 32 - n_gates), F32)], axis=1).T.astype(BF16)
    zpad = jnp.zeros((d, LANES - hd), F32)
    cmp_cols = []
    for i in (0, 1):
        for g in range(NSA_KV_HEADS):
            cmp_cols += [w_in[:, kv0 + i * 2 * hd + g * hd:kv0 + i * 2 * hd + (g + 1) * hd], zpad]
    ws = jnp.concatenate(cmp_cols + [w_in[:, g0 + n_gates:]], axis=1).astype(BF16)
    half = hd // 2
    tok = lambda i: (i, 0)
    tok_t = lambda i: (0, i)
    return pl.pallas_call(
        functools.partial(_even_proj_kernel, per_b=per_b),
        grid=(t // tm,),
        in_specs=[pl.BlockSpec((tm, d), tok),
                  pl.BlockSpec((EVEN_T_ROWS, d), lambda i: (0, 0)),
                  pl.BlockSpec((d, EVEN_S_COLS), lambda i: (0, 0)),
                  pl.BlockSpec((half, tm), tok_t),
                  pl.BlockSpec((half, tm), tok_t)],
        out_specs=[pl.BlockSpec((NSA_Q, tm), tok_t),
                   pl.BlockSpec((NSA_Q, tm), tok_t),
                   pl.BlockSpec((tm, 2 * LANES), tok),
                   pl.BlockSpec((tm, LANES), tok),
                   pl.BlockSpec((tm // NSA_KT, LANES, NSA_KT), lambda i: (i, 0, 0)),
                   pl.BlockSpec((tm // LANES, LANES, LANES), lambda i: (i, 0, 0)),
                   pl.BlockSpec((32, tm), tok_t),
                   pl.BlockSpec((4, tm, hd), lambda i: (0, i, 0)),
                   pl.BlockSpec((tm, SGU_W), tok),
                   pl.BlockSpec((tm, SGU_W), tok)],
        out_shape=[jax.ShapeDtypeStruct((NSA_Q, t), BF16),
                   jax.ShapeDtypeStruct((NSA_Q, t), BF16),
                   jax.ShapeDtypeStruct((t, 2 * LANES), BF16),
                   jax.ShapeDtypeStruct((t, LANES), BF16),
                   jax.ShapeDtypeStruct((t // NSA_KT, LANES, NSA_KT), BF16),
                   jax.ShapeDtypeStruct((t // LANES, LANES, LANES), BF16),
                   jax.ShapeDtypeStruct((32, t), F32),
                   jax.ShapeDtypeStruct((4, t, hd), F32),
                   jax.ShapeDtypeStruct((t, SGU_W), F32),
                   jax.ShapeDtypeStruct((t, SGU_W), F32)],
        compiler_params=_params("parallel"),
        name="even_proj",
    )(x, wt, ws, cos_t, sin_t)


def _compress_kernel(kvc_ref, pos_ref, w1_ref, w2_ref, kc_ref, vct_ref):
    ncp = kc_ref.shape[1]
    assert CMP_BLOCK == 2 * CMP_STRIDE
    for i in range(2):
        outs = []
        for g in range(NSA_KV_HEADS):
            first = jnp.zeros((ncp, CMP_HIDDEN), F32)
            second = jnp.zeros((ncp, CMP_HIDDEN), F32)
            for l in range(CMP_STRIDE):
                rows = kvc_ref[i * NSA_KV_HEADS + g, pl.ds(l, ncp, stride=CMP_STRIDE), :]
                first = first + _dot((rows + pos_ref[i, l:l + 1, :]).astype(BF16), w1_ref[i, l])
                l2 = l + CMP_STRIDE
                second = second + _dot((rows + pos_ref[i, l2:l2 + 1, :]).astype(BF16), w1_ref[i, l2])
            hid = _gelu(first + pltpu.roll(second, ncp - 1, 0))
            outs.append(_dot(hid.astype(BF16), w2_ref[i]))
        if i == 0:
            kc_ref[0] = jnp.concatenate(outs, axis=1).astype(BF16)
        else:
            vct_ref[0] = jnp.concatenate(outs, axis=1).T.astype(BF16)


def nsa_compress(kvc, cmp_pos, cmp_w1, cmp_w2, batch, seq):
    hd = HEAD_DIM
    ncp = seq // CMP_STRIDE
    w1 = cmp_w1.reshape(2, CMP_BLOCK, hd, CMP_HIDDEN).astype(BF16)
    return pl.pallas_call(
        _compress_kernel,
        grid=(batch,),
        in_specs=[pl.BlockSpec((4, seq, hd), lambda b: (0, b, 0)),
                  pl.BlockSpec((2, CMP_BLOCK, hd), lambda b: (0, 0, 0)),
                  pl.BlockSpec((2, CMP_BLOCK, hd, CMP_HIDDEN), lambda b: (0, 0, 0, 0)),
                  pl.BlockSpec((2, CMP_HIDDEN, hd), lambda b: (0, 0, 0))],
        out_specs=[pl.BlockSpec((1, ncp, 2 * hd), lambda b: (b, 0, 0)),
                   pl.BlockSpec((1, 2 * hd, ncp), lambda b: (b, 0, 0))],
        out_shape=[jax.ShapeDtypeStruct((batch, ncp, 2 * hd), BF16),
                   jax.ShapeDtypeStruct((batch, 2 * hd, ncp), BF16)],
        compiler_params=_params("parallel"),
        name="nsa_compress",
    )(kvc, cmp_pos, w1, cmp_w2.astype(BF16))


def _softmax2_cols(s, mask):
    sm = jnp.where(mask, s, NEG_INF)
    m = jnp.max(sm, axis=0, keepdims=True)
    e = jnp.exp2(sm - m)
    return jnp.where(mask, e / jnp.sum(e, axis=0, keepdims=True), 0.0)


def _nsa_kernel(qt_ref, qrt_ref, gt_ref, kc_ref, vct_ref, mct_ref, ks_ref, kw_ref, vs_ref, vw_ref, o_ref,
                s_scr, q_scr, m_scr, l_scr, acc_scr, *, n_top):
    tq = qt_ref.shape[1]
    hg = NSA_GROUP
    hd = HEAD_DIM
    groups = range(NSA_KV_HEADS)
    w = hg * tq
    ncp = kc_ref.shape[1]
    n_slc = mct_ref.shape[0]
    s0 = pl.program_id(1) * tq
    t_lane = s0 + lax.rem(lax.broadcasted_iota(I32, (1, w), 1), tq)
    t_q = s0 + lax.broadcasted_iota(I32, (1, tq), 1)
    zeros_g = jnp.zeros((hd, w), BF16)
    n_win = WINDOW // LANES + tq // LANES
    wt0 = jnp.maximum(s0 // LANES - WINDOW // LANES, 0)
    vrows = [slice(g * hd, (g + 1) * hd) for g in groups]

    def grp(ref, g):
        rows = jnp.concatenate([ref[(g * hg + h) * hd:(g * hg + h + 1) * hd, :] for h in range(hg)], axis=1)
        return jnp.concatenate([rows, zeros_g] if g == 0 else [zeros_g, rows], axis=0)

    q_rot = [grp(qrt_ref, g) for g in groups]
    o_cmp, q_aug = [], []
    for g in groups:
        sc = _dot(kc_ref[0], grp(qt_ref, g))
        c_end = lax.broadcasted_iota(I32, (ncp, w), 0) * CMP_STRIDE + (CMP_BLOCK - 1)
        p = _softmax2_cols(sc, c_end <= t_lane)
        o_cmp.append(_dot(vct_ref[0, vrows[g], :], p.astype(BF16)))
        psum = p[:, 0:tq]
        for h in range(1, hg):
            psum = psum + p[:, h * tq:(h + 1) * tq]
        imp = _dot(mct_ref[...], psum.astype(BF16))

        j_blk = lax.broadcasted_iota(I32, (n_slc, tq), 0)
        cur = t_q // SLC_BLOCK
        forced = (j_blk == 0) | (j_blk == cur) | (j_blk == cur - 1)
        valid = j_blk <= cur
        score = jnp.where(forced, SLC_FORCED_SCORE, jnp.where(valid, imp, -1.0))
        nv = n_slc // 8
        sblk = [score[8 * v:8 * v + 8] for v in range(nv)]
        rank = [jnp.zeros((8, tq), F32) for _ in range(nv)]
        sub = lax.broadcasted_iota(I32, (8, tq), 0)
        for k in range(n_slc):
            sk = score[k:k + 1, :]
            kv_ = k // 8
            for v in range(nv):
                ge = jnp.where(sk >= sblk[v], 1.0, 0.0)
                gt = jnp.where(sk > sblk[v], 1.0, 0.0)
                if v > kv_:
                    beats = ge
                elif v < kv_:
                    beats = gt
                else:
                    beats = jnp.where(sub > k - 8 * kv_, ge, gt)
                rank[v] = rank[v] + beats
        rank = jnp.concatenate(rank, axis=0)
        bias = jnp.where((rank < n_top) & valid, 0.0, NEG_INF)
        bias = jnp.concatenate([bias] * hg, axis=1)
        if n_slc < 2 * hd:
            bias = jnp.concatenate([bias, jnp.zeros((2 * hd - n_slc, w), F32)], axis=0)
        q_aug.append(jnp.concatenate([q_rot[g], bias.astype(BF16)], axis=0))

    for g in groups:
        q_scr[g] = q_aug[g]
        m_scr[g] = jnp.full((1, w), NEG_INF, F32)
        l_scr[g] = jnp.zeros((1, w), F32)
        acc_scr[g] = jnp.zeros((hd, w), F32)

    def slc_scores(kt, slot):
        k_tile = ks_ref[pl.ds(pl.multiple_of(kt * NSA_KT, NSA_KT), NSA_KT), :]
        for g in groups:
            s_scr[slot, g] = _dot(k_tile, q_scr[g])

    def slc_update(kt, slot, diagonal):
        for g in groups:
            s = s_scr[slot, g]
            if diagonal:
                kpos = kt * NSA_KT + lax.broadcasted_iota(I32, (NSA_KT, w), 0)
                s = jnp.where(kpos <= t_lane, s, NEG_INF)
            m = m_scr[g]
            m_new = jnp.maximum(m, jnp.max(s, axis=0, keepdims=True))
            alpha = jnp.exp2(m - m_new)
            pp = jnp.exp2(s - m_new)
            l_scr[g] = l_scr[g] * alpha + jnp.sum(pp, axis=0, keepdims=True)
            acc_scr[g] = acc_scr[g] * alpha + _dot(vs_ref[kt, vrows[g], :], pp.astype(BF16))
            m_scr[g] = m_new

    n_full = s0 // NSA_KT

    def slc_pair(p, c):
        a = 2 * p
        slc_scores(a + 1, 1)
        slc_update(a, 0, False)
        slc_scores(a + 2, 0)
        slc_update(a + 1, 1, False)
        return c

    slc_scores(0, 0)
    lax.fori_loop(0, n_full // 2, slc_pair, 0)
    odd = lax.rem(n_full, 2) == 1

    @pl.when(odd)
    def _():
        slc_scores(n_full, 1)
        slc_update(n_full - 1, 0, False)
        slc_update(n_full, 1, True)

    @pl.when(jnp.logical_not(odd))
    def _():
        slc_update(n_full, 0, True)

    o_slc = [acc_scr[g] / l_scr[g] for g in groups]

    kw0 = pl.multiple_of(wt0 * LANES, LANES)
    k_win = kw_ref[pl.ds(kw0, n_win * LANES), :]
    dist = t_lane - (kw0 + lax.broadcasted_iota(I32, (n_win * LANES, w), 0))
    in_window = (dist >= 0) & (dist < WINDOW)
    o_win = []
    for g in groups:
        sm = jnp.where(in_window, _dot(k_win, q_rot[g]), NEG_INF)
        e = jnp.exp2(sm - jnp.max(sm, axis=0, keepdims=True))
        den = jnp.sum(e, axis=0, keepdims=True)
        eb = e.astype(BF16)
        ow = _dot(vw_ref[wt0, vrows[g], :], eb[0:LANES])
        for j in range(1, n_win):
            ow = ow + _dot(vw_ref[wt0 + j, vrows[g], :], eb[j * LANES:(j + 1) * LANES])
        o_win.append(ow / den)

    heads = []
    for g in groups:
        for h in range(hg):
            r = (g * hg + h) * 3
            cols = slice(h * tq, (h + 1) * tq)
            heads.append(gt_ref[r:r + 1, :] * o_cmp[g][:, cols] + gt_ref[r + 1:r + 2, :] * o_slc[g][:, cols]
                         + gt_ref[r + 2:r + 3, :] * o_win[g][:, cols])
    o_ref[...] = jnp.concatenate(heads, axis=0).T


def _cmp_to_slc_t(seq):
    ncp = seq // CMP_STRIDE
    ns = seq // SLC_BLOCK
    cs = np.arange(ncp)[None, :] * CMP_STRIDE
    ss = np.arange(ns)[:, None] * SLC_BLOCK
    ov = np.clip(np.minimum(cs + CMP_BLOCK, ss + SLC_BLOCK) - np.maximum(cs, ss), 0, None) / CMP_BLOCK
    ov[:, ncp - 1] = 0.0
    return jnp.asarray(ov, BF16)


def nsa_mixer(qt, qrt, gt, kc, vct, ks, kw, vs, vw, batch, seq):
    tq = NSA_TQ
    nq = seq // tq
    ncp = seq // CMP_STRIDE
    n_slc = seq // SLC_BLOCK
    assert n_slc <= 2 * HEAD_DIM and n_slc % 8 == 0 and seq >= WINDOW + tq
    w = NSA_GROUP * tq
    col = lambda b, i: (0, b * nq + i)
    return pl.pallas_call(
        functools.partial(_nsa_kernel, n_top=min(SLC_TOP, n_slc)),
        grid=(batch, nq),
        in_specs=[pl.BlockSpec((NSA_Q, tq), col),
                  pl.BlockSpec((NSA_Q, tq), col),
                  pl.BlockSpec((32, tq), col),
                  pl.BlockSpec((1, ncp, 2 * HEAD_DIM), lambda b, i: (b, 0, 0)),
                  pl.BlockSpec((1, 2 * HEAD_DIM, ncp), lambda b, i: (b, 0, 0)),
                  pl.BlockSpec((n_slc, ncp), lambda b, i: (0, 0)),
                  pl.BlockSpec((seq, 2 * LANES), lambda b, i: (b, 0)),
                  pl.BlockSpec((seq, LANES), lambda b, i: (b, 0)),
                  pl.BlockSpec((seq // NSA_KT, LANES, NSA_KT), lambda b, i: (b, 0, 0)),
                  pl.BlockSpec((seq // LANES, LANES, LANES), lambda b, i: (b, 0, 0))],
        out_specs=pl.BlockSpec((tq, NSA_Q), lambda b, i: (b * nq + i, 0)),
        out_shape=jax.ShapeDtypeStruct((batch * seq, NSA_Q), F32),
        scratch_shapes=[pltpu.VMEM((2, NSA_KV_HEADS, NSA_KT, w), F32),
                        pltpu.VMEM((NSA_KV_HEADS, 4 * HEAD_DIM, w), BF16),
                        pltpu.VMEM((NSA_KV_HEADS, 1, w), F32),
                        pltpu.VMEM((NSA_KV_HEADS, 1, w), F32),
                        pltpu.VMEM((NSA_KV_HEADS, HEAD_DIM, w), F32)],
        compiler_params=_params("parallel", "parallel"),
        name="nsa_mixer",
    )(qt, qrt, gt, kc, vct, _cmp_to_slc_t(seq), ks, kw, vs, vw)


def kernel(x, mem, positions, w_in_even, nsa_cmp_pos, nsa_cmp_w1, nsa_cmp_w2, sgu_ln_g, sgu_ln_b, sgu_w, sgu_b, w_out_even, w_in_odd, hgrn_lb_logits, hgrn_norm_g, conv_w, conv_b, w_out_odd, xattn_w_q, xattn_w_kv, xattn_w_o, ln_g, ln_b, router_w, router_b, expert_w_gu, expert_b_gu, expert_w_dn, expert_b_dn):
    batch, seq, d = x.shape
    t = batch * seq
    cos_t, sin_t = rope_tables_t(positions)
    mem2 = mem.reshape(-1, d)
    xf = x.reshape(t, d)
    for layer in range(DEPTH):
        j = layer // 2
        if layer % 2 == 0:
            qt, qrt, ks, kw, vs, vw, gt, kvc, u, v = even_proj(xf, w_in_even[j], cos_t, sin_t, seq)
            kc, vct = nsa_compress(kvc, nsa_cmp_pos[j], nsa_cmp_w1[j], nsa_cmp_w2[j], batch, seq)
            o_a = nsa_mixer(qt, qrt, gt, kc, vct, ks, kw, vs, vw, batch, seq)
            o_b = sgu_mixer(u, v, sgu_ln_g[j], sgu_ln_b[j], sgu_w[j], sgu_b[j])
            w_out = w_out_even[j]
        else:
            proj = matmul(xf, w_in_odd[j].astype(BF16), F32, PROJ_TM)
            o_a = hgrn2_mixer(proj, hgrn_lb_logits, hgrn_norm_g[j], layer, batch, seq)
            o_b = conv_mixer(proj, 4 * HGRN_W // CONV_CH, conv_w[j], conv_b[j], batch, seq)
            w_out = w_out_odd[j]
        xf = outproj_ln(o_a, o_b, xf, w_out, ln_g[layer, 0], ln_b[layer, 0])
        kv = matmul(mem2, xattn_w_kv[layer].astype(BF16), BF16, mem.shape[1]).reshape(batch, mem.shape[1], 2 * d)
        xf = xattn_ln(xf, kv, xattn_w_q[layer], xattn_w_o[layer], ln_g[layer, 1], ln_b[layer, 1], seq)
        xf = moe_ln(xf, router_w[layer], router_b[layer], expert_w_gu[layer], expert_b_gu[layer],
                    expert_w_dn[layer], expert_b_dn[layer], ln_g[layer, 2], ln_b[layer, 2])
    return xf.reshape(batch, seq, d)
```

```python
import functools

import numpy as np
import jax
import jax.numpy as jnp
from jax import lax
from jax.experimental import pallas as pl
from jax.experimental.pallas import tpu as pltpu

F32 = jnp.float32
BF16 = jnp.bfloat16
I32 = jnp.int32

D_MODEL = 1024
DEPTH = 2
HEAD_DIM = 64
NSA_HEADS = 8
NSA_KV_HEADS = 2
NSA_GROUP = NSA_HEADS // NSA_KV_HEADS
CMP_BLOCK = 32
CMP_STRIDE = 16
CMP_HIDDEN = 256
SLC_BLOCK = 64
SLC_TOP = 16
WINDOW = 512
SLC_FORCED_SCORE = 1e4
SGU_GROUPS = 4
SGU_CH = 128
SGU_CHUNK = 128
HGRN_HEADS = 4
HGRN_DK = 128
HGRN_CHUNK = 64
CONV_CH = 512
XATTN_HEADS = 4
XATTN_DIM = D_MODEL // XATTN_HEADS
N_EXPERTS = 32
TOP_K = 4
D_EXPERT = D_MODEL
SWIGLU_LIMIT = 7.0
SWIGLU_ALPHA = 1.702
ROPE_THETA = 10000.0
LN_EPS = 1e-5
RMS_EPS = 1e-6
NEG_INF = -1e30
DEEPNORM_ALPHA = (2 * DEPTH) ** 0.25
LOG2_E = 1.4426950408889634

NSA_Q = NSA_HEADS * HEAD_DIM
SGU_W = SGU_GROUPS * SGU_CH
HGRN_W = HGRN_HEADS * HGRN_DK

VMEM_LIMIT_BYTES = 56 * 1024 * 1024
LANES = 128

PROJ_TM = 512
NSA_TQ = 128
NSA_KT = 256
HGRN_STEP = 32
HGRN_SEG = 1024
MOE_TM = 512
MOE_BM = 512
SEG_ALIGN = 8
MOE_RT = TOP_K * MOE_TM + SEG_ALIGN * N_EXPERTS
MOE_CHUNK = 256


def _params(*sem):
    return pltpu.CompilerParams(dimension_semantics=sem, vmem_limit_bytes=VMEM_LIMIT_BYTES)


def _dot(a, b):
    return jnp.dot(a, b, preferred_element_type=F32)


def _dot_nt(a, b):
    return lax.dot_general(a, b, (((1,), (1,)), ((), ())), preferred_element_type=F32)


def _dot_tn(a, b):
    return lax.dot_general(a, b, (((0,), (0,)), ((), ())), preferred_element_type=F32)


def _gelu(x):
    return 0.5 * x * (1.0 + jnp.tanh(np.sqrt(2.0 / np.pi).astype(np.float32) * (x + 0.044715 * (x * x * x))))


def _sigmoid(x):
    return 1.0 / (1.0 + jnp.exp(-x))


def _layer_norm(y, g, b):
    mu = jnp.mean(y, axis=-1, keepdims=True)
    d = y - mu
    var = jnp.mean(d * d, axis=-1, keepdims=True)
    return d * lax.rsqrt(var + LN_EPS) * g + b


def _rope_kernel(pos_ref, inv_ref, cos_ref, sin_ref):
    ang = pos_ref[...].astype(F32) * inv_ref[...]
    cos_ref[...] = jnp.cos(ang)
    sin_ref[...] = jnp.sin(ang)


def rope_tables_t(positions):
    t = positions.size
    inv = 1.0 / (ROPE_THETA ** (jnp.arange(0, HEAD_DIM, 2, dtype=F32) / HEAD_DIM))
    tn = min(t, 4096)
    half = HEAD_DIM // 2
    return pl.pallas_call(
        _rope_kernel,
        grid=(t // tn,),
        in_specs=[pl.BlockSpec((1, tn), lambda i: (0, i)),
                  pl.BlockSpec((half, 1), lambda i: (0, 0))],
        out_specs=[pl.BlockSpec((half, tn), lambda i: (0, i))] * 2,
        out_shape=[jax.ShapeDtypeStruct((half, t), F32)] * 2,
        compiler_params=_params("parallel"),
        name="rope_tables",
    )(positions.reshape(1, t), inv.reshape(half, 1))


def _mm_kernel(x_ref, w_ref, o_ref):
    o_ref[...] = _dot(x_ref[...].astype(BF16), w_ref[...]).astype(o_ref.dtype)


def matmul(x, w, out_dtype, tm):
    m, k = x.shape
    n = w.shape[1]
    return pl.pallas_call(
        _mm_kernel,
        grid=(m // tm,),
        in_specs=[pl.BlockSpec((tm, k), lambda i: (i, 0)),
                  pl.BlockSpec((k, n), lambda i: (0, 0))],
        out_specs=pl.BlockSpec((tm, n), lambda i: (i, 0)),
        out_shape=jax.ShapeDtypeStruct((m, n), out_dtype),
        compiler_params=_params("parallel"),
        name="matmul",
    )(x, w)


def _outproj_ln_kernel(a_ref, b_ref, x_ref, wa_ref, wb_ref, g_ref, beta_ref, o_ref):
    mix = _dot(a_ref[...].astype(BF16), wa_ref[...]) + _dot(b_ref[...].astype(BF16), wb_ref[...])
    o_ref[...] = _layer_norm(DEEPNORM_ALPHA * x_ref[...] + mix, g_ref[...], beta_ref[...])


def outproj_ln(a, b, x, w_out, g, beta):
    t, d = x.shape
    na, nb = a.shape[1], b.shape[1]
    tm = PROJ_TM
    wa = w_out[:na].astype(BF16)
    wb = w_out[na:].astype(BF16)
    return pl.pallas_call(
        _outproj_ln_kernel,
        grid=(t // tm,),
        in_specs=[pl.BlockSpec((tm, na), lambda i: (i, 0)),
                  pl.BlockSpec((tm, nb), lambda i: (i, 0)),
                  pl.BlockSpec((tm, d), lambda i: (i, 0)),
                  pl.BlockSpec((na, d), lambda i: (0, 0)),
                  pl.BlockSpec((nb, d), lambda i: (0, 0)),
                  pl.BlockSpec((1, d), lambda i: (0, 0)),
                  pl.BlockSpec((1, d), lambda i: (0, 0))],
        out_specs=pl.BlockSpec((tm, d), lambda i: (i, 0)),
        out_shape=jax.ShapeDtypeStruct((t, d), F32),
        compiler_params=_params("parallel"),
        name="outproj_ln",
    )(a, b, x, wa, wb, g.reshape(1, d), beta.reshape(1, d))


def _xattn_kernel(x_ref, wq_ref, kv_ref, wo_ref, g_ref, beta_ref, o_ref):
    x = x_ref[...]
    q = _dot(x.astype(BF16), wq_ref[...])
    hw = XATTN_HEADS * XATTN_DIM
    heads = []
    for h in range(XATTN_HEADS):
        lo = h * XATTN_DIM
        qh = q[:, lo:lo + XATTN_DIM].astype(BF16)
        kh = kv_ref[0, :, lo:lo + XATTN_DIM]
        vh = kv_ref[0, :, hw + lo:hw + lo + XATTN_DIM]
        s = _dot_nt(qh, kh) * (XATTN_DIM ** -0.5)
        m = jnp.max(s, axis=-1, keepdims=True)
        e = jnp.exp(s - m)
        p = e / jnp.sum(e, axis=-1, keepdims=True)
        heads.append(_dot(p.astype(BF16), vh))
    o = jnp.concatenate(heads, axis=-1)
    xa = _dot(o.astype(BF16), wo_ref[...])
    o_ref[...] = _layer_norm(DEEPNORM_ALPHA * x + xa, g_ref[...], beta_ref[...])


def xattn_ln(x, kv, w_q, w_o, g, beta, seq):
    t, d = x.shape
    tm = PROJ_TM
    per_b = seq // tm
    mlen = kv.shape[1]
    return pl.pallas_call(
        _xattn_kernel,
        grid=(t // tm,),
        in_specs=[pl.BlockSpec((tm, d), lambda i: (i, 0)),
                  pl.BlockSpec((d, d), lambda i: (0, 0)),
                  pl.BlockSpec((1, mlen, 2 * d), lambda i: (i // per_b, 0, 0)),
                  pl.BlockSpec((d, d), lambda i: (0, 0)),
                  pl.BlockSpec((1, d), lambda i: (0, 0)),
                  pl.BlockSpec((1, d), lambda i: (0, 0))],
        out_specs=pl.BlockSpec((tm, d), lambda i: (i, 0)),
        out_shape=jax.ShapeDtypeStruct((t, d), F32),
        compiler_params=_params("parallel"),
        name="xattn_ln",
    )(x, w_q.astype(BF16), kv, w_o.astype(BF16), g.reshape(1, d), beta.reshape(1, d))


def _router_kernel(x_ref, wt_ref, b_ref, tri_ref, ltri_ref, row_ref, gate_ref, off_ref, cnt_ref, base_ref,
                   carry_ref):
    i = pl.program_id(0)

    @pl.when(i == 0)
    def _():
        carry_ref[...] = jnp.zeros_like(carry_ref)

    tm = x_ref.shape[0]
    logits = _dot_nt(wt_ref[...], x_ref[...].astype(BF16)) + b_ref[...]
    e_iota = lax.broadcasted_iota(I32, (N_EXPERTS, tm), 0)
    work = logits
    vals, hots = [], []
    for _ in range(TOP_K):
        m = jnp.max(work, axis=0, keepdims=True)
        idx = jnp.min(jnp.where(work == m, e_iota, N_EXPERTS), axis=0, keepdims=True)
        hot = e_iota == idx
        vals.append(m)
        hots.append(hot)
        work = jnp.where(hot, -jnp.inf, work)
    exps = [jnp.exp(v - vals[0]) for v in vals]
    den = exps[0] + exps[1] + exps[2] + exps[3]
    gate_ref[...] = jnp.concatenate([e / den for e in exps], axis=0)

    hot_all = jnp.zeros((N_EXPERTS, tm), F32)
    for hot in hots:
        hot_all = hot_all + jnp.where(hot, 1.0, 0.0)
    rank = _dot(hot_all.astype(BF16), tri_ref[...])
    n = jnp.sum(hot_all, axis=1, keepdims=True)
    seg = jnp.floor((n + (SEG_ALIGN - 1)) * (1.0 / SEG_ALIGN))
    seg = jnp.broadcast_to(seg, (N_EXPERTS, LANES))
    off = _dot(ltri_ref[...], seg.astype(BF16))
    where_row = off[:, 0:1] * SEG_ALIGN + rank
    row_ref[...] = jnp.concatenate(
        [jnp.sum(jnp.where(hot, where_row, 0.0), axis=0, keepdims=True) for hot in hots], axis=0).astype(I32)
    off_ref[0] = off * SEG_ALIGN
    cnt_ref[0] = seg * SEG_ALIGN
    base_ref[0] = carry_ref[...]
    carry_ref[...] = carry_ref[...] + seg * SEG_ALIGN


def moe_route(x, w_router, b_router):
    t, d = x.shape
    tm = MOE_TM
    nt = t // tm
    tri = jnp.asarray(np.triu(np.ones((tm, tm), np.float32), 1), BF16)
    ltri = jnp.asarray(np.tril(np.ones((N_EXPERTS, N_EXPERTS), np.float32), -1), BF16)
    tab = pl.BlockSpec((1, N_EXPERTS, LANES), lambda i: (i, 0, 0))
    tab_shape = jax.ShapeDtypeStruct((nt, N_EXPERTS, LANES), F32)
    return pl.pallas_call(
        _router_kernel,
        grid=(nt,),
        in_specs=[pl.BlockSpec((tm, d), lambda i: (i, 0)),
                  pl.BlockSpec((N_EXPERTS, d), lambda i: (0, 0)),
                  pl.BlockSpec((N_EXPERTS, 1), lambda i: (0, 0)),
                  pl.BlockSpec((tm, tm), lambda i: (0, 0)),
                  pl.BlockSpec((N_EXPERTS, N_EXPERTS), lambda i: (0, 0))],
        out_specs=[pl.BlockSpec((TOP_K, tm), lambda i: (0, i)),
                   pl.BlockSpec((TOP_K, tm), lambda i: (0, i)),
                   tab, tab, tab],
        out_shape=[jax.ShapeDtypeStruct((TOP_K, t), I32),
                   jax.ShapeDtypeStruct((TOP_K, t), F32),
                   tab_shape, tab_shape, tab_shape],
        scratch_shapes=[pltpu.VMEM((N_EXPERTS, LANES), F32)],
        compiler_params=_params("arbitrary"),
        name="moe_router",
    )(x, w_router.T.astype(BF16), b_router.reshape(N_EXPERTS, 1), tri, ltri)


def _segment_copies(off_ref, cnt_ref, dst_ref, make_copy, wait):
    for e in range(N_EXPERTS):
        n = pl.multiple_of(cnt_ref[0, 0, e], SEG_ALIGN)

        @pl.when(n > 0)
        def _():
            cp = make_copy(pl.multiple_of(off_ref[0, 0, e], SEG_ALIGN), pl.multiple_of(dst_ref[0, 0, e], SEG_ALIGN), n)
            if wait:
                cp.wait()
            else:
                cp.start()


def _dispatch_kernel(off_ref, cnt_ref, dst_ref, poff_ref, pcnt_ref, pdst_ref, zero_ref, tail_cnt_ref, tail_dst_ref,
                     rest_ref, x_ref, row_ref, xs_hbm, buf, sem):
    i = pl.program_id(0)
    slot = lax.rem(i, 2)
    tm = x_ref.shape[0]
    xb = x_ref[...].astype(BF16)
    rows = [row_ref[k:k + 1, :] for k in range(TOP_K)]
    for c in range(MOE_RT // MOE_CHUNK):
        rr = c * MOE_CHUNK + lax.broadcasted_iota(I32, (MOE_CHUNK, tm), 0)
        perm = jnp.where(rr == rows[0], 1.0, 0.0)
        for k in range(1, TOP_K):
            perm = perm + jnp.where(rr == rows[k], 1.0, 0.0)
        buf[slot, c * MOE_CHUNK:(c + 1) * MOE_CHUNK, :] = _dot(perm.astype(BF16), xb)

    def copy_from(s):
        def copy(src_row, dst_row, n):
            return pltpu.make_async_copy(buf.at[s, pl.ds(src_row, n)], xs_hbm.at[pl.ds(dst_row, n)], sem.at[s])
        return copy

    @pl.when(i > 0)
    def _():
        _segment_copies(poff_ref, pcnt_ref, pdst_ref, copy_from(1 - slot), wait=True)

    _segment_copies(off_ref, cnt_ref, dst_ref, copy_from(slot), wait=False)

    @pl.when(i == pl.num_programs(0) - 1)
    def _():
        _segment_copies(off_ref, cnt_ref, dst_ref, copy_from(slot), wait=True)
        buf[0, 0:MOE_BM, :] = jnp.zeros((MOE_BM, buf.shape[2]), F32)
        zeros = copy_from(0)
        for wait in (False, True):
            _segment_copies(zero_ref, tail_cnt_ref, tail_dst_ref, zeros, wait=wait)

            def rest(j, c):
                cp = zeros(0, pl.multiple_of(rest_ref[0] + j * MOE_BM, MOE_BM), MOE_BM)
                cp.wait() if wait else cp.start()
                return c
            lax.fori_loop(0, rest_ref[1], rest, 0)


def _seg_spec(n_tiles, shift=0):
    def index(i):
        return (jnp.clip(i + shift, 0, n_tiles - 1), 0, 0)
    return pl.BlockSpec((1, 1, N_EXPERTS), index, memory_space=pltpu.SMEM)


def _smem_whole(shape):
    return pl.BlockSpec(shape, lambda i: (0,) * len(shape), memory_space=pltpu.SMEM)


def moe_dispatch(x, row, seg_off, seg_cnt, seg_dst, tail_cnt, tail_dst, rest, n_rows):
    t, d = x.shape
    tm = MOE_TM
    nt = t // tm
    tab = (1, 1, N_EXPERTS)
    return pl.pallas_call(
        _dispatch_kernel,
        grid=(nt,),
        in_specs=[_seg_spec(nt), _seg_spec(nt), _seg_spec(nt),
                  _seg_spec(nt, -1), _seg_spec(nt, -1), _seg_spec(nt, -1),
                  _smem_whole(tab), _smem_whole(tab), _smem_whole(tab), _smem_whole((2,)),
                  pl.BlockSpec((tm, d), lambda i: (i, 0)),
                  pl.BlockSpec((TOP_K, tm), lambda i: (0, i))],
        out_specs=pl.BlockSpec(memory_space=pl.ANY),
        out_shape=jax.ShapeDtypeStruct((n_rows, d), F32),
        scratch_shapes=[pltpu.VMEM((2, MOE_RT, d), F32), pltpu.SemaphoreType.DMA((2,))],
        compiler_params=_params("arbitrary"),
        name="moe_dispatch",
    )(seg_off, seg_cnt, seg_dst, seg_off, seg_cnt, seg_dst, jnp.zeros(tab, I32), tail_cnt.reshape(tab),
      tail_dst.reshape(tab), rest, x, row)


def _expert_kernel(blk_e_ref, nused_ref, xs_ref, wgu_ref, bgu_ref, wdn_ref, bdn_ref, ys_ref, wgu_bf, wdn_bf):
    i = pl.program_id(0)
    used = i < nused_ref[0]
    new_expert = (i == 0) | (blk_e_ref[i] != blk_e_ref[jnp.maximum(i - 1, 0)])

    @pl.when(used & new_expert)
    def _():
        rows = 128
        for r in range(0, wgu_bf.shape[0], rows):
            wgu_bf[r:r + rows, :] = wgu_ref[0, 0, r:r + rows, :].astype(BF16)
        for r in range(0, wdn_bf.shape[0], rows):
            wdn_bf[r:r + rows, :] = wdn_ref[0, 0, r:r + rows, :].astype(BF16)

    @pl.when(used)
    def _():
        h = _dot(xs_ref[...].astype(BF16), wgu_bf[...]) + bgu_ref[0, 0]
        h_gate = jnp.minimum(h[:, :D_EXPERT], SWIGLU_LIMIT)
        h_up = jnp.clip(h[:, D_EXPERT:], -SWIGLU_LIMIT, SWIGLU_LIMIT)
        act = (h_up + 1.0) * (h_gate * _sigmoid(h_gate * SWIGLU_ALPHA))
        ys_ref[...] = _dot(act.astype(BF16), wdn_bf[...]) + bdn_ref[0, 0]

    @pl.when(jnp.logical_not(used))
    def _():
        ys_ref[...] = jnp.zeros_like(ys_ref)


def moe_experts(xs, blk_e, nused, w_gu, b_gu, w_dn, b_dn, layer):
    n_rows, d = xs.shape
    bm = MOE_BM
    nb = n_rows // bm
    wsel = lambda i, be, nu: (layer, be[i], 0, 0)
    grid_spec = pltpu.PrefetchScalarGridSpec(
        num_scalar_prefetch=2,
        grid=(nb,),
        in_specs=[pl.BlockSpec((bm, d), lambda i, be, nu: (i, 0)),
                  pl.BlockSpec((1, 1, d, 2 * D_EXPERT), wsel),
                  pl.BlockSpec((1, 1, 1, 2 * D_EXPERT), wsel),
                  pl.BlockSpec((1, 1, D_EXPERT, d), wsel),
                  pl.BlockSpec((1, 1, 1, d), wsel)],
        out_specs=pl.BlockSpec((bm, d), lambda i, be, nu: (i, 0)),
        scratch_shapes=[pltpu.VMEM((d, 2 * D_EXPERT), BF16), pltpu.VMEM((D_EXPERT, d), BF16)],
    )
    return pl.pallas_call(
        _expert_kernel,
        grid_spec=grid_spec,
        out_shape=jax.ShapeDtypeStruct((n_rows, d), F32),
        compiler_params=_params("arbitrary"),
        name="moe_experts",
    )(blk_e, nused, xs, w_gu, b_gu.reshape(DEPTH, N_EXPERTS, 1, -1), w_dn, b_dn.reshape(DEPTH, N_EXPERTS, 1, -1))


def _combine_ln_kernel(off_ref, cnt_ref, dst_ref, noff_ref, ncnt_ref, ndst_ref, ys_hbm, row_ref, gate_ref, x_ref,
                       g_ref, beta_ref, o_ref, buf, sem):
    i = pl.program_id(0)
    slot = lax.rem(i, 2)
    tm = x_ref.shape[0]

    def copy_into(s):
        def copy(buf_row, ys_row, n):
            return pltpu.make_async_copy(ys_hbm.at[pl.ds(ys_row, n)], buf.at[s, pl.ds(buf_row, n)], sem.at[s])
        return copy

    @pl.when(i == 0)
    def _():
        buf[...] = jnp.zeros_like(buf)
        _segment_copies(off_ref, cnt_ref, dst_ref, copy_into(0), wait=False)

    @pl.when(i + 1 < pl.num_programs(0))
    def _():
        _segment_copies(noff_ref, ncnt_ref, ndst_ref, copy_into(1 - slot), wait=False)

    rows = [jnp.broadcast_to(row_ref[:, k:k + 1], (tm, MOE_CHUNK)) for k in range(TOP_K)]
    gates = [jnp.broadcast_to(gate_ref[:, k:k + 1], (tm, MOE_CHUNK)) for k in range(TOP_K)]
    lane = lax.broadcasted_iota(I32, (tm, MOE_CHUNK), 1)
    _segment_copies(off_ref, cnt_ref, dst_ref, copy_into(slot), wait=True)
    ff = jnp.zeros((tm, x_ref.shape[1]), F32)
    for c in range(MOE_RT // MOE_CHUNK):
        rr = lane + c * MOE_CHUNK
        mix = jnp.where(rr == rows[0], gates[0], 0.0)
        for k in range(1, TOP_K):
            mix = mix + jnp.where(rr == rows[k], gates[k], 0.0)
        ff = ff + _dot(mix.astype(BF16), buf[slot, c * MOE_CHUNK:(c + 1) * MOE_CHUNK, :].astype(BF16))
    o_ref[...] = _layer_norm(DEEPNORM_ALPHA * x_ref[...] + ff, g_ref[...], beta_ref[...])


def moe_combine_ln(ys, row_t, gate_t, seg_off, seg_cnt, seg_dst, x, g, beta):
    t, d = x.shape
    tm = MOE_TM
    nt = t // tm
    return pl.pallas_call(
        _combine_ln_kernel,
        grid=(nt,),
        in_specs=[_seg_spec(nt), _seg_spec(nt), _seg_spec(nt),
                  _seg_spec(nt, 1), _seg_spec(nt, 1), _seg_spec(nt, 1),
                  pl.BlockSpec(memory_space=pl.ANY),
                  pl.BlockSpec((tm, TOP_K), lambda i: (i, 0)),
                  pl.BlockSpec((tm, TOP_K), lambda i: (i, 0)),
                  pl.BlockSpec((tm, d), lambda i: (i, 0)),
                  pl.BlockSpec((1, d), lambda i: (0, 0)),
                  pl.BlockSpec((1, d), lambda i: (0, 0))],
        out_specs=pl.BlockSpec((tm, d), lambda i: (i, 0)),
        out_shape=jax.ShapeDtypeStruct((t, d), F32),
        scratch_shapes=[pltpu.VMEM((2, MOE_RT, d), F32), pltpu.SemaphoreType.DMA((2,))],
        compiler_params=_params("arbitrary"),
        name="moe_combine_ln",
    )(seg_off, seg_cnt, seg_dst, seg_off, seg_cnt, seg_dst, ys, row_t, gate_t, x, g.reshape(1, d),
      beta.reshape(1, d))


def moe_ln(x, w_router, b_router, w_gu, b_gu, w_dn, b_dn, g, beta, layer):
    t, d = x.shape
    bm = MOE_BM
    nt = t // MOE_TM
    row, gate, off, cnt, base = moe_route(x, w_router, b_router)
    seg_off = off[:, :, 0].astype(I32)
    seg_cnt = cnt[:, :, 0].astype(I32)
    seg_base = base[:, :, 0].astype(I32)
    total = seg_base[-1] + seg_cnt[-1]
    padded = (total + bm - 1) // bm * bm
    pend = jnp.cumsum(padded)
    pstart = pend - padded
    n_rows = (t * TOP_K + SEG_ALIGN * N_EXPERTS * nt) // bm * bm + N_EXPERTS * bm
    nb = n_rows // bm
    blk_row = jnp.arange(nb, dtype=I32) * bm
    blk_e = jnp.minimum(jnp.sum((pend[None, :] <= blk_row[:, None]).astype(I32), axis=1), N_EXPERTS - 1)
    nused = (pend[-1] // bm).astype(I32).reshape(1)
    seg_dst = (pstart[None, :] + seg_base).reshape(nt, 1, N_EXPERTS)
    seg_off = seg_off.reshape(nt, 1, N_EXPERTS)
    seg_cnt = seg_cnt.reshape(nt, 1, N_EXPERTS)
    rest = jnp.stack([pend[-1], nb - nused[0]]).astype(I32)
    xs = moe_dispatch(x, row, seg_off, seg_cnt, seg_dst, padded - total, pstart + total, rest, n_rows)
    ys = moe_experts(xs, blk_e, nused, w_gu, b_gu, w_dn, b_dn, layer)
    return moe_combine_ln(ys, row.T, gate.T, seg_off, seg_cnt, seg_dst, x, g, beta)


def _sgu_kernel(u_ref, v_ref, lng_ref, lnb_ref, w_ref, bs_ref, o_ref):
    tm = u_ref.shape[0]
    row = lax.broadcasted_iota(I32, (SGU_CHUNK, SGU_CHUNK), 0)
    col = lax.broadcasted_iota(I32, (SGU_CHUNK, SGU_CHUNK), 1)
    causal = row >= col
    for g in range(SGU_GROUPS):
        lo = g * SGU_CH
        vg = _gelu(v_ref[:, lo:lo + SGU_CH])
        vg = _layer_norm(vg, lng_ref[g:g + 1, :], lnb_ref[g:g + 1, :]).astype(BF16)
        wg = jnp.where(causal, w_ref[g], 0.0).astype(BF16)
        bias = bs_ref[:, g:g + 1]
        for n in range(tm // SGU_CHUNK):
            r0 = n * SGU_CHUNK
            mix = _dot(wg, vg[r0:r0 + SGU_CHUNK]) + bias
            o_ref[r0:r0 + SGU_CHUNK, lo:lo + SGU_CH] = _gelu(u_ref[r0:r0 + SGU_CHUNK, lo:lo + SGU_CH]) * mix


def sgu_mixer(u, v, ln_g, ln_b, w_s, b_s):
    t, w = u.shape
    tm = PROJ_TM
    return pl.pallas_call(
        _sgu_kernel,
        grid=(t // tm,),
        in_specs=[pl.BlockSpec((tm, w), lambda i: (i, 0)),
                  pl.BlockSpec((tm, w), lambda i: (i, 0)),
                  pl.BlockSpec((SGU_GROUPS, SGU_CH), lambda i: (0, 0)),
                  pl.BlockSpec((SGU_GROUPS, SGU_CH), lambda i: (0, 0)),
                  pl.BlockSpec((SGU_GROUPS, SGU_CHUNK, SGU_CHUNK), lambda i: (0, 0, 0)),
                  pl.BlockSpec((SGU_CHUNK, SGU_GROUPS), lambda i: (0, 0))],
        out_specs=pl.BlockSpec((tm, w), lambda i: (i, 0)),
        out_shape=jax.ShapeDtypeStruct((t, w), F32),
        compiler_params=_params("parallel"),
        name="sgu_mixer",
    )(u, v, ln_g, ln_b, w_s, b_s.T)


def _conv_kernel(h_ref, bg_ref, cg_ref, w_ref, b_ref, o_ref, carry_ref):
    @pl.when(pl.program_id(1) == 0)
    def _():
        carry_ref[...] = jnp.zeros_like(carry_ref)

    z = cg_ref[...] * h_ref[...]
    tm = z.shape[0]
    row = lax.broadcasted_iota(I32, z.shape, 0)
    prev = carry_ref[...]
    z1 = jnp.where(row == 0, prev[7:8, :], pltpu.roll(z, 1, 0))
    z2 = jnp.where(row == 0, prev[6:7, :], jnp.where(row == 1, prev[7:8, :], pltpu.roll(z, 2, 0)))
    y = w_ref[0:1, :] * z2 + w_ref[1:2, :] * z1 + w_ref[2:3, :] * z + b_ref[...]
    o_ref[...] = bg_ref[...] * y
    carry_ref[...] = z[tm - 8:tm, :]


def conv_mixer(proj, col0, conv_w, conv_b, batch, seq):
    tm = PROJ_TM
    per_b = seq // tm
    c = CONV_CH

    def spec(j):
        return pl.BlockSpec((tm, c), lambda b, i: (b * per_b + i, j))

    return pl.pallas_call(
        _conv_kernel,
        grid=(batch, per_b),
        in_specs=[spec(col0), spec(col0 + 1), spec(col0 + 2),
                  pl.BlockSpec((3, c), lambda b, i: (0, 0)),
                  pl.BlockSpec((1, c), lambda b, i: (0, 0))],
        out_specs=pl.BlockSpec((tm, c), lambda b, i: (b * per_b + i, 0)),
        out_shape=jax.ShapeDtypeStruct((batch * seq, c), F32),
        scratch_shapes=[pltpu.VMEM((8, c), F32)],
        compiler_params=_params("arbitrary", "arbitrary"),
        name="conv_mixer",
    )(proj, proj, proj, conv_w, conv_b.reshape(1, c))


def _split3(x):
    hi = x.astype(BF16)
    r1 = x - hi.astype(F32)
    mid = r1.astype(BF16)
    lo = (r1 - mid.astype(F32)).astype(BF16)
    return hi, mid, lo


def _hgrn_kernel(q_ref, f_ref, i_ref, g_ref, lbl_ref, ng_ref, o_ref, state_ref, *, layer):
    c = HGRN_STEP
    dk = HGRN_DK
    n_chunks = q_ref.shape[0] // c
    lw = lbl_ref[...]
    lw = jnp.exp(lw - jnp.max(lw, axis=0, keepdims=True))
    lw = lw / jnp.sum(lw, axis=0, keepdims=True)
    lb = jnp.sum(lw[1:layer + 1], axis=0, keepdims=True)
    row = lax.broadcasted_iota(I32, (c, c), 0)
    col = lax.broadcasted_iota(I32, (c, c), 1)
    tril = jnp.where(row >= col, 1.0, 0.0).astype(BF16)
    sub = lax.broadcasted_iota(I32, (8, dk), 0)

    @pl.when(pl.program_id(1) == 0)
    def _():
        state_ref[...] = jnp.zeros_like(state_ref)

    def chunk(ci, carry):
        r0 = pl.multiple_of(ci * c, c)
        z = f_ref[pl.ds(r0, c), :]
        k_all = (1.0 - lb) * _sigmoid(-z)
        log_f = jnp.log1p(-k_all)
        hi, mid, lo = _split3(log_f)
        b_all = _dot(tril, hi) + _dot(tril, mid) + _dot(tril, lo)
        q_all = q_ref[pl.ds(r0, c), :]
        v_all = i_ref[pl.ds(r0, c), :]
        g_all = g_ref[pl.ds(r0, c), :]
        outs = []
        for h in range(HGRN_HEADS):
            cols = slice(h * dk, (h + 1) * dk)
            q, k, v, b = q_all[:, cols], k_all[:, cols], v_all[:, cols], b_all[:, cols]
            state_t = state_ref[h]
            o = _dot_nt((q * jnp.exp(b)).astype(BF16), state_t.astype(BF16))

            acc = [o[8 * j:8 * j + 8] for j in range(c // 8)]
            for s in range(c):
                bs = b[s:s + 1, :]
                ks = k[s:s + 1, :]
                vs = v[s:s + 1, :]
                j0 = s // 8
                for j in range(j0, c // 8):
                    dlt = b[8 * j:8 * j + 8] - bs
                    if j == j0:
                        dlt = jnp.where(sub >= s - 8 * j0, dlt, NEG_INF)
                    a = jnp.sum(q[8 * j:8 * j + 8] * ks * jnp.exp(dlt), axis=-1, keepdims=True)
                    acc[j] = acc[j] + a * vs
            o = jnp.concatenate(acc, axis=0)

            b_last = b[c - 1:c, :]
            kd = (k * jnp.exp(b_last - b)).astype(BF16)
            state_ref[h] = jnp.exp(b_last) * state_t + _dot_tn(v.astype(BF16), kd)
            outs.append(o * lax.rsqrt(jnp.mean(o * o, axis=-1, keepdims=True) + RMS_EPS))
        o_ref[pl.ds(r0, c), :] = jnp.concatenate(outs, axis=1) * ng_ref[...] * (g_all * _sigmoid(g_all))
        return carry

    lax.fori_loop(0, n_chunks, chunk, 0)


def hgrn2_mixer(proj, lb_logits, norm_g, layer, batch, seq):
    w = HGRN_W
    seg = min(seq, HGRN_SEG)
    per_b = seq // seg

    def spec(grp):
        return pl.BlockSpec((seg, w), lambda b, i: (b * per_b + i, grp))

    return pl.pallas_call(
        functools.partial(_hgrn_kernel, layer=layer),
        grid=(batch, per_b),
        in_specs=[spec(0), spec(1), spec(2), spec(3),
                  pl.BlockSpec((DEPTH, w), lambda b, i: (0, 0)),
                  pl.BlockSpec((1, w), lambda b, i: (0, 0))],
        out_specs=pl.BlockSpec((seg, w), lambda b, i: (b * per_b + i, 0)),
        out_shape=jax.ShapeDtypeStruct((batch * seq, w), F32),
        scratch_shapes=[pltpu.VMEM((HGRN_HEADS, HGRN_DK, HGRN_DK), F32)],
        compiler_params=_params("arbitrary", "arbitrary"),
        name="hgrn2_mixer",
    )(proj, proj, proj, proj, lb_logits, norm_g.reshape(1, w))


EVEN_T_ROWS = NSA_Q + 4 * 2 * HEAD_DIM + 32
EVEN_S_COLS = 4 * LANES + 2 * SGU_W


def _even_proj_kernel(x_ref, wt_ref, ws_ref, cos_ref, sin_ref, qt_ref, qrt_ref, ks_ref, kw_ref,
                      vs_ref, vw_ref, gt_ref, kvc_ref, u_ref, v_ref, *, per_b):
    tm = x_ref.shape[0]
    xb = x_ref[...].astype(BF16)
    st = _dot_nt(wt_ref[...], xb)
    cos = cos_ref[...]
    sin = sin_ref[...]
    half = HEAD_DIM // 2
    scale = HEAD_DIM ** -0.5 * LOG2_E

    def rope(blk):
        x1, x2 = blk[:half], blk[half:]
        return jnp.concatenate([x1 * cos - x2 * sin, x2 * cos + x1 * sin], axis=0)

    for hh in range(NSA_HEADS):
        blk = st[hh * HEAD_DIM:(hh + 1) * HEAD_DIM]
        qt_ref[hh * HEAD_DIM:(hh + 1) * HEAD_DIM, :] = (blk * scale).astype(BF16)
        qrt_ref[hh * HEAD_DIM:(hh + 1) * HEAD_DIM, :] = (rope(blk) * scale).astype(BF16)

    kk = jnp.concatenate([rope(st[NSA_Q + j * HEAD_DIM:NSA_Q + (j + 1) * HEAD_DIM]) for j in range(4)], axis=0)
    kk = kk.T
    pos = lax.rem(pl.program_id(0), per_b) * tm + lax.broadcasted_iota(I32, (tm, LANES), 0)
    lane = lax.broadcasted_iota(I32, (tm, LANES), 1)
    member = jnp.where(lane == pos // SLC_BLOCK, 1.0, 0.0)
    ks_ref[...] = jnp.concatenate([kk[:, :LANES], member], axis=1).astype(BF16)
    kw_ref[...] = kk[:, LANES:].astype(BF16)
    v0 = NSA_Q + 4 * HEAD_DIM
    for j in range(tm // NSA_KT):
        vs_ref[j] = st[v0:v0 + LANES, j * NSA_KT:(j + 1) * NSA_KT].astype(BF16)
    for j in range(tm // LANES):
        vw_ref[j] = st[v0 + LANES:v0 + 2 * LANES, j * LANES:(j + 1) * LANES].astype(BF16)
    gt_ref[...] = _sigmoid(st[v0 + 2 * LANES:v0 + 2 * LANES + 32])

    ss = _dot(xb, ws_ref[...])
    for j in range(4):
        kvc_ref[j] = ss[:, j * LANES:j * LANES + HEAD_DIM]
    u_ref[...] = ss[:, 4 * LANES:4 * LANES + SGU_W]
    v_ref[...] = ss[:, 4 * LANES + SGU_W:]


def even_proj(x, w_in, cos_t, sin_t, seq):
    t, d = x.shape
    tm = PROJ_TM
    per_b = seq // tm
    hd = HEAD_DIM
    kv0 = NSA_Q

    def kvcols(i):
        return w_in[:, kv0 + i * 2 * hd:kv0 + (i + 1) * 2 * hd]

    g0 = kv0 + 6 * 2 * hd
    n_gates = 3 * NSA_HEADS
    wt = jnp.concatenate([w_in[:, :NSA_Q], kvcols(2), kvcols(4), kvcols(3), kvcols(5),
                          w_in[:, g0:g0 + n_gates], jnp.zeros((d, 32 - n_gates), F32)], axis=1).T.astype(BF16)
    zpad = jnp.zeros((d, LANES - hd), F32)
    cmp_cols = []
    for i in (0, 1):
        for g in range(NSA_KV_HEADS):
            cmp_cols += [w_in[:, kv0 + i * 2 * hd + g * hd:kv0 + i * 2 * hd + (g + 1) * hd], zpad]
    ws = jnp.concatenate(cmp_cols + [w_in[:, g0 + n_gates:]], axis=1).astype(BF16)
    half = hd // 2
    tok = lambda i: (i, 0)
    tok_t = lambda i: (0, i)
    return pl.pallas_call(
        functools.partial(_even_proj_kernel, per_b=per_b),
        grid=(t // tm,),
        in_specs=[pl.BlockSpec((tm, d), tok),
                  pl.BlockSpec((EVEN_T_ROWS, d), lambda i: (0, 0)),
                  pl.BlockSpec((d, EVEN_S_COLS), lambda i: (0, 0)),
                  pl.BlockSpec((half, tm), tok_t),
                  pl.BlockSpec((half, tm), tok_t)],
        out_specs=[pl.BlockSpec((NSA_Q, tm), tok_t),
                   pl.BlockSpec((NSA_Q, tm), tok_t),
                   pl.BlockSpec((tm, 2 * LANES), tok),
                   pl.BlockSpec((tm, LANES), tok),
                   pl.BlockSpec((tm // NSA_KT, LANES, NSA_KT), lambda i: (i, 0, 0)),
                   pl.BlockSpec((tm // LANES, LANES, LANES), lambda i: (i, 0, 0)),
                   pl.BlockSpec((32, tm), tok_t),
                   pl.BlockSpec((4, tm, hd), lambda i: (0, i, 0)),
                   pl.BlockSpec((tm, SGU_W), tok),
                   pl.BlockSpec((tm, SGU_W), tok)],
        out_shape=[jax.ShapeDtypeStruct((NSA_Q, t), BF16),
                   jax.ShapeDtypeStruct((NSA_Q, t), BF16),
                   jax.ShapeDtypeStruct((t, 2 * LANES), BF16),
                   jax.ShapeDtypeStruct((t, LANES), BF16),
                   jax.ShapeDtypeStruct((t // NSA_KT, LANES, NSA_KT), BF16),
                   jax.ShapeDtypeStruct((t // LANES, LANES, LANES), BF16),
                   jax.ShapeDtypeStruct((32, t), F32),
                   jax.ShapeDtypeStruct((4, t, hd), F32),
                   jax.ShapeDtypeStruct((t, SGU_W), F32),
                   jax.ShapeDtypeStruct((t, SGU_W), F32)],
        compiler_params=_params("parallel"),
        name="even_proj",
    )(x, wt, ws, cos_t, sin_t)


def _compress_kernel(kvc_ref, pos_ref, w1_ref, w2_ref, kc_ref, vct_ref):
    ncp = kc_ref.shape[1]
    assert CMP_BLOCK == 2 * CMP_STRIDE
    for i in range(2):
        outs = []
        for g in range(NSA_KV_HEADS):
            first = jnp.zeros((ncp, CMP_HIDDEN), F32)
            second = jnp.zeros((ncp, CMP_HIDDEN), F32)
            for l in range(CMP_STRIDE):
                rows = kvc_ref[i * NSA_KV_HEADS + g, pl.ds(l, ncp, stride=CMP_STRIDE), :]
                first = first + _dot((rows + pos_ref[i, l:l + 1, :]).astype(BF16), w1_ref[i, l])
                l2 = l + CMP_STRIDE
                second = second + _dot((rows + pos_ref[i, l2:l2 + 1, :]).astype(BF16), w1_ref[i, l2])
            hid = _gelu(first + pltpu.roll(second, ncp - 1, 0))
            outs.append(_dot(hid.astype(BF16), w2_ref[i]))
        if i == 0:
            kc_ref[0] = jnp.concatenate(outs, axis=1).astype(BF16)
        else:
            vct_ref[0] = jnp.concatenate(outs, axis=1).T.astype(BF16)


def nsa_compress(kvc, cmp_pos, cmp_w1, cmp_w2, batch, seq):
    hd = HEAD_DIM
    ncp = seq // CMP_STRIDE
    w1 = cmp_w1.reshape(2, CMP_BLOCK, hd, CMP_HIDDEN).astype(BF16)
    return pl.pallas_call(
        _compress_kernel,
        grid=(batch,),
        in_specs=[pl.BlockSpec((4, seq, hd), lambda b: (0, b, 0)),
                  pl.BlockSpec((2, CMP_BLOCK, hd), lambda b: (0, 0, 0)),
                  pl.BlockSpec((2, CMP_BLOCK, hd, CMP_HIDDEN), lambda b: (0, 0, 0, 0)),
                  pl.BlockSpec((2, CMP_HIDDEN, hd), lambda b: (0, 0, 0))],
        out_specs=[pl.BlockSpec((1, ncp, 2 * hd), lambda b: (b, 0, 0)),
                   pl.BlockSpec((1, 2 * hd, ncp), lambda b: (b, 0, 0))],
        out_shape=[jax.ShapeDtypeStruct((batch, ncp, 2 * hd), BF16),
                   jax.ShapeDtypeStruct((batch, 2 * hd, ncp), BF16)],
        compiler_params=_params("parallel"),
        name="nsa_compress",
    )(kvc, cmp_pos, w1, cmp_w2.astype(BF16))


def _softmax2_cols(s, mask):
    sm = jnp.where(mask, s, NEG_INF)
    m = jnp.max(sm, axis=0, keepdims=True)
    e = jnp.exp2(sm - m)
    return jnp.where(mask, e / jnp.sum(e, axis=0, keepdims=True), 0.0)


def _nsa_kernel(qt_ref, qrt_ref, gt_ref, kc_ref, vct_ref, mct_ref, ks_ref, kw_ref, vs_ref, vw_ref, o_ref,
                s_scr, q_scr, m_scr, l_scr, acc_scr, *, n_top):
    tq = qt_ref.shape[1]
    hg = NSA_GROUP
    hd = HEAD_DIM
    groups = range(NSA_KV_HEADS)
    w = hg * tq
    ncp = kc_ref.shape[1]
    n_slc = mct_ref.shape[0]
    s0 = pl.program_id(1) * tq
    t_lane = s0 + lax.rem(lax.broadcasted_iota(I32, (1, w), 1), tq)
    t_q = s0 + lax.broadcasted_iota(I32, (1, tq), 1)
    zeros_g = jnp.zeros((hd, w), BF16)
    n_win = WINDOW // LANES + tq // LANES
    wt0 = jnp.maximum(s0 // LANES - WINDOW // LANES, 0)
    vrows = [slice(g * hd, (g + 1) * hd) for g in groups]

    def grp(ref, g):
        rows = jnp.concatenate([ref[(g * hg + h) * hd:(g * hg + h + 1) * hd, :] for h in range(hg)], axis=1)
        return jnp.concatenate([rows, zeros_g] if g == 0 else [zeros_g, rows], axis=0)

    q_rot = [grp(qrt_ref, g) for g in groups]
    o_cmp, q_aug = [], []
    for g in groups:
        sc = _dot(kc_ref[0], grp(qt_ref, g))
        c_end = lax.broadcasted_iota(I32, (ncp, w), 0) * CMP_STRIDE + (CMP_BLOCK - 1)
        p = _softmax2_cols(sc, c_end <= t_lane)
        o_cmp.append(_dot(vct_ref[0, vrows[g], :], p.astype(BF16)))
        psum = p[:, 0:tq]
        for h in range(1, hg):
            psum = psum + p[:, h * tq:(h + 1) * tq]
        imp = _dot(mct_ref[...], psum.astype(BF16))

        j_blk = lax.broadcasted_iota(I32, (n_slc, tq), 0)
        cur = t_q // SLC_BLOCK
        forced = (j_blk == 0) | (j_blk == cur) | (j_blk == cur - 1)
        valid = j_blk <= cur
        score = jnp.where(forced, SLC_FORCED_SCORE, jnp.where(valid, imp, -1.0))
        nv = n_slc // 8
        sblk = [score[8 * v:8 * v + 8] for v in range(nv)]
        rank = [jnp.zeros((8, tq), F32) for _ in range(nv)]
        sub = lax.broadcasted_iota(I32, (8, tq), 0)
        for k in range(n_slc):
            sk = score[k:k + 1, :]
            kv_ = k // 8
            for v in range(nv):
                ge = jnp.where(sk >= sblk[v], 1.0, 0.0)
                gt = jnp.where(sk > sblk[v], 1.0, 0.0)
                if v > kv_:
                    beats = ge
                elif v < kv_:
                    beats = gt
                else:
                    beats = jnp.where(sub > k - 8 * kv_, ge, gt)
                rank[v] = rank[v] + beats
        rank = jnp.concatenate(rank, axis=0)
        bias = jnp.where((rank < n_top) & valid, 0.0, NEG_INF)
        bias = jnp.concatenate([bias] * hg, axis=1)
        if n_slc < 2 * hd:
            bias = jnp.concatenate([bias, jnp.zeros((2 * hd - n_slc, w), F32)], axis=0)
        q_aug.append(jnp.concatenate([q_rot[g], bias.astype(BF16)], axis=0))

    for g in groups:
        q_scr[g] = q_aug[g]
        m_scr[g] = jnp.full((1, w), NEG_INF, F32)
        l_scr[g] = jnp.zeros((1, w), F32)
        acc_scr[g] = jnp.zeros((hd, w), F32)

    def slc_scores(kt, slot):
        k_tile = ks_ref[pl.ds(pl.multiple_of(kt * NSA_KT, NSA_KT), NSA_KT), :]
        for g in groups:
            s_scr[slot, g] = _dot(k_tile, q_scr[g])

    def slc_update(kt, slot, diagonal):
        for g in groups:
            s = s_scr[slot, g]
            if diagonal:
                kpos = kt * NSA_KT + lax.broadcasted_iota(I32, (NSA_KT, w), 0)
                s = jnp.where(kpos <= t_lane, s, NEG_INF)
            m = m_scr[g]
            m_new = jnp.maximum(m, jnp.max(s, axis=0, keepdims=True))
            alpha = jnp.exp2(m - m_new)
            pp = jnp.exp2(s - m_new)
            l_scr[g] = l_scr[g] * alpha + jnp.sum(pp, axis=0, keepdims=True)
            acc_scr[g] = acc_scr[g] * alpha + _dot(vs_ref[kt, vrows[g], :], pp.astype(BF16))
            m_scr[g] = m_new

    n_full = s0 // NSA_KT

    def slc_pair(p, c):
        a = 2 * p
        slc_scores(a + 1, 1)
        slc_update(a, 0, False)
        slc_scores(a + 2, 0)
        slc_update(a + 1, 1, False)
        return c

    slc_scores(0, 0)
    lax.fori_loop(0, n_full // 2, slc_pair, 0)
    odd = lax.rem(n_full, 2) == 1

    @pl.when(odd)
    def _():
        slc_scores(n_full, 1)
        slc_update(n_full - 1, 0, False)
        slc_update(n_full, 1, True)

    @pl.when(jnp.logical_not(odd))
    def _():
        slc_update(n_full, 0, True)

    o_slc = [acc_scr[g] / l_scr[g] for g in groups]

    kw0 = pl.multiple_of(wt0 * LANES, LANES)
    k_win = kw_ref[pl.ds(kw0, n_win * LANES), :]
    dist = t_lane - (kw0 + lax.broadcasted_iota(I32, (n_win * LANES, w), 0))
    in_window = (dist >= 0) & (dist < WINDOW)
    o_win = []
    for g in groups:
        sm = jnp.where(in_window, _dot(k_win, q_rot[g]), NEG_INF)
        e = jnp.exp2(sm - jnp.max(sm, axis=0, keepdims=True))
        den = jnp.sum(e, axis=0, keepdims=True)
        eb = e.astype(BF16)
        ow = _dot(vw_ref[wt0, vrows[g], :], eb[0:LANES])
        for j in range(1, n_win):
            ow = ow + _dot(vw_ref[wt0 + j, vrows[g], :], eb[j * LANES:(j + 1) * LANES])
        o_win.append(ow / den)

    heads = []
    for g in groups:
        for h in range(hg):
            r = (g * hg + h) * 3
            cols = slice(h * tq, (h + 1) * tq)
            heads.append(gt_ref[r:r + 1, :] * o_cmp[g][:, cols] + gt_ref[r + 1:r + 2, :] * o_slc[g][:, cols]
                         + gt_ref[r + 2:r + 3, :] * o_win[g][:, cols])
    o_ref[...] = jnp.concatenate(heads, axis=0).T


def _cmp_to_slc_t(seq):
    ncp = seq // CMP_STRIDE
    ns = seq // SLC_BLOCK
    cs = np.arange(ncp)[None, :] * CMP_STRIDE
    ss = np.arange(ns)[:, None] * SLC_BLOCK
    ov = np.clip(np.minimum(cs + CMP_BLOCK, ss + SLC_BLOCK) - np.maximum(cs, ss), 0, None) / CMP_BLOCK
    ov[:, ncp - 1] = 0.0
    return jnp.asarray(ov, BF16)


def nsa_mixer(qt, qrt, gt, kc, vct, ks, kw, vs, vw, batch, seq):
    tq = NSA_TQ
    nq = seq // tq
    ncp = seq // CMP_STRIDE
    n_slc = seq // SLC_BLOCK
    assert n_slc <= 2 * HEAD_DIM and n_slc % 8 == 0 and seq >= WINDOW + tq
    w = NSA_GROUP * tq
    col = lambda b, i: (0, b * nq + i)
    return pl.pallas_call(
        functools.partial(_nsa_kernel, n_top=min(SLC_TOP, n_slc)),
        grid=(batch, nq),
        in_specs=[pl.BlockSpec((NSA_Q, tq), col),
                  pl.BlockSpec((NSA_Q, tq), col),
                  pl.BlockSpec((32, tq), col),
                  pl.BlockSpec((1, ncp, 2 * HEAD_DIM), lambda b, i: (b, 0, 0)),
                  pl.BlockSpec((1, 2 * HEAD_DIM, ncp), lambda b, i: (b, 0, 0)),
                  pl.BlockSpec((n_slc, ncp), lambda b, i: (0, 0)),
                  pl.BlockSpec((seq, 2 * LANES), lambda b, i: (b, 0)),
                  pl.BlockSpec((seq, LANES), lambda b, i: (b, 0)),
                  pl.BlockSpec((seq // NSA_KT, LANES, NSA_KT), lambda b, i: (b, 0, 0)),
                  pl.BlockSpec((seq // LANES, LANES, LANES), lambda b, i: (b, 0, 0))],
        out_specs=pl.BlockSpec((tq, NSA_Q), lambda b, i: (b * nq + i, 0)),
        out_shape=jax.ShapeDtypeStruct((batch * seq, NSA_Q), F32),
        scratch_shapes=[pltpu.VMEM((2, NSA_KV_HEADS, NSA_KT, w), F32),
                        pltpu.VMEM((NSA_KV_HEADS, 4 * HEAD_DIM, w), BF16),
                        pltpu.VMEM((NSA_KV_HEADS, 1, w), F32),
                        pltpu.VMEM((NSA_KV_HEADS, 1, w), F32),
                        pltpu.VMEM((NSA_KV_HEADS, HEAD_DIM, w), F32)],
        compiler_params=_params("parallel", "parallel"),
        name="nsa_mixer",
    )(qt, qrt, gt, kc, vct, _cmp_to_slc_t(seq), ks, kw, vs, vw)


def kernel(x, mem, positions, w_in_even, nsa_cmp_pos, nsa_cmp_w1, nsa_cmp_w2, sgu_ln_g, sgu_ln_b, sgu_w, sgu_b, w_out_even, w_in_odd, hgrn_lb_logits, hgrn_norm_g, conv_w, conv_b, w_out_odd, xattn_w_q, xattn_w_kv, xattn_w_o, ln_g, ln_b, router_w, router_b, expert_w_gu, expert_b_gu, expert_w_dn, expert_b_dn):
    batch, seq, d = x.shape
    t = batch * seq
    cos_t, sin_t = rope_tables_t(positions)
    mem2 = mem.reshape(-1, d)
    xf = x.reshape(t, d)
    for layer in range(DEPTH):
        j = layer // 2
        if layer % 2 == 0:
            qt, qrt, ks, kw, vs, vw, gt, kvc, u, v = even_proj(xf, w_in_even[j], cos_t, sin_t, seq)
            kc, vct = nsa_compress(kvc, nsa_cmp_pos[j], nsa_cmp_w1[j], nsa_cmp_w2[j], batch, seq)
            o_a = nsa_mixer(qt, qrt, gt, kc, vct, ks, kw, vs, vw, batch, seq)
            o_b = sgu_mixer(u, v, sgu_ln_g[j], sgu_ln_b[j], sgu_w[j], sgu_b[j])
            w_out = w_out_even[j]
        else:
            proj = matmul(xf, w_in_odd[j].astype(BF16), F32, PROJ_TM)
            o_a = hgrn2_mixer(proj, hgrn_lb_logits, hgrn_norm_g[j], layer, batch, seq)
            o_b = conv_mixer(proj, 4 * HGRN_W // CONV_CH, conv_w[j], conv_b[j], batch, seq)
            w_out = w_out_odd[j]
        xf = outproj_ln(o_a, o_b, xf, w_out, ln_g[layer, 0], ln_b[layer, 0])
        kv = matmul(mem2, xattn_w_kv[layer].astype(BF16), BF16, mem.shape[1]).reshape(batch, mem.shape[1], 2 * d)
        xf = xattn_ln(xf, kv, xattn_w_q[layer], xattn_w_o[layer], ln_g[layer, 1], ln_b[layer, 1], seq)
        xf = moe_ln(xf, router_w[layer], router_b[layer], expert_w_gu, expert_b_gu, expert_w_dn, expert_b_dn,
                    ln_g[layer, 2], ln_b[layer, 2], layer)
    return xf.reshape(batch, seq, d)
```

```python
import functools

import numpy as np
import jax
import jax.numpy as jnp
from jax import lax
from jax.experimental import pallas as pl
from jax.experimental.pallas import tpu as pltpu

F32 = jnp.float32
BF16 = jnp.bfloat16
I32 = jnp.int32

D_MODEL = 1024
DEPTH = 2
HEAD_DIM = 64
NSA_HEADS = 8
NSA_KV_HEADS = 2
NSA_GROUP = NSA_HEADS // NSA_KV_HEADS
CMP_BLOCK = 32
CMP_STRIDE = 16
CMP_HIDDEN = 256
SLC_BLOCK = 64
SLC_TOP = 16
WINDOW = 512
SLC_FORCED_SCORE = 1e4
SGU_GROUPS = 4
SGU_CH = 128
SGU_CHUNK = 128
HGRN_HEADS = 4
HGRN_DK = 128
HGRN_CHUNK = 64
CONV_CH = 512
XATTN_HEADS = 4
XATTN_DIM = D_MODEL // XATTN_HEADS
N_EXPERTS = 32
TOP_K = 4
D_EXPERT = D_MODEL
SWIGLU_LIMIT = 7.0
SWIGLU_ALPHA = 1.702
ROPE_THETA = 10000.0
LN_EPS = 1e-5
RMS_EPS = 1e-6
NEG_INF = -1e30
DEEPNORM_ALPHA = (2 * DEPTH) ** 0.25
LOG2_E = 1.4426950408889634

NSA_Q = NSA_HEADS * HEAD_DIM
SGU_W = SGU_GROUPS * SGU_CH
HGRN_W = HGRN_HEADS * HGRN_DK

VMEM_LIMIT_BYTES = 56 * 1024 * 1024
LANES = 128

PROJ_TM = 512
NSA_TQ = 256
NSA_KT = 256
HGRN_STEP = 32
HGRN_SEG = 1024
MOE_TM = 512
MOE_BM = 512
SEG_ALIGN = 8
MOE_RT = TOP_K * MOE_TM + SEG_ALIGN * N_EXPERTS
MOE_CHUNK = 256


def _params(*sem):
    return pltpu.CompilerParams(dimension_semantics=sem, vmem_limit_bytes=VMEM_LIMIT_BYTES)


def _dot(a, b):
    return jnp.dot(a, b, preferred_element_type=F32)


def _dot_nt(a, b):
    return lax.dot_general(a, b, (((1,), (1,)), ((), ())), preferred_element_type=F32)


def _dot_tn(a, b):
    return lax.dot_general(a, b, (((0,), (0,)), ((), ())), preferred_element_type=F32)


def _gelu(x):
    return 0.5 * x * (1.0 + jnp.tanh(np.sqrt(2.0 / np.pi).astype(np.float32) * (x + 0.044715 * (x * x * x))))


def _sigmoid(x):
    return 1.0 / (1.0 + jnp.exp(-x))


def _layer_norm(y, g, b):
    mu = jnp.mean(y, axis=-1, keepdims=True)
    d = y - mu
    var = jnp.mean(d * d, axis=-1, keepdims=True)
    return d * lax.rsqrt(var + LN_EPS) * g + b


def _rope_kernel(pos_ref, inv_ref, cos_ref, sin_ref):
    ang = pos_ref[...].astype(F32) * inv_ref[...]
    cos_ref[...] = jnp.cos(ang)
    sin_ref[...] = jnp.sin(ang)


def rope_tables_t(positions):
    t = positions.size
    inv = 1.0 / (ROPE_THETA ** (jnp.arange(0, HEAD_DIM, 2, dtype=F32) / HEAD_DIM))
    tn = min(t, 4096)
    half = HEAD_DIM // 2
    return pl.pallas_call(
        _rope_kernel,
        grid=(t // tn,),
        in_specs=[pl.BlockSpec((1, tn), lambda i: (0, i)),
                  pl.BlockSpec((half, 1), lambda i: (0, 0))],
        out_specs=[pl.BlockSpec((half, tn), lambda i: (0, i))] * 2,
        out_shape=[jax.ShapeDtypeStruct((half, t), F32)] * 2,
        compiler_params=_params("parallel"),
        name="rope_tables",
    )(positions.reshape(1, t), inv.reshape(half, 1))


def _mm_kernel(x_ref, w_ref, o_ref):
    o_ref[...] = _dot(x_ref[...].astype(BF16), w_ref[...]).astype(o_ref.dtype)


def matmul(x, w, out_dtype, tm):
    m, k = x.shape
    n = w.shape[1]
    return pl.pallas_call(
        _mm_kernel,
        grid=(m // tm,),
        in_specs=[pl.BlockSpec((tm, k), lambda i: (i, 0)),
                  pl.BlockSpec((k, n), lambda i: (0, 0))],
        out_specs=pl.BlockSpec((tm, n), lambda i: (i, 0)),
        out_shape=jax.ShapeDtypeStruct((m, n), out_dtype),
        compiler_params=_params("parallel"),
        name="matmul",
    )(x, w)


def _outproj_ln_kernel(a_ref, b_ref, x_ref, wa_ref, wb_ref, g_ref, beta_ref, o_ref):
    mix = _dot(a_ref[...].astype(BF16), wa_ref[...]) + _dot(b_ref[...].astype(BF16), wb_ref[...])
    o_ref[...] = _layer_norm(DEEPNORM_ALPHA * x_ref[...] + mix, g_ref[...], beta_ref[...])


def outproj_ln(a, b, x, w_out, g, beta):
    t, d = x.shape
    na, nb = a.shape[1], b.shape[1]
    tm = PROJ_TM
    wa = w_out[:na].astype(BF16)
    wb = w_out[na:].astype(BF16)
    return pl.pallas_call(
        _outproj_ln_kernel,
        grid=(t // tm,),
        in_specs=[pl.BlockSpec((tm, na), lambda i: (i, 0)),
                  pl.BlockSpec((tm, nb), lambda i: (i, 0)),
                  pl.BlockSpec((tm, d), lambda i: (i, 0)),
                  pl.BlockSpec((na, d), lambda i: (0, 0)),
                  pl.BlockSpec((nb, d), lambda i: (0, 0)),
                  pl.BlockSpec((1, d), lambda i: (0, 0)),
                  pl.BlockSpec((1, d), lambda i: (0, 0))],
        out_specs=pl.BlockSpec((tm, d), lambda i: (i, 0)),
        out_shape=jax.ShapeDtypeStruct((t, d), F32),
        compiler_params=_params("parallel"),
        name="outproj_ln",
    )(a, b, x, wa, wb, g.reshape(1, d), beta.reshape(1, d))


def _xattn_kernel(x_ref, wq_ref, kv_ref, wo_ref, g_ref, beta_ref, o_ref):
    x = x_ref[...]
    q = _dot(x.astype(BF16), wq_ref[...])
    hw = XATTN_HEADS * XATTN_DIM
    heads = []
    for h in range(XATTN_HEADS):
        lo = h * XATTN_DIM
        qh = q[:, lo:lo + XATTN_DIM].astype(BF16)
        kh = kv_ref[0, :, lo:lo + XATTN_DIM]
        vh = kv_ref[0, :, hw + lo:hw + lo + XATTN_DIM]
        s = _dot_nt(qh, kh) * (XATTN_DIM ** -0.5)
        m = jnp.max(s, axis=-1, keepdims=True)
        e = jnp.exp(s - m)
        p = e / jnp.sum(e, axis=-1, keepdims=True)
        heads.append(_dot(p.astype(BF16), vh))
    o = jnp.concatenate(heads, axis=-1)
    xa = _dot(o.astype(BF16), wo_ref[...])
    o_ref[...] = _layer_norm(DEEPNORM_ALPHA * x + xa, g_ref[...], beta_ref[...])


def xattn_ln(x, kv, w_q, w_o, g, beta, seq):
    t, d = x.shape
    tm = PROJ_TM
    per_b = seq // tm
    mlen = kv.shape[1]
    return pl.pallas_call(
        _xattn_kernel,
        grid=(t // tm,),
        in_specs=[pl.BlockSpec((tm, d), lambda i: (i, 0)),
                  pl.BlockSpec((d, d), lambda i: (0, 0)),
                  pl.BlockSpec((1, mlen, 2 * d), lambda i: (i // per_b, 0, 0)),
                  pl.BlockSpec((d, d), lambda i: (0, 0)),
                  pl.BlockSpec((1, d), lambda i: (0, 0)),
                  pl.BlockSpec((1, d), lambda i: (0, 0))],
        out_specs=pl.BlockSpec((tm, d), lambda i: (i, 0)),
        out_shape=jax.ShapeDtypeStruct((t, d), F32),
        compiler_params=_params("parallel"),
        name="xattn_ln",
    )(x, w_q.astype(BF16), kv, w_o.astype(BF16), g.reshape(1, d), beta.reshape(1, d))


def _router_kernel(x_ref, wt_ref, b_ref, tri_ref, ltri_ref, row_ref, gate_ref, off_ref, cnt_ref, base_ref,
                   carry_ref):
    i = pl.program_id(0)

    @pl.when(i == 0)
    def _():
        carry_ref[...] = jnp.zeros_like(carry_ref)

    tm = x_ref.shape[0]
    logits = _dot_nt(wt_ref[...], x_ref[...].astype(BF16)) + b_ref[...]
    e_iota = lax.broadcasted_iota(I32, (N_EXPERTS, tm), 0)
    work = logits
    vals, hots = [], []
    for _ in range(TOP_K):
        m = jnp.max(work, axis=0, keepdims=True)
        idx = jnp.min(jnp.where(work == m, e_iota, N_EXPERTS), axis=0, keepdims=True)
        hot = e_iota == idx
        vals.append(m)
        hots.append(hot)
        work = jnp.where(hot, -jnp.inf, work)
    exps = [jnp.exp(v - vals[0]) for v in vals]
    den = exps[0] + exps[1] + exps[2] + exps[3]
    gate_ref[...] = jnp.concatenate([e / den for e in exps], axis=0)

    hot_all = jnp.zeros((N_EXPERTS, tm), F32)
    for hot in hots:
        hot_all = hot_all + jnp.where(hot, 1.0, 0.0)
    rank = _dot(hot_all.astype(BF16), tri_ref[...])
    n = jnp.sum(hot_all, axis=1, keepdims=True)
    seg = jnp.floor((n + (SEG_ALIGN - 1)) * (1.0 / SEG_ALIGN))
    seg = jnp.broadcast_to(seg, (N_EXPERTS, LANES))
    off = _dot(ltri_ref[...], seg.astype(BF16))
    where_row = off[:, 0:1] * SEG_ALIGN + rank
    row_ref[...] = jnp.concatenate(
        [jnp.sum(jnp.where(hot, where_row, 0.0), axis=0, keepdims=True) for hot in hots], axis=0).astype(I32)
    off_ref[0] = off * SEG_ALIGN
    cnt_ref[0] = seg * SEG_ALIGN
    base_ref[0] = carry_ref[...]
    carry_ref[...] = carry_ref[...] + seg * SEG_ALIGN


def moe_route(x, w_router, b_router):
    t, d = x.shape
    tm = MOE_TM
    nt = t // tm
    tri = jnp.asarray(np.triu(np.ones((tm, tm), np.float32), 1), BF16)
    ltri = jnp.asarray(np.tril(np.ones((N_EXPERTS, N_EXPERTS), np.float32), -1), BF16)
    tab = pl.BlockSpec((1, N_EXPERTS, LANES), lambda i: (i, 0, 0))
    tab_shape = jax.ShapeDtypeStruct((nt, N_EXPERTS, LANES), F32)
    return pl.pallas_call(
        _router_kernel,
        grid=(nt,),
        in_specs=[pl.BlockSpec((tm, d), lambda i: (i, 0)),
                  pl.BlockSpec((N_EXPERTS, d), lambda i: (0, 0)),
                  pl.BlockSpec((N_EXPERTS, 1), lambda i: (0, 0)),
                  pl.BlockSpec((tm, tm), lambda i: (0, 0)),
                  pl.BlockSpec((N_EXPERTS, N_EXPERTS), lambda i: (0, 0))],
        out_specs=[pl.BlockSpec((TOP_K, tm), lambda i: (0, i)),
                   pl.BlockSpec((TOP_K, tm), lambda i: (0, i)),
                   tab, tab, tab],
        out_shape=[jax.ShapeDtypeStruct((TOP_K, t), I32),
                   jax.ShapeDtypeStruct((TOP_K, t), F32),
                   tab_shape, tab_shape, tab_shape],
        scratch_shapes=[pltpu.VMEM((N_EXPERTS, LANES), F32)],
        compiler_params=_params("arbitrary"),
        name="moe_router",
    )(x, w_router.T.astype(BF16), b_router.reshape(N_EXPERTS, 1), tri, ltri)


def _segment_copies(off_ref, cnt_ref, dst_ref, make_copy, wait):
    for e in range(N_EXPERTS):
        n = pl.multiple_of(cnt_ref[0, 0, e], SEG_ALIGN)

        @pl.when(n > 0)
        def _():
            cp = make_copy(pl.multiple_of(off_ref[0, 0, e], SEG_ALIGN), pl.multiple_of(dst_ref[0, 0, e], SEG_ALIGN), n)
            if wait:
                cp.wait()
            else:
                cp.start()


def _dispatch_kernel(off_ref, cnt_ref, dst_ref, poff_ref, pcnt_ref, pdst_ref, zero_ref, tail_cnt_ref, tail_dst_ref,
                     rest_ref, x_ref, row_ref, xs_hbm, buf, sem):
    i = pl.program_id(0)
    slot = lax.rem(i, 2)
    tm = x_ref.shape[0]
    xb = x_ref[...].astype(BF16)
    rows = [row_ref[k:k + 1, :] for k in range(TOP_K)]
    for c in range(MOE_RT // MOE_CHUNK):
        rr = c * MOE_CHUNK + lax.broadcasted_iota(I32, (MOE_CHUNK, tm), 0)
        perm = jnp.where(rr == rows[0], 1.0, 0.0)
        for k in range(1, TOP_K):
            perm = perm + jnp.where(rr == rows[k], 1.0, 0.0)
        buf[slot, c * MOE_CHUNK:(c + 1) * MOE_CHUNK, :] = _dot(perm.astype(BF16), xb)

    def copy_from(s):
        def copy(src_row, dst_row, n):
            return pltpu.make_async_copy(buf.at[s, pl.ds(src_row, n)], xs_hbm.at[pl.ds(dst_row, n)], sem.at[s])
        return copy

    @pl.when(i > 0)
    def _():
        _segment_copies(poff_ref, pcnt_ref, pdst_ref, copy_from(1 - slot), wait=True)

    _segment_copies(off_ref, cnt_ref, dst_ref, copy_from(slot), wait=False)

    @pl.when(i == pl.num_programs(0) - 1)
    def _():
        _segment_copies(off_ref, cnt_ref, dst_ref, copy_from(slot), wait=True)
        buf[0, 0:MOE_BM, :] = jnp.zeros((MOE_BM, buf.shape[2]), F32)
        zeros = copy_from(0)
        for wait in (False, True):
            _segment_copies(zero_ref, tail_cnt_ref, tail_dst_ref, zeros, wait=wait)

            def rest(j, c):
                cp = zeros(0, pl.multiple_of(rest_ref[0] + j * MOE_BM, MOE_BM), MOE_BM)
                cp.wait() if wait else cp.start()
                return c
            lax.fori_loop(0, rest_ref[1], rest, 0)


def _seg_spec(n_tiles, shift=0):
    def index(i):
        return (jnp.clip(i + shift, 0, n_tiles - 1), 0, 0)
    return pl.BlockSpec((1, 1, N_EXPERTS), index, memory_space=pltpu.SMEM)


def _smem_whole(shape):
    return pl.BlockSpec(shape, lambda i: (0,) * len(shape), memory_space=pltpu.SMEM)


def moe_dispatch(x, row, seg_off, seg_cnt, seg_dst, tail_cnt, tail_dst, rest, n_rows):
    t, d = x.shape
    tm = MOE_TM
    nt = t // tm
    tab = (1, 1, N_EXPERTS)
    return pl.pallas_call(
        _dispatch_kernel,
        grid=(nt,),
        in_specs=[_seg_spec(nt), _seg_spec(nt), _seg_spec(nt),
                  _seg_spec(nt, -1), _seg_spec(nt, -1), _seg_spec(nt, -1),
                  _smem_whole(tab), _smem_whole(tab), _smem_whole(tab), _smem_whole((2,)),
                  pl.BlockSpec((tm, d), lambda i: (i, 0)),
                  pl.BlockSpec((TOP_K, tm), lambda i: (0, i))],
        out_specs=pl.BlockSpec(memory_space=pl.ANY),
        out_shape=jax.ShapeDtypeStruct((n_rows, d), F32),
        scratch_shapes=[pltpu.VMEM((2, MOE_RT, d), F32), pltpu.SemaphoreType.DMA((2,))],
        compiler_params=_params("arbitrary"),
        name="moe_dispatch",
    )(seg_off, seg_cnt, seg_dst, seg_off, seg_cnt, seg_dst, jnp.zeros(tab, I32), tail_cnt.reshape(tab),
      tail_dst.reshape(tab), rest, x, row)


def _expert_kernel(blk_e_ref, nused_ref, xs_ref, wgu_ref, bgu_ref, wdn_ref, bdn_ref, ys_ref, wgu_bf, wdn_bf):
    i = pl.program_id(0)
    used = i < nused_ref[0]
    new_expert = (i == 0) | (blk_e_ref[i] != blk_e_ref[jnp.maximum(i - 1, 0)])

    @pl.when(used & new_expert)
    def _():
        rows = 128
        for r in range(0, wgu_bf.shape[0], rows):
            wgu_bf[r:r + rows, :] = wgu_ref[0, 0, r:r + rows, :].astype(BF16)
        for r in range(0, wdn_bf.shape[0], rows):
            wdn_bf[r:r + rows, :] = wdn_ref[0, 0, r:r + rows, :].astype(BF16)

    @pl.when(used)
    def _():
        h = _dot(xs_ref[...].astype(BF16), wgu_bf[...]) + bgu_ref[0, 0]
        h_gate = jnp.minimum(h[:, :D_EXPERT], SWIGLU_LIMIT)
        h_up = jnp.clip(h[:, D_EXPERT:], -SWIGLU_LIMIT, SWIGLU_LIMIT)
        act = (h_up + 1.0) * (h_gate * _sigmoid(h_gate * SWIGLU_ALPHA))
        ys_ref[...] = _dot(act.astype(BF16), wdn_bf[...]) + bdn_ref[0, 0]

    @pl.when(jnp.logical_not(used))
    def _():
        ys_ref[...] = jnp.zeros_like(ys_ref)


def moe_experts(xs, blk_e, nused, w_gu, b_gu, w_dn, b_dn, layer):
    n_rows, d = xs.shape
    bm = MOE_BM
    nb = n_rows // bm
    wsel = lambda i, be, nu: (layer, be[i], 0, 0)
    grid_spec = pltpu.PrefetchScalarGridSpec(
        num_scalar_prefetch=2,
        grid=(nb,),
        in_specs=[pl.BlockSpec((bm, d), lambda i, be, nu: (jnp.minimum(i, nu[0] - 1), 0)),
                  pl.BlockSpec((1, 1, d, 2 * D_EXPERT), wsel),
                  pl.BlockSpec((1, 1, 1, 2 * D_EXPERT), wsel),
                  pl.BlockSpec((1, 1, D_EXPERT, d), wsel),
                  pl.BlockSpec((1, 1, 1, d), wsel)],
        out_specs=pl.BlockSpec((bm, d), lambda i, be, nu: (i, 0)),
        scratch_shapes=[pltpu.VMEM((d, 2 * D_EXPERT), BF16), pltpu.VMEM((D_EXPERT, d), BF16)],
    )
    return pl.pallas_call(
        _expert_kernel,
        grid_spec=grid_spec,
        out_shape=jax.ShapeDtypeStruct((n_rows, d), F32),
        compiler_params=_params("arbitrary"),
        name="moe_experts",
    )(blk_e, nused, xs, w_gu, b_gu.reshape(DEPTH, N_EXPERTS, 1, -1), w_dn, b_dn.reshape(DEPTH, N_EXPERTS, 1, -1))


def _combine_ln_kernel(off_ref, cnt_ref, dst_ref, noff_ref, ncnt_ref, ndst_ref, ys_hbm, row_ref, gate_ref, x_ref,
                       g_ref, beta_ref, o_ref, buf, sem):
    i = pl.program_id(0)
    slot = lax.rem(i, 2)
    tm = x_ref.shape[0]

    def copy_into(s):
        def copy(buf_row, ys_row, n):
            return pltpu.make_async_copy(ys_hbm.at[pl.ds(ys_row, n)], buf.at[s, pl.ds(buf_row, n)], sem.at[s])
        return copy

    @pl.when(i == 0)
    def _():
        buf[...] = jnp.zeros_like(buf)
        _segment_copies(off_ref, cnt_ref, dst_ref, copy_into(0), wait=False)

    @pl.when(i + 1 < pl.num_programs(0))
    def _():
        _segment_copies(noff_ref, ncnt_ref, ndst_ref, copy_into(1 - slot), wait=False)

    rows = [jnp.broadcast_to(row_ref[:, k:k + 1], (tm, MOE_CHUNK)) for k in range(TOP_K)]
    gates = [jnp.broadcast_to(gate_ref[:, k:k + 1], (tm, MOE_CHUNK)) for k in range(TOP_K)]
    lane = lax.broadcasted_iota(I32, (tm, MOE_CHUNK), 1)
    _segment_copies(off_ref, cnt_ref, dst_ref, copy_into(slot), wait=True)
    ff = jnp.zeros((tm, x_ref.shape[1]), F32)
    for c in range(MOE_RT // MOE_CHUNK):
        rr = lane + c * MOE_CHUNK
        mix = jnp.where(rr == rows[0], gates[0], 0.0)
        for k in range(1, TOP_K):
            mix = mix + jnp.where(rr == rows[k], gates[k], 0.0)
        ff = ff + _dot(mix.astype(BF16), buf[slot, c * MOE_CHUNK:(c + 1) * MOE_CHUNK, :].astype(BF16))
    o_ref[...] = _layer_norm(DEEPNORM_ALPHA * x_ref[...] + ff, g_ref[...], beta_ref[...])


def moe_combine_ln(ys, row_t, gate_t, seg_off, seg_cnt, seg_dst, x, g, beta):
    t, d = x.shape
    tm = MOE_TM
    nt = t // tm
    return pl.pallas_call(
        _combine_ln_kernel,
        grid=(nt,),
        in_specs=[_seg_spec(nt), _seg_spec(nt), _seg_spec(nt),
                  _seg_spec(nt, 1), _seg_spec(nt, 1), _seg_spec(nt, 1),
                  pl.BlockSpec(memory_space=pl.ANY),
                  pl.BlockSpec((tm, TOP_K), lambda i: (i, 0)),
                  pl.BlockSpec((tm, TOP_K), lambda i: (i, 0)),
                  pl.BlockSpec((tm, d), lambda i: (i, 0)),
                  pl.BlockSpec((1, d), lambda i: (0, 0)),
                  pl.BlockSpec((1, d), lambda i: (0, 0))],
        out_specs=pl.BlockSpec((tm, d), lambda i: (i, 0)),
        out_shape=jax.ShapeDtypeStruct((t, d), F32),
        scratch_shapes=[pltpu.VMEM((2, MOE_RT, d), F32), pltpu.SemaphoreType.DMA((2,))],
        compiler_params=_params("arbitrary"),
        name="moe_combine_ln",
    )(seg_off, seg_cnt, seg_dst, seg_off, seg_cnt, seg_dst, ys, row_t, gate_t, x, g.reshape(1, d),
      beta.reshape(1, d))


def moe_ln(x, w_router, b_router, w_gu, b_gu, w_dn, b_dn, g, beta, layer):
    t, d = x.shape
    bm = MOE_BM
    nt = t // MOE_TM
    row, gate, off, cnt, base = moe_route(x, w_router, b_router)
    seg_off = off[:, :, 0].astype(I32)
    seg_cnt = cnt[:, :, 0].astype(I32)
    seg_base = base[:, :, 0].astype(I32)
    total = seg_base[-1] + seg_cnt[-1]
    padded = (total + bm - 1) // bm * bm
    pend = jnp.cumsum(padded)
    pstart = pend - padded
    n_rows = (t * TOP_K + SEG_ALIGN * N_EXPERTS * nt) // bm * bm + N_EXPERTS * bm
    nb = n_rows // bm
    blk_row = jnp.arange(nb, dtype=I32) * bm
    blk_e = jnp.minimum(jnp.sum((pend[None, :] <= blk_row[:, None]).astype(I32), axis=1), N_EXPERTS - 1)
    nused = (pend[-1] // bm).astype(I32).reshape(1)
    seg_dst = (pstart[None, :] + seg_base).reshape(nt, 1, N_EXPERTS)
    seg_off = seg_off.reshape(nt, 1, N_EXPERTS)
    seg_cnt = seg_cnt.reshape(nt, 1, N_EXPERTS)
    rest = jnp.stack([pend[-1], nb - nused[0]]).astype(I32)
    xs = moe_dispatch(x, row, seg_off, seg_cnt, seg_dst, padded - total, pstart + total, rest, n_rows)
    ys = moe_experts(xs, blk_e, nused, w_gu, b_gu, w_dn, b_dn, layer)
    return moe_combine_ln(ys, row.T, gate.T, seg_off, seg_cnt, seg_dst, x, g, beta)


def _sgu_kernel(u_ref, v_ref, lng_ref, lnb_ref, w_ref, bs_ref, o_ref):
    tm = u_ref.shape[0]
    row = lax.broadcasted_iota(I32, (SGU_CHUNK, SGU_CHUNK), 0)
    col = lax.broadcasted_iota(I32, (SGU_CHUNK, SGU_CHUNK), 1)
    causal = row >= col
    for g in range(SGU_GROUPS):
        lo = g * SGU_CH
        vg = _gelu(v_ref[:, lo:lo + SGU_CH])
        vg = _layer_norm(vg, lng_ref[g:g + 1, :], lnb_ref[g:g + 1, :]).astype(BF16)
        wg = jnp.where(causal, w_ref[g], 0.0).astype(BF16)
        bias = bs_ref[:, g:g + 1]
        for n in range(tm // SGU_CHUNK):
            r0 = n * SGU_CHUNK
            mix = _dot(wg, vg[r0:r0 + SGU_CHUNK]) + bias
            o_ref[r0:r0 + SGU_CHUNK, lo:lo + SGU_CH] = _gelu(u_ref[r0:r0 + SGU_CHUNK, lo:lo + SGU_CH]) * mix


def sgu_mixer(u, v, ln_g, ln_b, w_s, b_s):
    t, w = u.shape
    tm = PROJ_TM
    return pl.pallas_call(
        _sgu_kernel,
        grid=(t // tm,),
        in_specs=[pl.BlockSpec((tm, w), lambda i: (i, 0)),
                  pl.BlockSpec((tm, w), lambda i: (i, 0)),
                  pl.BlockSpec((SGU_GROUPS, SGU_CH), lambda i: (0, 0)),
                  pl.BlockSpec((SGU_GROUPS, SGU_CH), lambda i: (0, 0)),
                  pl.BlockSpec((SGU_GROUPS, SGU_CHUNK, SGU_CHUNK), lambda i: (0, 0, 0)),
                  pl.BlockSpec((SGU_CHUNK, SGU_GROUPS), lambda i: (0, 0))],
        out_specs=pl.BlockSpec((tm, w), lambda i: (i, 0)),
        out_shape=jax.ShapeDtypeStruct((t, w), F32),
        compiler_params=_params("parallel"),
        name="sgu_mixer",
    )(u, v, ln_g, ln_b, w_s, b_s.T)


def _conv_kernel(h_ref, bg_ref, cg_ref, w_ref, b_ref, o_ref, carry_ref):
    @pl.when(pl.program_id(1) == 0)
    def _():
        carry_ref[...] = jnp.zeros_like(carry_ref)

    z = cg_ref[...] * h_ref[...]
    tm = z.shape[0]
    row = lax.broadcasted_iota(I32, z.shape, 0)
    prev = carry_ref[...]
    z1 = jnp.where(row == 0, prev[7:8, :], pltpu.roll(z, 1, 0))
    z2 = jnp.where(row == 0, prev[6:7, :], jnp.where(row == 1, prev[7:8, :], pltpu.roll(z, 2, 0)))
    y = w_ref[0:1, :] * z2 + w_ref[1:2, :] * z1 + w_ref[2:3, :] * z + b_ref[...]
    o_ref[...] = bg_ref[...] * y
    carry_ref[...] = z[tm - 8:tm, :]


def conv_mixer(proj, col0, conv_w, conv_b, batch, seq):
    tm = PROJ_TM
    per_b = seq // tm
    c = CONV_CH

    def spec(j):
        return pl.BlockSpec((tm, c), lambda b, i: (b * per_b + i, j))

    return pl.pallas_call(
        _conv_kernel,
        grid=(batch, per_b),
        in_specs=[spec(col0), spec(col0 + 1), spec(col0 + 2),
                  pl.BlockSpec((3, c), lambda b, i: (0, 0)),
                  pl.BlockSpec((1, c), lambda b, i: (0, 0))],
        out_specs=pl.BlockSpec((tm, c), lambda b, i: (b * per_b + i, 0)),
        out_shape=jax.ShapeDtypeStruct((batch * seq, c), F32),
        scratch_shapes=[pltpu.VMEM((8, c), F32)],
        compiler_params=_params("arbitrary", "arbitrary"),
        name="conv_mixer",
    )(proj, proj, proj, conv_w, conv_b.reshape(1, c))


def _split3(x):
    hi = x.astype(BF16)
    r1 = x - hi.astype(F32)
    mid = r1.astype(BF16)
    lo = (r1 - mid.astype(F32)).astype(BF16)
    return hi, mid, lo


def _hgrn_kernel(q_ref, f_ref, i_ref, g_ref, lbl_ref, ng_ref, o_ref, state_ref, *, layer):
    c = HGRN_STEP
    dk = HGRN_DK
    n_chunks = q_ref.shape[0] // c
    lw = lbl_ref[...]
    lw = jnp.exp(lw - jnp.max(lw, axis=0, keepdims=True))
    lw = lw / jnp.sum(lw, axis=0, keepdims=True)
    lb = jnp.sum(lw[1:layer + 1], axis=0, keepdims=True)
    row = lax.broadcasted_iota(I32, (c, c), 0)
    col = lax.broadcasted_iota(I32, (c, c), 1)
    tril = jnp.where(row >= col, 1.0, 0.0).astype(BF16)
    sub = lax.broadcasted_iota(I32, (8, dk), 0)

    @pl.when(pl.program_id(1) == 0)
    def _():
        state_ref[...] = jnp.zeros_like(state_ref)

    def decay(ci):
        z = f_ref[pl.ds(pl.multiple_of(ci * c, c), c), :]
        k_all = (1.0 - lb) * _sigmoid(-z)
        log_f = jnp.log1p(-k_all)
        hi, mid, lo = _split3(log_f)
        return k_all, _dot(tril, hi) + _dot(tril, mid) + _dot(tril, lo)

    def chunk(ci, carry):
        k_all, b_all = carry
        ahead = decay(jnp.minimum(ci + 1, n_chunks - 1))
        r0 = pl.multiple_of(ci * c, c)
        q_all = q_ref[pl.ds(r0, c), :]
        v_all = i_ref[pl.ds(r0, c), :]
        g_all = g_ref[pl.ds(r0, c), :]
        outs = []
        for h in range(HGRN_HEADS):
            cols = slice(h * dk, (h + 1) * dk)
            q, k, v, b = q_all[:, cols], k_all[:, cols], v_all[:, cols], b_all[:, cols]
            state_t = state_ref[h]
            o = _dot_nt((q * jnp.exp(b)).astype(BF16), state_t.astype(BF16))

            acc = [o[8 * j:8 * j + 8] for j in range(c // 8)]
            for s in range(c):
                bs = b[s:s + 1, :]
                ks = k[s:s + 1, :]
                vs = v[s:s + 1, :]
                j0 = s // 8
                for j in range(j0, c // 8):
                    dlt = b[8 * j:8 * j + 8] - bs
                    if j == j0:
                        dlt = jnp.where(sub >= s - 8 * j0, dlt, NEG_INF)
                    a = jnp.sum(q[8 * j:8 * j + 8] * ks * jnp.exp(dlt), axis=-1, keepdims=True)
                    acc[j] = acc[j] + a * vs
            o = jnp.concatenate(acc, axis=0)

            b_last = b[c - 1:c, :]
            kd = (k * jnp.exp(b_last - b)).astype(BF16)
            state_ref[h] = jnp.exp(b_last) * state_t + _dot_tn(v.astype(BF16), kd)
            outs.append(o * lax.rsqrt(jnp.mean(o * o, axis=-1, keepdims=True) + RMS_EPS))
        o_ref[pl.ds(r0, c), :] = jnp.concatenate(outs, axis=1) * ng_ref[...] * (g_all * _sigmoid(g_all))
        return ahead

    lax.fori_loop(0, n_chunks, chunk, decay(0))


def hgrn2_mixer(proj, lb_logits, norm_g, layer, batch, seq):
    w = HGRN_W
    seg = min(seq, HGRN_SEG)
    per_b = seq // seg

    def spec(grp):
        return pl.BlockSpec((seg, w), lambda b, i: (b * per_b + i, grp))

    return pl.pallas_call(
        functools.partial(_hgrn_kernel, layer=layer),
        grid=(batch, per_b),
        in_specs=[spec(0), spec(1), spec(2), spec(3),
                  pl.BlockSpec((DEPTH, w), lambda b, i: (0, 0)),
                  pl.BlockSpec((1, w), lambda b, i: (0, 0))],
        out_specs=pl.BlockSpec((seg, w), lambda b, i: (b * per_b + i, 0)),
        out_shape=jax.ShapeDtypeStruct((batch * seq, w), F32),
        scratch_shapes=[pltpu.VMEM((HGRN_HEADS, HGRN_DK, HGRN_DK), F32)],
        compiler_params=_params("arbitrary", "arbitrary"),
        name="hgrn2_mixer",
    )(proj, proj, proj, proj, lb_logits, norm_g.reshape(1, w))


EVEN_T_ROWS = NSA_Q + 4 * 2 * HEAD_DIM + 32
EVEN_S_COLS = 4 * LANES + 2 * SGU_W


def _even_proj_kernel(x_ref, wt_ref, ws_ref, cos_ref, sin_ref, qt_ref, qrt_ref, ks_ref, kw_ref,
                      vs_ref, vw_ref, gt_ref, kvc_ref, u_ref, v_ref, *, per_b):
    tm = x_ref.shape[0]
    xb = x_ref[...].astype(BF16)
    st = _dot_nt(wt_ref[...], xb)
    cos = cos_ref[...]
    sin = sin_ref[...]
    half = HEAD_DIM // 2
    scale = HEAD_DIM ** -0.5 * LOG2_E

    def rope(blk):
        x1, x2 = blk[:half], blk[half:]
        return jnp.concatenate([x1 * cos - x2 * sin, x2 * cos + x1 * sin], axis=0)

    for hh in range(NSA_HEADS):
        blk = st[hh * HEAD_DIM:(hh + 1) * HEAD_DIM]
        qt_ref[hh * HEAD_DIM:(hh + 1) * HEAD_DIM, :] = (blk * scale).astype(BF16)
        qrt_ref[hh * HEAD_DIM:(hh + 1) * HEAD_DIM, :] = (rope(blk) * scale).astype(BF16)

    kk = jnp.concatenate([rope(st[NSA_Q + j * HEAD_DIM:NSA_Q + (j + 1) * HEAD_DIM]) for j in range(4)], axis=0)
    kk = kk.T
    pos = lax.rem(pl.program_id(0), per_b) * tm + lax.broadcasted_iota(I32, (tm, LANES), 0)
    lane = lax.broadcasted_iota(I32, (tm, LANES), 1)
    member = jnp.where(lane == pos // SLC_BLOCK, 1.0, 0.0)
    ks_ref[...] = jnp.concatenate([kk[:, :LANES], member], axis=1).astype(BF16)
    kw_ref[...] = kk[:, LANES:].astype(BF16)
    v0 = NSA_Q + 4 * HEAD_DIM
    for j in range(tm // NSA_KT):
        vs_ref[j] = st[v0:v0 + LANES, j * NSA_KT:(j + 1) * NSA_KT].astype(BF16)
    for j in range(tm // LANES):
        vw_ref[j] = st[v0 + LANES:v0 + 2 * LANES, j * LANES:(j + 1) * LANES].astype(BF16)
    gt_ref[...] = _sigmoid(st[v0 + 2 * LANES:v0 + 2 * LANES + 32])

    ss = _dot(xb, ws_ref[...])
    for j in range(4):
        kvc_ref[j] = ss[:, j * LANES:j * LANES + HEAD_DIM]
    u_ref[...] = ss[:, 4 * LANES:4 * LANES + SGU_W]
    v_ref[...] = ss[:, 4 * LANES + SGU_W:]


def even_proj(x, w_in, cos_t, sin_t, seq):
    t, d = x.shape
    tm = PROJ_TM
    per_b = seq // tm
    hd = HEAD_DIM
    kv0 = NSA_Q

    def kvcols(i):
        return w_in[:, kv0 + i * 2 * hd:kv0 + (i + 1) * 2 * hd]

    g0 = kv0 + 6 * 2 * hd
    n_gates = 3 * NSA_HEADS
    wt = jnp.concatenate([w_in[:, :NSA_Q], kvcols(2), kvcols(4), kvcols(3), kvcols(5),
                          w_in[:, g0:g0 + n_gates], jnp.zeros((d, 32 - n_gates), F32)], axis=1).T.astype(BF16)
    zpad = jnp.zeros((d, LANES - hd), F32)
    cmp_cols = []
    for i in (0, 1):
        for g in range(NSA_KV_HEADS):
            cmp_cols += [w_in[:, kv0 + i * 2 * hd + g * hd:kv0 + i * 2 * hd + (g + 1) * hd], zpad]
    ws = jnp.concatenate(cmp_cols + [w_in[:, g0 + n_gates:]], axis=1).astype(BF16)
    half = hd // 2
    tok = lambda i: (i, 0)
    tok_t = lambda i: (0, i)
    return pl.pallas_call(
        functools.partial(_even_proj_kernel, per_b=per_b),
        grid=(t // tm,),
        in_specs=[pl.BlockSpec((tm, d), tok),
                  pl.BlockSpec((EVEN_T_ROWS, d), lambda i: (0, 0)),
                  pl.BlockSpec((d, EVEN_S_COLS), lambda i: (0, 0)),
                  pl.BlockSpec((half, tm), tok_t),
                  pl.BlockSpec((half, tm), tok_t)],
        out_specs=[pl.BlockSpec((NSA_Q, tm), tok_t),
                   pl.BlockSpec((NSA_Q, tm), tok_t),
                   pl.BlockSpec((tm, 2 * LANES), tok),
                   pl.BlockSpec((tm, LANES), tok),
                   pl.BlockSpec((tm // NSA_KT, LANES, NSA_KT), lambda i: (i, 0, 0)),
                   pl.BlockSpec((tm // LANES, LANES, LANES), lambda i: (i, 0, 0)),
                   pl.BlockSpec((32, tm), tok_t),
                   pl.BlockSpec((4, tm, hd), lambda i: (0, i, 0)),
                   pl.BlockSpec((tm, SGU_W), tok),
                   pl.BlockSpec((tm, SGU_W), tok)],
        out_shape=[jax.ShapeDtypeStruct((NSA_Q, t), BF16),
                   jax.ShapeDtypeStruct((NSA_Q, t), BF16),
                   jax.ShapeDtypeStruct((t, 2 * LANES), BF16),
                   jax.ShapeDtypeStruct((t, LANES), BF16),
                   jax.ShapeDtypeStruct((t // NSA_KT, LANES, NSA_KT), BF16),
                   jax.ShapeDtypeStruct((t // LANES, LANES, LANES), BF16),
                   jax.ShapeDtypeStruct((32, t), F32),
                   jax.ShapeDtypeStruct((4, t, hd), F32),
                   jax.ShapeDtypeStruct((t, SGU_W), F32),
                   jax.ShapeDtypeStruct((t, SGU_W), F32)],
        compiler_params=_params("parallel"),
        name="even_proj",
    )(x, wt, ws, cos_t, sin_t)


def _compress_kernel(kvc_ref, pos_ref, w1_ref, w2_ref, kc_ref, vct_ref):
    ncp = kc_ref.shape[1]
    assert CMP_BLOCK == 2 * CMP_STRIDE
    for i in range(2):
        outs = []
        for g in range(NSA_KV_HEADS):
            first = jnp.zeros((ncp, CMP_HIDDEN), F32)
            second = jnp.zeros((ncp, CMP_HIDDEN), F32)
            for l in range(CMP_STRIDE):
                rows = kvc_ref[i * NSA_KV_HEADS + g, pl.ds(l, ncp, stride=CMP_STRIDE), :]
                first = first + _dot((rows + pos_ref[i, l:l + 1, :]).astype(BF16), w1_ref[i, l])
                l2 = l + CMP_STRIDE
                second = second + _dot((rows + pos_ref[i, l2:l2 + 1, :]).astype(BF16), w1_ref[i, l2])
            hid = _gelu(first + pltpu.roll(second, ncp - 1, 0))
            outs.append(_dot(hid.astype(BF16), w2_ref[i]))
        if i == 0:
            kc_ref[0] = jnp.concatenate(outs, axis=1).astype(BF16)
        else:
            vct_ref[0] = jnp.concatenate(outs, axis=1).T.astype(BF16)


def nsa_compress(kvc, cmp_pos, cmp_w1, cmp_w2, batch, seq):
    hd = HEAD_DIM
    ncp = seq // CMP_STRIDE
    w1 = cmp_w1.reshape(2, CMP_BLOCK, hd, CMP_HIDDEN).astype(BF16)
    return pl.pallas_call(
        _compress_kernel,
        grid=(batch,),
        in_specs=[pl.BlockSpec((4, seq, hd), lambda b: (0, b, 0)),
                  pl.BlockSpec((2, CMP_BLOCK, hd), lambda b: (0, 0, 0)),
                  pl.BlockSpec((2, CMP_BLOCK, hd, CMP_HIDDEN), lambda b: (0, 0, 0, 0)),
                  pl.BlockSpec((2, CMP_HIDDEN, hd), lambda b: (0, 0, 0))],
        out_specs=[pl.BlockSpec((1, ncp, 2 * hd), lambda b: (b, 0, 0)),
                   pl.BlockSpec((1, 2 * hd, ncp), lambda b: (b, 0, 0))],
        out_shape=[jax.ShapeDtypeStruct((batch, ncp, 2 * hd), BF16),
                   jax.ShapeDtypeStruct((batch, 2 * hd, ncp), BF16)],
        compiler_params=_params("parallel"),
        name="nsa_compress",
    )(kvc, cmp_pos, w1, cmp_w2.astype(BF16))


def _softmax2_cols(s, mask):
    sm = jnp.where(mask, s, NEG_INF)
    m = jnp.max(sm, axis=0, keepdims=True)
    e = jnp.exp2(sm - m)
    return jnp.where(mask, e / jnp.sum(e, axis=0, keepdims=True), 0.0)


def _nsa_kernel(qt_ref, qrt_ref, gt_ref, kc_ref, vct_ref, mct_ref, ks_ref, kw_ref, vs_ref, vw_ref, o_ref,
                s_scr, q_scr, m_scr, l_scr, acc_scr, *, n_top):
    tq = qt_ref.shape[1]
    hg = NSA_GROUP
    hd = HEAD_DIM
    groups = range(NSA_KV_HEADS)
    w = hg * tq
    ncp = kc_ref.shape[1]
    n_slc = mct_ref.shape[0]
    s0 = pl.program_id(1) * tq
    t_lane = s0 + lax.rem(lax.broadcasted_iota(I32, (1, w), 1), tq)
    t_q = s0 + lax.broadcasted_iota(I32, (1, tq), 1)
    zeros_g = jnp.zeros((hd, w), BF16)
    n_win = WINDOW // LANES + tq // LANES
    wt0 = jnp.maximum(s0 // LANES - WINDOW // LANES, 0)
    vrows = [slice(g * hd, (g + 1) * hd) for g in groups]

    def grp(ref, g):
        rows = jnp.concatenate([ref[(g * hg + h) * hd:(g * hg + h + 1) * hd, :] for h in range(hg)], axis=1)
        return jnp.concatenate([rows, zeros_g] if g == 0 else [zeros_g, rows], axis=0)

    q_rot = [grp(qrt_ref, g) for g in groups]
    o_cmp, q_aug = [], []
    for g in groups:
        sc = _dot(kc_ref[0], grp(qt_ref, g))
        c_end = lax.broadcasted_iota(I32, (ncp, w), 0) * CMP_STRIDE + (CMP_BLOCK - 1)
        p = _softmax2_cols(sc, c_end <= t_lane)
        o_cmp.append(_dot(vct_ref[0, vrows[g], :], p.astype(BF16)))
        psum = p[:, 0:tq]
        for h in range(1, hg):
            psum = psum + p[:, h * tq:(h + 1) * tq]
        imp = _dot(mct_ref[...], psum.astype(BF16))

        j_blk = lax.broadcasted_iota(I32, (n_slc, tq), 0)
        cur = t_q // SLC_BLOCK
        forced = (j_blk == 0) | (j_blk == cur) | (j_blk == cur - 1)
        valid = j_blk <= cur
        score = jnp.where(forced, SLC_FORCED_SCORE, jnp.where(valid, imp, -1.0))
        nv = n_slc // 8
        sblk = [score[8 * v:8 * v + 8] for v in range(nv)]
        rank = [jnp.zeros((8, tq), F32) for _ in range(nv)]
        sub = lax.broadcasted_iota(I32, (8, tq), 0)
        for k in range(n_slc):
            sk = score[k:k + 1, :]
            kv_ = k // 8
            for v in range(nv):
                ge = jnp.where(sk >= sblk[v], 1.0, 0.0)
                gt = jnp.where(sk > sblk[v], 1.0, 0.0)
                if v > kv_:
                    beats = ge
                elif v < kv_:
                    beats = gt
                else:
                    beats = jnp.where(sub > k - 8 * kv_, ge, gt)
                rank[v] = rank[v] + beats
        rank = jnp.concatenate(rank, axis=0)
        bias = jnp.where((rank < n_top) & valid, 0.0, NEG_INF)
        bias = jnp.concatenate([bias] * hg, axis=1)
        if n_slc < 2 * hd:
            bias = jnp.concatenate([bias, jnp.zeros((2 * hd - n_slc, w), F32)], axis=0)
        q_aug.append(jnp.concatenate([q_rot[g], bias.astype(BF16)], axis=0))

    for g in groups:
        q_scr[g] = q_aug[g]
        m_scr[g] = jnp.full((1, w), NEG_INF, F32)
        l_scr[g] = jnp.zeros((1, w), F32)
        acc_scr[g] = jnp.zeros((hd, w), F32)

    def slc_scores(kt, slot):
        k_tile = ks_ref[pl.ds(pl.multiple_of(kt * NSA_KT, NSA_KT), NSA_KT), :]
        for g in groups:
            s_scr[slot, g] = _dot(k_tile, q_scr[g])

    def slc_update(kt, slot, diagonal):
        for g in groups:
            s = s_scr[slot, g]
            if diagonal:
                kpos = kt * NSA_KT + lax.broadcasted_iota(I32, (NSA_KT, w), 0)
                s = jnp.where(kpos <= t_lane, s, NEG_INF)
            m = m_scr[g]
            m_new = jnp.maximum(m, jnp.max(s, axis=0, keepdims=True))
            alpha = jnp.exp2(m - m_new)
            pp = jnp.exp2(s - m_new)
            l_scr[g] = l_scr[g] * alpha + jnp.sum(pp, axis=0, keepdims=True)
            acc_scr[g] = acc_scr[g] * alpha + _dot(vs_ref[kt, vrows[g], :], pp.astype(BF16))
            m_scr[g] = m_new

    n_full = s0 // NSA_KT

    def slc_pair(p, c):
        a = 2 * p
        slc_scores(a + 1, 1)
        slc_update(a, 0, False)
        slc_scores(a + 2, 0)
        slc_update(a + 1, 1, False)
        return c

    slc_scores(0, 0)
    lax.fori_loop(0, n_full // 2, slc_pair, 0)
    odd = lax.rem(n_full, 2) == 1

    @pl.when(odd)
    def _():
        slc_scores(n_full, 1)
        slc_update(n_full - 1, 0, False)
        slc_update(n_full, 1, True)

    @pl.when(jnp.logical_not(odd))
    def _():
        slc_update(n_full, 0, True)

    o_slc = [acc_scr[g] / l_scr[g] for g in groups]

    kw0 = pl.multiple_of(wt0 * LANES, LANES)
    k_win = kw_ref[pl.ds(kw0, n_win * LANES), :]
    dist = t_lane - (kw0 + lax.broadcasted_iota(I32, (n_win * LANES, w), 0))
    in_window = (dist >= 0) & (dist < WINDOW)
    o_win = []
    for g in groups:
        sm = jnp.where(in_window, _dot(k_win, q_rot[g]), NEG_INF)
        e = jnp.exp2(sm - jnp.max(sm, axis=0, keepdims=True))
        den = jnp.sum(e, axis=0, keepdims=True)
        eb = e.astype(BF16)
        ow = _dot(vw_ref[wt0, vrows[g], :], eb[0:LANES])
        for j in range(1, n_win):
            ow = ow + _dot(vw_ref[wt0 + j, vrows[g], :], eb[j * LANES:(j + 1) * LANES])
        o_win.append(ow / den)

    heads = []
    for g in groups:
        for h in range(hg):
            r = (g * hg + h) * 3
            cols = slice(h * tq, (h + 1) * tq)
            heads.append(gt_ref[r:r + 1, :] * o_cmp[g][:, cols] + gt_ref[r + 1:r + 2, :] * o_slc[g][:, cols]
                         + gt_ref[r + 2:r + 3, :] * o_win[g][:, cols])
    o_ref[...] = jnp.concatenate(heads, axis=0).T


def _cmp_to_slc_t(seq):
    ncp = seq // CMP_STRIDE
    ns = seq // SLC_BLOCK
    cs = np.arange(ncp)[None, :] * CMP_STRIDE
    ss = np.arange(ns)[:, None] * SLC_BLOCK
    ov = np.clip(np.minimum(cs + CMP_BLOCK, ss + SLC_BLOCK) - np.maximum(cs, ss), 0, None) / CMP_BLOCK
    ov[:, ncp - 1] = 0.0
    return jnp.asarray(ov, BF16)


def nsa_mixer(qt, qrt, gt, kc, vct, ks, kw, vs, vw, batch, seq):
    tq = NSA_TQ
    nq = seq // tq
    ncp = seq // CMP_STRIDE
    n_slc = seq // SLC_BLOCK
    assert n_slc <= 2 * HEAD_DIM and n_slc % 8 == 0 and seq >= WINDOW + tq
    w = NSA_GROUP * tq
    col = lambda b, i: (0, b * nq + i)
    return pl.pallas_call(
        functools.partial(_nsa_kernel, n_top=min(SLC_TOP, n_slc)),
        grid=(batch, nq),
        in_specs=[pl.BlockSpec((NSA_Q, tq), col),
                  pl.BlockSpec((NSA_Q, tq), col),
                  pl.BlockSpec((32, tq), col),
                  pl.BlockSpec((1, ncp, 2 * HEAD_DIM), lambda b, i: (b, 0, 0)),
                  pl.BlockSpec((1, 2 * HEAD_DIM, ncp), lambda b, i: (b, 0, 0)),
                  pl.BlockSpec((n_slc, ncp), lambda b, i: (0, 0)),
                  pl.BlockSpec((seq, 2 * LANES), lambda b, i: (b, 0)),
                  pl.BlockSpec((seq, LANES), lambda b, i: (b, 0)),
                  pl.BlockSpec((seq // NSA_KT, LANES, NSA_KT), lambda b, i: (b, 0, 0)),
                  pl.BlockSpec((seq // LANES, LANES, LANES), lambda b, i: (b, 0, 0))],
        out_specs=pl.BlockSpec((tq, NSA_Q), lambda b, i: (b * nq + i, 0)),
        out_shape=jax.ShapeDtypeStruct((batch * seq, NSA_Q), F32),
        scratch_shapes=[pltpu.VMEM((2, NSA_KV_HEADS, NSA_KT, w), F32),
                        pltpu.VMEM((NSA_KV_HEADS, 4 * HEAD_DIM, w), BF16),
                        pltpu.VMEM((NSA_KV_HEADS, 1, w), F32),
                        pltpu.VMEM((NSA_KV_HEADS, 1, w), F32),
                        pltpu.VMEM((NSA_KV_HEADS, HEAD_DIM, w), F32)],
        compiler_params=_params("parallel", "parallel"),
        name="nsa_mixer",
    )(qt, qrt, gt, kc, vct, _cmp_to_slc_t(seq), ks, kw, vs, vw)


def kernel(x, mem, positions, w_in_even, nsa_cmp_pos, nsa_cmp_w1, nsa_cmp_w2, sgu_ln_g, sgu_ln_b, sgu_w, sgu_b, w_out_even, w_in_odd, hgrn_lb_logits, hgrn_norm_g, conv_w, conv_b, w_out_odd, xattn_w_q, xattn_w_kv, xattn_w_o, ln_g, ln_b, router_w, router_b, expert_w_gu, expert_b_gu, expert_w_dn, expert_b_dn):
    batch, seq, d = x.shape
    t = batch * seq
    cos_t, sin_t = rope_tables_t(positions)
    mem2 = mem.reshape(-1, d)
    xf = x.reshape(t, d)
    for layer in range(DEPTH):
        j = layer // 2
        if layer % 2 == 0:
            qt, qrt, ks, kw, vs, vw, gt, kvc, u, v = even_proj(xf, w_in_even[j], cos_t, sin_t, seq)
            kc, vct = nsa_compress(kvc, nsa_cmp_pos[j], nsa_cmp_w1[j], nsa_cmp_w2[j], batch, seq)
            o_a = nsa_mixer(qt, qrt, gt, kc, vct, ks, kw, vs, vw, batch, seq)
            o_b = sgu_mixer(u, v, sgu_ln_g[j], sgu_ln_b[j], sgu_w[j], sgu_b[j])
            w_out = w_out_even[j]
        else:
            proj = matmul(xf, w_in_odd[j].astype(BF16), F32, PROJ_TM)
            o_a = hgrn2_mixer(proj, hgrn_lb_logits, hgrn_norm_g[j], layer, batch, seq)
            o_b = conv_mixer(proj, 4 * HGRN_W // CONV_CH, conv_w[j], conv_b[j], batch, seq)
            w_out = w_out_odd[j]
        xf = outproj_ln(o_a, o_b, xf, w_out, ln_g[layer, 0], ln_b[layer, 0])
        kv = matmul(mem2, xattn_w_kv[layer].astype(BF16), BF16, mem.shape[1]).reshape(batch, mem.shape[1], 2 * d)
        xf = xattn_ln(xf, kv, xattn_w_q[layer], xattn_w_o[layer], ln_g[layer, 1], ln_b[layer, 1], seq)
        xf = moe_ln(xf, router_w[layer], router_b[layer], expert_w_gu, expert_b_gu, expert_w_dn, expert_b_dn,
                    ln_g[layer, 2], ln_b[layer, 2], layer)
    return xf.reshape(batch, seq, d)
```

```python
import functools

import numpy as np
import jax
import jax.numpy as jnp
from jax import lax
from jax.experimental import pallas as pl
from jax.experimental.pallas import tpu as pltpu

F32 = jnp.float32
BF16 = jnp.bfloat16
I32 = jnp.int32

D_MODEL = 1024
DEPTH = 2
HEAD_DIM = 64
NSA_HEADS = 8
NSA_KV_HEADS = 2
NSA_GROUP = NSA_HEADS // NSA_KV_HEADS
CMP_BLOCK = 32
CMP_STRIDE = 16
CMP_HIDDEN = 256
SLC_BLOCK = 64
SLC_TOP = 16
WINDOW = 512
SLC_FORCED_SCORE = 1e4
SGU_GROUPS = 4
SGU_CH = 128
SGU_CHUNK = 128
HGRN_HEADS = 4
HGRN_DK = 128
HGRN_CHUNK = 64
CONV_CH = 512
XATTN_HEADS = 4
XATTN_DIM = D_MODEL // XATTN_HEADS
N_EXPERTS = 32
TOP_K = 4
D_EXPERT = D_MODEL
SWIGLU_LIMIT = 7.0
SWIGLU_ALPHA = 1.702
ROPE_THETA = 10000.0
LN_EPS = 1e-5
RMS_EPS = 1e-6
NEG_INF = -1e30
DEEPNORM_ALPHA = (2 * DEPTH) ** 0.25
LOG2_E = 1.4426950408889634

NSA_Q = NSA_HEADS * HEAD_DIM
SGU_W = SGU_GROUPS * SGU_CH
HGRN_W = HGRN_HEADS * HGRN_DK

VMEM_LIMIT_BYTES = 56 * 1024 * 1024
LANES = 128

PROJ_TM = 512
NSA_TQ = 256
NSA_KT = 256
HGRN_STEP = 32
HGRN_SEG = 1024
MOE_TM = 512
MOE_BM = 512
SEG_ALIGN = 8
MOE_RT = TOP_K * MOE_TM + SEG_ALIGN * N_EXPERTS
MOE_CHUNK = 256


def _params(*sem):
    return pltpu.CompilerParams(dimension_semantics=sem, vmem_limit_bytes=VMEM_LIMIT_BYTES)


def _dot(a, b):
    return jnp.dot(a, b, preferred_element_type=F32)


def _dot_nt(a, b):
    return lax.dot_general(a, b, (((1,), (1,)), ((), ())), preferred_element_type=F32)


def _dot_tn(a, b):
    return lax.dot_general(a, b, (((0,), (0,)), ((), ())), preferred_element_type=F32)


def _gelu(x):
    return 0.5 * x * (1.0 + jnp.tanh(np.sqrt(2.0 / np.pi).astype(np.float32) * (x + 0.044715 * (x * x * x))))


def _sigmoid(x):
    return 1.0 / (1.0 + jnp.exp(-x))


def _layer_norm(y, g, b):
    mu = jnp.mean(y, axis=-1, keepdims=True)
    d = y - mu
    var = jnp.mean(d * d, axis=-1, keepdims=True)
    return d * lax.rsqrt(var + LN_EPS) * g + b


def _rope_kernel(pos_ref, inv_ref, cos_ref, sin_ref):
    ang = pos_ref[...].astype(F32) * inv_ref[...]
    cos_ref[...] = jnp.cos(ang)
    sin_ref[...] = jnp.sin(ang)


def rope_tables_t(positions):
    t = positions.size
    inv = 1.0 / (ROPE_THETA ** (jnp.arange(0, HEAD_DIM, 2, dtype=F32) / HEAD_DIM))
    tn = min(t, 4096)
    half = HEAD_DIM // 2
    return pl.pallas_call(
        _rope_kernel,
        grid=(t // tn,),
        in_specs=[pl.BlockSpec((1, tn), lambda i: (0, i)),
                  pl.BlockSpec((half, 1), lambda i: (0, 0))],
        out_specs=[pl.BlockSpec((half, tn), lambda i: (0, i))] * 2,
        out_shape=[jax.ShapeDtypeStruct((half, t), F32)] * 2,
        compiler_params=_params("parallel"),
        name="rope_tables",
    )(positions.reshape(1, t), inv.reshape(half, 1))


def _mm_kernel(x_ref, w_ref, o_ref):
    o_ref[...] = _dot(x_ref[...].astype(BF16), w_ref[...]).astype(o_ref.dtype)


def matmul(x, w, out_dtype, tm):
    m, k = x.shape
    n = w.shape[1]
    return pl.pallas_call(
        _mm_kernel,
        grid=(m // tm,),
        in_specs=[pl.BlockSpec((tm, k), lambda i: (i, 0)),
                  pl.BlockSpec((k, n), lambda i: (0, 0))],
        out_specs=pl.BlockSpec((tm, n), lambda i: (i, 0)),
        out_shape=jax.ShapeDtypeStruct((m, n), out_dtype),
        compiler_params=_params("parallel"),
        name="matmul",
    )(x, w)


def _outproj_ln_kernel(a_ref, b_ref, x_ref, wa_ref, wb_ref, g_ref, beta_ref, o_ref):
    mix = _dot(a_ref[...].astype(BF16), wa_ref[...]) + _dot(b_ref[...].astype(BF16), wb_ref[...])
    o_ref[...] = _layer_norm(DEEPNORM_ALPHA * x_ref[...] + mix, g_ref[...], beta_ref[...])


def outproj_ln(a, b, x, w_out, g, beta):
    t, d = x.shape
    na, nb = a.shape[1], b.shape[1]
    tm = PROJ_TM
    wa = w_out[:na].astype(BF16)
    wb = w_out[na:].astype(BF16)
    return pl.pallas_call(
        _outproj_ln_kernel,
        grid=(t // tm,),
        in_specs=[pl.BlockSpec((tm, na), lambda i: (i, 0)),
                  pl.BlockSpec((tm, nb), lambda i: (i, 0)),
                  pl.BlockSpec((tm, d), lambda i: (i, 0)),
                  pl.BlockSpec((na, d), lambda i: (0, 0)),
                  pl.BlockSpec((nb, d), lambda i: (0, 0)),
                  pl.BlockSpec((1, d), lambda i: (0, 0)),
                  pl.BlockSpec((1, d), lambda i: (0, 0))],
        out_specs=pl.BlockSpec((tm, d), lambda i: (i, 0)),
        out_shape=jax.ShapeDtypeStruct((t, d), F32),
        compiler_params=_params("parallel"),
        name="outproj_ln",
    )(a, b, x, wa, wb, g.reshape(1, d), beta.reshape(1, d))


def _xattn_kernel(x_ref, wq_ref, kv_ref, wo_ref, g_ref, beta_ref, o_ref):
    x = x_ref[...]
    q = _dot(x.astype(BF16), wq_ref[...])
    hw = XATTN_HEADS * XATTN_DIM
    heads = []
    for h in range(XATTN_HEADS):
        lo = h * XATTN_DIM
        qh = q[:, lo:lo + XATTN_DIM].astype(BF16)
        kh = kv_ref[0, :, lo:lo + XATTN_DIM]
        vh = kv_ref[0, :, hw + lo:hw + lo + XATTN_DIM]
        s = _dot_nt(qh, kh) * (XATTN_DIM ** -0.5)
        m = jnp.max(s, axis=-1, keepdims=True)
        e = jnp.exp(s - m)
        p = e / jnp.sum(e, axis=-1, keepdims=True)
        heads.append(_dot(p.astype(BF16), vh))
    o = jnp.concatenate(heads, axis=-1)
    xa = _dot(o.astype(BF16), wo_ref[...])
    o_ref[...] = _layer_norm(DEEPNORM_ALPHA * x + xa, g_ref[...], beta_ref[...])


def xattn_ln(x, kv, w_q, w_o, g, beta, seq):
    t, d = x.shape
    tm = PROJ_TM
    per_b = seq // tm
    mlen = kv.shape[1]
    return pl.pallas_call(
        _xattn_kernel,
        grid=(t // tm,),
        in_specs=[pl.BlockSpec((tm, d), lambda i: (i, 0)),
                  pl.BlockSpec((d, d), lambda i: (0, 0)),
                  pl.BlockSpec((1, mlen, 2 * d), lambda i: (i // per_b, 0, 0)),
                  pl.BlockSpec((d, d), lambda i: (0, 0)),
                  pl.BlockSpec((1, d), lambda i: (0, 0)),
                  pl.BlockSpec((1, d), lambda i: (0, 0))],
        out_specs=pl.BlockSpec((tm, d), lambda i: (i, 0)),
        out_shape=jax.ShapeDtypeStruct((t, d), F32),
        compiler_params=_params("parallel"),
        name="xattn_ln",
    )(x, w_q.astype(BF16), kv, w_o.astype(BF16), g.reshape(1, d), beta.reshape(1, d))


def _router_kernel(x_ref, wt_ref, b_ref, tri_ref, ltri_ref, row_ref, gate_ref, off_ref, cnt_ref, base_ref,
                   carry_ref):
    i = pl.program_id(0)

    @pl.when(i == 0)
    def _():
        carry_ref[...] = jnp.zeros_like(carry_ref)

    tm = x_ref.shape[0]
    logits = _dot_nt(wt_ref[...], x_ref[...].astype(BF16)) + b_ref[...]
    e_iota = lax.broadcasted_iota(I32, (N_EXPERTS, tm), 0)
    work = logits
    vals, hots = [], []
    for _ in range(TOP_K):
        m = jnp.max(work, axis=0, keepdims=True)
        idx = jnp.min(jnp.where(work == m, e_iota, N_EXPERTS), axis=0, keepdims=True)
        hot = e_iota == idx
        vals.append(m)
        hots.append(hot)
        work = jnp.where(hot, -jnp.inf, work)
    exps = [jnp.exp(v - vals[0]) for v in vals]
    den = exps[0] + exps[1] + exps[2] + exps[3]
    gate_ref[...] = jnp.concatenate([e / den for e in exps], axis=0)

    hot_all = jnp.zeros((N_EXPERTS, tm), F32)
    for hot in hots:
        hot_all = hot_all + jnp.where(hot, 1.0, 0.0)
    rank = _dot(hot_all.astype(BF16), tri_ref[...])
    n = jnp.sum(hot_all, axis=1, keepdims=True)
    seg = jnp.floor((n + (SEG_ALIGN - 1)) * (1.0 / SEG_ALIGN))
    seg = jnp.broadcast_to(seg, (N_EXPERTS, LANES))
    off = _dot(ltri_ref[...], seg.astype(BF16))
    where_row = off[:, 0:1] * SEG_ALIGN + rank
    row_ref[...] = jnp.concatenate(
        [jnp.sum(jnp.where(hot, where_row, 0.0), axis=0, keepdims=True) for hot in hots], axis=0).astype(I32)
    off_ref[0] = off * SEG_ALIGN
    cnt_ref[0] = seg * SEG_ALIGN
    base_ref[0] = carry_ref[...]
    carry_ref[...] = carry_ref[...] + seg * SEG_ALIGN


def moe_route(x, w_router, b_router):
    t, d = x.shape
    tm = MOE_TM
    nt = t // tm
    tri = jnp.asarray(np.triu(np.ones((tm, tm), np.float32), 1), BF16)
    ltri = jnp.asarray(np.tril(np.ones((N_EXPERTS, N_EXPERTS), np.float32), -1), BF16)
    tab = pl.BlockSpec((1, N_EXPERTS, LANES), lambda i: (i, 0, 0))
    tab_shape = jax.ShapeDtypeStruct((nt, N_EXPERTS, LANES), F32)
    return pl.pallas_call(
        _router_kernel,
        grid=(nt,),
        in_specs=[pl.BlockSpec((tm, d), lambda i: (i, 0)),
                  pl.BlockSpec((N_EXPERTS, d), lambda i: (0, 0)),
                  pl.BlockSpec((N_EXPERTS, 1), lambda i: (0, 0)),
                  pl.BlockSpec((tm, tm), lambda i: (0, 0)),
                  pl.BlockSpec((N_EXPERTS, N_EXPERTS), lambda i: (0, 0))],
        out_specs=[pl.BlockSpec((TOP_K, tm), lambda i: (0, i)),
                   pl.BlockSpec((TOP_K, tm), lambda i: (0, i)),
                   tab, tab, tab],
        out_shape=[jax.ShapeDtypeStruct((TOP_K, t), I32),
                   jax.ShapeDtypeStruct((TOP_K, t), F32),
                   tab_shape, tab_shape, tab_shape],
        scratch_shapes=[pltpu.VMEM((N_EXPERTS, LANES), F32)],
        compiler_params=_params("arbitrary"),
        name="moe_router",
    )(x, w_router.T.astype(BF16), b_router.reshape(N_EXPERTS, 1), tri, ltri)


def _segment_copies(off_ref, cnt_ref, dst_ref, make_copy, wait):
    for e in range(N_EXPERTS):
        n = pl.multiple_of(cnt_ref[0, 0, e], SEG_ALIGN)

        @pl.when(n > 0)
        def _():
            cp = make_copy(pl.multiple_of(off_ref[0, 0, e], SEG_ALIGN), pl.multiple_of(dst_ref[0, 0, e], SEG_ALIGN), n)
            if wait:
                cp.wait()
            else:
                cp.start()


def _dispatch_kernel(off_ref, cnt_ref, dst_ref, poff_ref, pcnt_ref, pdst_ref, zero_ref, tail_cnt_ref, tail_dst_ref,
                     rest_ref, x_ref, row_ref, xs_hbm, buf, sem):
    i = pl.program_id(0)
    slot = lax.rem(i, 2)
    tm = x_ref.shape[0]
    xb = x_ref[...].astype(BF16)
    rows = [row_ref[k:k + 1, :] for k in range(TOP_K)]
    for c in range(MOE_RT // MOE_CHUNK):
        rr = c * MOE_CHUNK + lax.broadcasted_iota(I32, (MOE_CHUNK, tm), 0)
        perm = jnp.where(rr == rows[0], 1.0, 0.0)
        for k in range(1, TOP_K):
            perm = perm + jnp.where(rr == rows[k], 1.0, 0.0)
        buf[slot, c * MOE_CHUNK:(c + 1) * MOE_CHUNK, :] = _dot(perm.astype(BF16), xb)

    def copy_from(s):
        def copy(src_row, dst_row, n):
            return pltpu.make_async_copy(buf.at[s, pl.ds(src_row, n)], xs_hbm.at[pl.ds(dst_row, n)], sem.at[s])
        return copy

    @pl.when(i > 0)
    def _():
        _segment_copies(poff_ref, pcnt_ref, pdst_ref, copy_from(1 - slot), wait=True)

    _segment_copies(off_ref, cnt_ref, dst_ref, copy_from(slot), wait=False)

    @pl.when(i == pl.num_programs(0) - 1)
    def _():
        _segment_copies(off_ref, cnt_ref, dst_ref, copy_from(slot), wait=True)
        buf[0, 0:MOE_BM, :] = jnp.zeros((MOE_BM, buf.shape[2]), F32)
        zeros = copy_from(0)
        for wait in (False, True):
            _segment_copies(zero_ref, tail_cnt_ref, tail_dst_ref, zeros, wait=wait)

            def rest(j, c):
                cp = zeros(0, pl.multiple_of(rest_ref[0] + j * MOE_BM, MOE_BM), MOE_BM)
                cp.wait() if wait else cp.start()
                return c
            lax.fori_loop(0, rest_ref[1], rest, 0)


def _seg_spec(n_tiles, shift=0):
    def index(i):
        return (jnp.clip(i + shift, 0, n_tiles - 1), 0, 0)
    return pl.BlockSpec((1, 1, N_EXPERTS), index, memory_space=pltpu.SMEM)


def _smem_whole(shape):
    return pl.BlockSpec(shape, lambda i: (0,) * len(shape), memory_space=pltpu.SMEM)


def moe_dispatch(x, row, seg_off, seg_cnt, seg_dst, tail_cnt, tail_dst, rest, n_rows):
    t, d = x.shape
    tm = MOE_TM
    nt = t // tm
    tab = (1, 1, N_EXPERTS)
    return pl.pallas_call(
        _dispatch_kernel,
        grid=(nt,),
        in_specs=[_seg_spec(nt), _seg_spec(nt), _seg_spec(nt),
                  _seg_spec(nt, -1), _seg_spec(nt, -1), _seg_spec(nt, -1),
                  _smem_whole(tab), _smem_whole(tab), _smem_whole(tab), _smem_whole((2,)),
                  pl.BlockSpec((tm, d), lambda i: (i, 0)),
                  pl.BlockSpec((TOP_K, tm), lambda i: (0, i))],
        out_specs=pl.BlockSpec(memory_space=pl.ANY),
        out_shape=jax.ShapeDtypeStruct((n_rows, d), F32),
        scratch_shapes=[pltpu.VMEM((2, MOE_RT, d), F32), pltpu.SemaphoreType.DMA((2,))],
        compiler_params=_params("arbitrary"),
        name="moe_dispatch",
    )(seg_off, seg_cnt, seg_dst, seg_off, seg_cnt, seg_dst, jnp.zeros(tab, I32), tail_cnt.reshape(tab),
      tail_dst.reshape(tab), rest, x, row)


def _expert_kernel(blk_e_ref, nused_ref, xs_ref, wgu_ref, bgu_ref, wdn_ref, bdn_ref, ys_ref, wgu_bf, wdn_bf):
    i = pl.program_id(0)
    used = i < nused_ref[0]
    new_expert = (i == 0) | (blk_e_ref[i] != blk_e_ref[jnp.maximum(i - 1, 0)])

    @pl.when(used & new_expert)
    def _():
        rows = 128
        for r in range(0, wgu_bf.shape[0], rows):
            wgu_bf[r:r + rows, :] = wgu_ref[0, 0, r:r + rows, :].astype(BF16)
        for r in range(0, wdn_bf.shape[0], rows):
            wdn_bf[r:r + rows, :] = wdn_ref[0, 0, r:r + rows, :].astype(BF16)

    @pl.when(used)
    def _():
        h = _dot(xs_ref[...].astype(BF16), wgu_bf[...]) + bgu_ref[0, 0]
        h_gate = jnp.minimum(h[:, :D_EXPERT], SWIGLU_LIMIT)
        h_up = jnp.clip(h[:, D_EXPERT:], -SWIGLU_LIMIT, SWIGLU_LIMIT)
        act = (h_up + 1.0) * (h_gate * _sigmoid(h_gate * SWIGLU_ALPHA))
        ys_ref[...] = _dot(act.astype(BF16), wdn_bf[...]) + bdn_ref[0, 0]

    @pl.when(jnp.logical_not(used))
    def _():
        ys_ref[...] = jnp.zeros_like(ys_ref)


def moe_experts(xs, blk_e, nused, w_gu, b_gu, w_dn, b_dn, layer):
    n_rows, d = xs.shape
    bm = MOE_BM
    nb = n_rows // bm
    wsel = lambda i, be, nu: (layer, be[i], 0, 0)
    grid_spec = pltpu.PrefetchScalarGridSpec(
        num_scalar_prefetch=2,
        grid=(nb,),
        in_specs=[pl.BlockSpec((bm, d), lambda i, be, nu: (jnp.minimum(i, nu[0] - 1), 0)),
                  pl.BlockSpec((1, 1, d, 2 * D_EXPERT), wsel),
                  pl.BlockSpec((1, 1, 1, 2 * D_EXPERT), wsel),
                  pl.BlockSpec((1, 1, D_EXPERT, d), wsel),
                  pl.BlockSpec((1, 1, 1, d), wsel)],
        out_specs=pl.BlockSpec((bm, d), lambda i, be, nu: (i, 0)),
        scratch_shapes=[pltpu.VMEM((d, 2 * D_EXPERT), BF16), pltpu.VMEM((D_EXPERT, d), BF16)],
    )
    return pl.pallas_call(
        _expert_kernel,
        grid_spec=grid_spec,
        out_shape=jax.ShapeDtypeStruct((n_rows, d), F32),
        compiler_params=_params("arbitrary"),
        name="moe_experts",
    )(blk_e, nused, xs, w_gu, b_gu.reshape(DEPTH, N_EXPERTS, 1, -1), w_dn, b_dn.reshape(DEPTH, N_EXPERTS, 1, -1))


def _combine_ln_kernel(off_ref, cnt_ref, dst_ref, noff_ref, ncnt_ref, ndst_ref, ys_hbm, row_ref, gate_ref, x_ref,
                       g_ref, beta_ref, o_ref, buf, sem):
    i = pl.program_id(0)
    slot = lax.rem(i, 2)
    tm = x_ref.shape[0]

    def copy_into(s):
        def copy(buf_row, ys_row, n):
            return pltpu.make_async_copy(ys_hbm.at[pl.ds(ys_row, n)], buf.at[s, pl.ds(buf_row, n)], sem.at[s])
        return copy

    @pl.when(i == 0)
    def _():
        buf[...] = jnp.zeros_like(buf)
        _segment_copies(off_ref, cnt_ref, dst_ref, copy_into(0), wait=False)

    @pl.when(i + 1 < pl.num_programs(0))
    def _():
        _segment_copies(noff_ref, ncnt_ref, ndst_ref, copy_into(1 - slot), wait=False)

    rows = [jnp.broadcast_to(row_ref[:, k:k + 1], (tm, MOE_CHUNK)) for k in range(TOP_K)]
    gates = [jnp.broadcast_to(gate_ref[:, k:k + 1], (tm, MOE_CHUNK)) for k in range(TOP_K)]
    lane = lax.broadcasted_iota(I32, (tm, MOE_CHUNK), 1)
    _segment_copies(off_ref, cnt_ref, dst_ref, copy_into(slot), wait=True)
    ff = jnp.zeros((tm, x_ref.shape[1]), F32)
    for c in range(MOE_RT // MOE_CHUNK):
        rr = lane + c * MOE_CHUNK
        mix = jnp.where(rr == rows[0], gates[0], 0.0)
        for k in range(1, TOP_K):
            mix = mix + jnp.where(rr == rows[k], gates[k], 0.0)
        ff = ff + _dot(mix.astype(BF16), buf[slot, c * MOE_CHUNK:(c + 1) * MOE_CHUNK, :].astype(BF16))
    o_ref[...] = _layer_norm(DEEPNORM_ALPHA * x_ref[...] + ff, g_ref[...], beta_ref[...])


def moe_combine_ln(ys, row_t, gate_t, seg_off, seg_cnt, seg_dst, x, g, beta):
    t, d = x.shape
    tm = MOE_TM
    nt = t // tm
    return pl.pallas_call(
        _combine_ln_kernel,
        grid=(nt,),
        in_specs=[_seg_spec(nt), _seg_spec(nt), _seg_spec(nt),
                  _seg_spec(nt, 1), _seg_spec(nt, 1), _seg_spec(nt, 1),
                  pl.BlockSpec(memory_space=pl.ANY),
                  pl.BlockSpec((tm, TOP_K), lambda i: (i, 0)),
                  pl.BlockSpec((tm, TOP_K), lambda i: (i, 0)),
                  pl.BlockSpec((tm, d), lambda i: (i, 0)),
                  pl.BlockSpec((1, d), lambda i: (0, 0)),
                  pl.BlockSpec((1, d), lambda i: (0, 0))],
        out_specs=pl.BlockSpec((tm, d), lambda i: (i, 0)),
        out_shape=jax.ShapeDtypeStruct((t, d), F32),
        scratch_shapes=[pltpu.VMEM((2, MOE_RT, d), F32), pltpu.SemaphoreType.DMA((2,))],
        compiler_params=_params("arbitrary"),
        name="moe_combine_ln",
    )(seg_off, seg_cnt, seg_dst, seg_off, seg_cnt, seg_dst, ys, row_t, gate_t, x, g.reshape(1, d),
      beta.reshape(1, d))


def moe_ln(x, w_router, b_router, w_gu, b_gu, w_dn, b_dn, g, beta, layer):
    t, d = x.shape
    bm = MOE_BM
    nt = t // MOE_TM
    row, gate, off, cnt, base = moe_route(x, w_router, b_router)
    seg_off = off[:, :, 0].astype(I32)
    seg_cnt = cnt[:, :, 0].astype(I32)
    seg_base = base[:, :, 0].astype(I32)
    total = seg_base[-1] + seg_cnt[-1]
    padded = (total + bm - 1) // bm * bm
    pend = jnp.cumsum(padded)
    pstart = pend - padded
    n_rows = (t * TOP_K + SEG_ALIGN * N_EXPERTS * nt) // bm * bm + N_EXPERTS * bm
    nb = n_rows // bm
    blk_row = jnp.arange(nb, dtype=I32) * bm
    blk_e = jnp.minimum(jnp.sum((pend[None, :] <= blk_row[:, None]).astype(I32), axis=1), N_EXPERTS - 1)
    nused = (pend[-1] // bm).astype(I32).reshape(1)
    seg_dst = (pstart[None, :] + seg_base).reshape(nt, 1, N_EXPERTS)
    seg_off = seg_off.reshape(nt, 1, N_EXPERTS)
    seg_cnt = seg_cnt.reshape(nt, 1, N_EXPERTS)
    rest = jnp.stack([pend[-1], nb - nused[0]]).astype(I32)
    xs = moe_dispatch(x, row, seg_off, seg_cnt, seg_dst, padded - total, pstart + total, rest, n_rows)
    ys = moe_experts(xs, blk_e, nused, w_gu, b_gu, w_dn, b_dn, layer)
    return moe_combine_ln(ys, row.T, gate.T, seg_off, seg_cnt, seg_dst, x, g, beta)


def _sgu_tile(u, v, lng_ref, lnb_ref, w_ref, bs_ref, o_ref):
    tm = u.shape[0]
    row = lax.broadcasted_iota(I32, (SGU_CHUNK, SGU_CHUNK), 0)
    col = lax.broadcasted_iota(I32, (SGU_CHUNK, SGU_CHUNK), 1)
    causal = row >= col
    for g in range(SGU_GROUPS):
        lo = g * SGU_CH
        vg = _gelu(v[:, lo:lo + SGU_CH])
        vg = _layer_norm(vg, lng_ref[g:g + 1, :], lnb_ref[g:g + 1, :]).astype(BF16)
        wg = jnp.where(causal, w_ref[g], 0.0).astype(BF16)
        bias = bs_ref[:, g:g + 1]
        for n in range(tm // SGU_CHUNK):
            r0 = n * SGU_CHUNK
            mix = _dot(wg, vg[r0:r0 + SGU_CHUNK]) + bias
            o_ref[r0:r0 + SGU_CHUNK, lo:lo + SGU_CH] = (
                _gelu(u[r0:r0 + SGU_CHUNK, lo:lo + SGU_CH]) * mix).astype(o_ref.dtype)


def _odd_proj_kernel(x_ref, w_ref, cw_ref, cb_ref, hg_ref, od_ref, carry_ref):
    @pl.when(pl.program_id(1) == 0)
    def _():
        carry_ref[...] = jnp.zeros_like(carry_ref)

    pr = _dot(x_ref[...].astype(BF16), w_ref[...])
    n_h = 4 * HGRN_W
    hg_ref[...] = pr[:, :n_h]
    h, bg, cg = (pr[:, n_h + j * CONV_CH:n_h + (j + 1) * CONV_CH] for j in range(3))
    z = cg * h
    tm = z.shape[0]
    row = lax.broadcasted_iota(I32, z.shape, 0)
    prev = carry_ref[...]
    z1 = jnp.where(row == 0, prev[7:8, :], pltpu.roll(z, 1, 0))
    z2 = jnp.where(row == 0, prev[6:7, :], jnp.where(row == 1, prev[7:8, :], pltpu.roll(z, 2, 0)))
    y = cw_ref[0:1, :] * z2 + cw_ref[1:2, :] * z1 + cw_ref[2:3, :] * z + cb_ref[...]
    od_ref[...] = (bg * y).astype(od_ref.dtype)
    carry_ref[...] = z[tm - 8:tm, :]


def odd_proj(x, w_in, conv_w, conv_b, batch, seq):
    t, d = x.shape
    tm = PROJ_TM
    per_b = seq // tm
    n = w_in.shape[1]
    n_h = 4 * HGRN_W
    c = CONV_CH
    tok = lambda b, i: (b * per_b + i, 0)
    return pl.pallas_call(
        _odd_proj_kernel,
        grid=(batch, per_b),
        in_specs=[pl.BlockSpec((tm, d), tok),
                  pl.BlockSpec((d, n), lambda b, i: (0, 0)),
                  pl.BlockSpec((3, c), lambda b, i: (0, 0)),
                  pl.BlockSpec((1, c), lambda b, i: (0, 0))],
        out_specs=[pl.BlockSpec((tm, n_h), tok), pl.BlockSpec((tm, c), tok)],
        out_shape=[jax.ShapeDtypeStruct((t, n_h), F32), jax.ShapeDtypeStruct((t, c), BF16)],
        scratch_shapes=[pltpu.VMEM((8, c), F32)],
        compiler_params=_params("arbitrary", "arbitrary"),
        name="odd_proj",
    )(x, w_in.astype(BF16), conv_w, conv_b.reshape(1, c))


def _split3(x):
    hi = x.astype(BF16)
    r1 = x - hi.astype(F32)
    mid = r1.astype(BF16)
    lo = (r1 - mid.astype(F32)).astype(BF16)
    return hi, mid, lo


def _hgrn_kernel(q_ref, f_ref, i_ref, g_ref, lbl_ref, ng_ref, o_ref, state_ref, *, layer):
    c = HGRN_STEP
    dk = HGRN_DK
    n_chunks = q_ref.shape[0] // c
    lw = lbl_ref[...]
    lw = jnp.exp(lw - jnp.max(lw, axis=0, keepdims=True))
    lw = lw / jnp.sum(lw, axis=0, keepdims=True)
    lb = jnp.sum(lw[1:layer + 1], axis=0, keepdims=True)
    row = lax.broadcasted_iota(I32, (c, c), 0)
    col = lax.broadcasted_iota(I32, (c, c), 1)
    tril = jnp.where(row >= col, 1.0, 0.0).astype(BF16)
    sub = lax.broadcasted_iota(I32, (8, dk), 0)

    @pl.when(pl.program_id(1) == 0)
    def _():
        state_ref[...] = jnp.zeros_like(state_ref)

    def decay(ci):
        z = f_ref[pl.ds(pl.multiple_of(ci * c, c), c), :]
        k_all = (1.0 - lb) * _sigmoid(-z)
        log_f = jnp.log1p(-k_all)
        hi, mid, lo = _split3(log_f)
        return k_all, _dot(tril, hi) + _dot(tril, mid) + _dot(tril, lo)

    def chunk(ci, carry):
        k_all, b_all = carry
        ahead = decay(jnp.minimum(ci + 1, n_chunks - 1))
        r0 = pl.multiple_of(ci * c, c)
        q_all = q_ref[pl.ds(r0, c), :]
        v_all = i_ref[pl.ds(r0, c), :]
        g_all = g_ref[pl.ds(r0, c), :]
        outs = []
        for h in range(HGRN_HEADS):
            cols = slice(h * dk, (h + 1) * dk)
            q, k, v, b = q_all[:, cols], k_all[:, cols], v_all[:, cols], b_all[:, cols]
            state_t = state_ref[h]
            o = _dot_nt((q * jnp.exp(b)).astype(BF16), state_t.astype(BF16))

            acc = [o[8 * j:8 * j + 8] for j in range(c // 8)]
            for s in range(c):
                bs = b[s:s + 1, :]
                ks = k[s:s + 1, :]
                vs = v[s:s + 1, :]
                j0 = s // 8
                for j in range(j0, c // 8):
                    dlt = b[8 * j:8 * j + 8] - bs
                    if j == j0:
                        dlt = jnp.where(sub >= s - 8 * j0, dlt, NEG_INF)
                    a = jnp.sum(q[8 * j:8 * j + 8] * ks * jnp.exp(dlt), axis=-1, keepdims=True)
                    acc[j] = acc[j] + a * vs
            o = jnp.concatenate(acc, axis=0)

            b_last = b[c - 1:c, :]
            kd = (k * jnp.exp(b_last - b)).astype(BF16)
            state_ref[h] = jnp.exp(b_last) * state_t + _dot_tn(v.astype(BF16), kd)
            outs.append(o * lax.rsqrt(jnp.mean(o * o, axis=-1, keepdims=True) + RMS_EPS))
        o_ref[pl.ds(r0, c), :] = (jnp.concatenate(outs, axis=1) * ng_ref[...]
                                  * (g_all * _sigmoid(g_all))).astype(o_ref.dtype)
        return ahead

    lax.fori_loop(0, n_chunks, chunk, decay(0))


def hgrn2_mixer(proj, lb_logits, norm_g, layer, batch, seq):
    w = HGRN_W
    seg = min(seq, HGRN_SEG)
    per_b = seq // seg

    def spec(grp):
        return pl.BlockSpec((seg, w), lambda b, i: (b * per_b + i, grp))

    return pl.pallas_call(
        functools.partial(_hgrn_kernel, layer=layer),
        grid=(batch, per_b),
        in_specs=[spec(0), spec(1), spec(2), spec(3),
                  pl.BlockSpec((DEPTH, w), lambda b, i: (0, 0)),
                  pl.BlockSpec((1, w), lambda b, i: (0, 0))],
        out_specs=pl.BlockSpec((seg, w), lambda b, i: (b * per_b + i, 0)),
        out_shape=jax.ShapeDtypeStruct((batch * seq, w), BF16),
        scratch_shapes=[pltpu.VMEM((HGRN_HEADS, HGRN_DK, HGRN_DK), F32)],
        compiler_params=_params("arbitrary", "arbitrary"),
        name="hgrn2_mixer",
    )(proj, proj, proj, proj, lb_logits, norm_g.reshape(1, w))


EVEN_T_ROWS = NSA_Q + 4 * 2 * HEAD_DIM + 32
EVEN_S_COLS = 4 * LANES + 2 * SGU_W


def _even_proj_kernel(x_ref, wt_ref, ws_ref, cos_ref, sin_ref, lng_ref, lnb_ref, sw_ref, sb_ref, qt_ref, qrt_ref,
                      ks_ref, kw_ref, vs_ref, vw_ref, gt_ref, kvc_ref, ob_ref, *, per_b):
    tm = x_ref.shape[0]
    xb = x_ref[...].astype(BF16)
    st = _dot_nt(wt_ref[...], xb)
    cos = cos_ref[...]
    sin = sin_ref[...]
    half = HEAD_DIM // 2
    scale = HEAD_DIM ** -0.5 * LOG2_E

    def rope(blk):
        x1, x2 = blk[:half], blk[half:]
        return jnp.concatenate([x1 * cos - x2 * sin, x2 * cos + x1 * sin], axis=0)

    for hh in range(NSA_HEADS):
        blk = st[hh * HEAD_DIM:(hh + 1) * HEAD_DIM]
        qt_ref[hh * HEAD_DIM:(hh + 1) * HEAD_DIM, :] = (blk * scale).astype(BF16)
        qrt_ref[hh * HEAD_DIM:(hh + 1) * HEAD_DIM, :] = (rope(blk) * scale).astype(BF16)

    kk = jnp.concatenate([rope(st[NSA_Q + j * HEAD_DIM:NSA_Q + (j + 1) * HEAD_DIM]) for j in range(4)], axis=0)
    kk = kk.T
    pos = lax.rem(pl.program_id(0), per_b) * tm + lax.broadcasted_iota(I32, (tm, LANES), 0)
    lane = lax.broadcasted_iota(I32, (tm, LANES), 1)
    member = jnp.where(lane == pos // SLC_BLOCK, 1.0, 0.0)
    ks_ref[...] = jnp.concatenate([kk[:, :LANES], member], axis=1).astype(BF16)
    kw_ref[...] = kk[:, LANES:].astype(BF16)
    v0 = NSA_Q + 4 * HEAD_DIM
    for j in range(tm // NSA_KT):
        vs_ref[j] = st[v0:v0 + LANES, j * NSA_KT:(j + 1) * NSA_KT].astype(BF16)
    for j in range(tm // LANES):
        vw_ref[j] = st[v0 + LANES:v0 + 2 * LANES, j * LANES:(j + 1) * LANES].astype(BF16)
    gt_ref[...] = _sigmoid(st[v0 + 2 * LANES:v0 + 2 * LANES + 32])

    ss = _dot(xb, ws_ref[...])
    for j in range(4):
        kvc_ref[j] = ss[:, j * LANES:j * LANES + HEAD_DIM]
    _sgu_tile(ss[:, 4 * LANES:4 * LANES + SGU_W], ss[:, 4 * LANES + SGU_W:], lng_ref, lnb_ref, sw_ref, sb_ref, ob_ref)


def even_proj(x, w_in, cos_t, sin_t, sgu_ln_g, sgu_ln_b, sgu_w, sgu_b, seq):
    t, d = x.shape
    tm = PROJ_TM
    per_b = seq // tm
    hd = HEAD_DIM
    kv0 = NSA_Q

    def kvcols(i):
        return w_in[:, kv0 + i * 2 * hd:kv0 + (i + 1) * 2 * hd]

    g0 = kv0 + 6 * 2 * hd
    n_gates = 3 * NSA_HEADS
    wt = jnp.concatenate([w_in[:, :NSA_Q], kvcols(2), kvcols(4), kvcols(3), kvcols(5),
                          w_in[:, g0:g0 + n_gates], jnp.zeros((d, 32 - n_gates), F32)], axis=1).T.astype(BF16)
    zpad = jnp.zeros((d, LANES - hd), F32)
    cmp_cols = []
    for i in (0, 1):
        for g in range(NSA_KV_HEADS):
            cmp_cols += [w_in[:, kv0 + i * 2 * hd + g * hd:kv0 + i * 2 * hd + (g + 1) * hd], zpad]
    ws = jnp.concatenate(cmp_cols + [w_in[:, g0 + n_gates:]], axis=1).astype(BF16)
    half = hd // 2
    tok = lambda i: (i, 0)
    tok_t = lambda i: (0, i)
    return pl.pallas_call(
        functools.partial(_even_proj_kernel, per_b=per_b),
        grid=(t // tm,),
        in_specs=[pl.BlockSpec((tm, d), tok),
                  pl.BlockSpec((EVEN_T_ROWS, d), lambda i: (0, 0)),
                  pl.BlockSpec((d, EVEN_S_COLS), lambda i: (0, 0)),
                  pl.BlockSpec((half, tm), tok_t),
                  pl.BlockSpec((half, tm), tok_t),
                  pl.BlockSpec((SGU_GROUPS, SGU_CH), lambda i: (0, 0)),
                  pl.BlockSpec((SGU_GROUPS, SGU_CH), lambda i: (0, 0)),
                  pl.BlockSpec((SGU_GROUPS, SGU_CHUNK, SGU_CHUNK), lambda i: (0, 0, 0)),
                  pl.BlockSpec((SGU_CHUNK, SGU_GROUPS), lambda i: (0, 0))],
        out_specs=[pl.BlockSpec((NSA_Q, tm), tok_t),
                   pl.BlockSpec((NSA_Q, tm), tok_t),
                   pl.BlockSpec((tm, 2 * LANES), tok),
                   pl.BlockSpec((tm, LANES), tok),
                   pl.BlockSpec((tm // NSA_KT, LANES, NSA_KT), lambda i: (i, 0, 0)),
                   pl.BlockSpec((tm // LANES, LANES, LANES), lambda i: (i, 0, 0)),
                   pl.BlockSpec((32, tm), tok_t),
                   pl.BlockSpec((4, tm, hd), lambda i: (0, i, 0)),
                   pl.BlockSpec((tm, SGU_W), tok)],
        out_shape=[jax.ShapeDtypeStruct((NSA_Q, t), BF16),
                   jax.ShapeDtypeStruct((NSA_Q, t), BF16),
                   jax.ShapeDtypeStruct((t, 2 * LANES), BF16),
                   jax.ShapeDtypeStruct((t, LANES), BF16),
                   jax.ShapeDtypeStruct((t // NSA_KT, LANES, NSA_KT), BF16),
                   jax.ShapeDtypeStruct((t // LANES, LANES, LANES), BF16),
                   jax.ShapeDtypeStruct((32, t), F32),
                   jax.ShapeDtypeStruct((4, t, hd), F32),
                   jax.ShapeDtypeStruct((t, SGU_W), BF16)],
        compiler_params=_params("parallel"),
        name="even_proj",
    )(x, wt, ws, cos_t, sin_t, sgu_ln_g, sgu_ln_b, sgu_w, sgu_b.T)


def _compress_kernel(kvc_ref, pos_ref, w1_ref, w2_ref, kc_ref, vct_ref):
    ncp = kc_ref.shape[1]
    assert CMP_BLOCK == 2 * CMP_STRIDE
    for i in range(2):
        outs = []
        for g in range(NSA_KV_HEADS):
            first = jnp.zeros((ncp, CMP_HIDDEN), F32)
            second = jnp.zeros((ncp, CMP_HIDDEN), F32)
            for l in range(CMP_STRIDE):
                rows = kvc_ref[i * NSA_KV_HEADS + g, pl.ds(l, ncp, stride=CMP_STRIDE), :]
                first = first + _dot((rows + pos_ref[i, l:l + 1, :]).astype(BF16), w1_ref[i, l])
                l2 = l + CMP_STRIDE
                second = second + _dot((rows + pos_ref[i, l2:l2 + 1, :]).astype(BF16), w1_ref[i, l2])
            hid = _gelu(first + pltpu.roll(second, ncp - 1, 0))
            outs.append(_dot(hid.astype(BF16), w2_ref[i]))
        if i == 0:
            kc_ref[0] = jnp.concatenate(outs, axis=1).astype(BF16)
        else:
            vct_ref[0] = jnp.concatenate(outs, axis=1).T.astype(BF16)


def nsa_compress(kvc, cmp_pos, cmp_w1, cmp_w2, batch, seq):
    hd = HEAD_DIM
    ncp = seq // CMP_STRIDE
    w1 = cmp_w1.reshape(2, CMP_BLOCK, hd, CMP_HIDDEN).astype(BF16)
    return pl.pallas_call(
        _compress_kernel,
        grid=(batch,),
        in_specs=[pl.BlockSpec((4, seq, hd), lambda b: (0, b, 0)),
                  pl.BlockSpec((2, CMP_BLOCK, hd), lambda b: (0, 0, 0)),
                  pl.BlockSpec((2, CMP_BLOCK, hd, CMP_HIDDEN), lambda b: (0, 0, 0, 0)),
                  pl.BlockSpec((2, CMP_HIDDEN, hd), lambda b: (0, 0, 0))],
        out_specs=[pl.BlockSpec((1, ncp, 2 * hd), lambda b: (b, 0, 0)),
                   pl.BlockSpec((1, 2 * hd, ncp), lambda b: (b, 0, 0))],
        out_shape=[jax.ShapeDtypeStruct((batch, ncp, 2 * hd), BF16),
                   jax.ShapeDtypeStruct((batch, 2 * hd, ncp), BF16)],
        compiler_params=_params("parallel"),
        name="nsa_compress",
    )(kvc, cmp_pos, w1, cmp_w2.astype(BF16))


def _softmax2_cols(s, mask):
    sm = jnp.where(mask, s, NEG_INF)
    m = jnp.max(sm, axis=0, keepdims=True)
    e = jnp.exp2(sm - m)
    return jnp.where(mask, e / jnp.sum(e, axis=0, keepdims=True), 0.0)


def _nsa_kernel(qt_ref, qrt_ref, gt_ref, kc_ref, vct_ref, mct_ref, ks_ref, kw_ref, vs_ref, vw_ref, o_ref,
                s_scr, q_scr, m_scr, l_scr, acc_scr, *, n_top):
    tq = qt_ref.shape[1]
    hg = NSA_GROUP
    hd = HEAD_DIM
    groups = range(NSA_KV_HEADS)
    w = hg * tq
    ncp = kc_ref.shape[1]
    n_slc = mct_ref.shape[0]
    s0 = pl.program_id(1) * tq
    t_lane = s0 + lax.rem(lax.broadcasted_iota(I32, (1, w), 1), tq)
    t_q = s0 + lax.broadcasted_iota(I32, (1, tq), 1)
    zeros_g = jnp.zeros((hd, w), BF16)
    n_win = WINDOW // LANES + tq // LANES
    wt0 = jnp.maximum(s0 // LANES - WINDOW // LANES, 0)
    vrows = [slice(g * hd, (g + 1) * hd) for g in groups]

    def grp(ref, g):
        rows = jnp.concatenate([ref[(g * hg + h) * hd:(g * hg + h + 1) * hd, :] for h in range(hg)], axis=1)
        return jnp.concatenate([rows, zeros_g] if g == 0 else [zeros_g, rows], axis=0)

    q_rot = [grp(qrt_ref, g) for g in groups]
    o_cmp, q_aug = [], []
    for g in groups:
        sc = _dot(kc_ref[0], grp(qt_ref, g))
        c_end = lax.broadcasted_iota(I32, (ncp, w), 0) * CMP_STRIDE + (CMP_BLOCK - 1)
        p = _softmax2_cols(sc, c_end <= t_lane)
        o_cmp.append(_dot(vct_ref[0, vrows[g], :], p.astype(BF16)))
        psum = p[:, 0:tq]
        for h in range(1, hg):
            psum = psum + p[:, h * tq:(h + 1) * tq]
        imp = _dot(mct_ref[...], psum.astype(BF16))

        j_blk = lax.broadcasted_iota(I32, (n_slc, tq), 0)
        cur = t_q // SLC_BLOCK
        forced = (j_blk == 0) | (j_blk == cur) | (j_blk == cur - 1)
        valid = j_blk <= cur
        score = jnp.where(forced, SLC_FORCED_SCORE, jnp.where(valid, imp, -1.0))
        nv = n_slc // 8
        sblk = [score[8 * v:8 * v + 8] for v in range(nv)]
        rank = [jnp.zeros((8, tq), F32) for _ in range(nv)]
        sub = lax.broadcasted_iota(I32, (8, tq), 0)
        for k in range(n_slc):
            sk = score[k:k + 1, :]
            kv_ = k // 8
            for v in range(nv):
                ge = jnp.where(sk >= sblk[v], 1.0, 0.0)
                gt = jnp.where(sk > sblk[v], 1.0, 0.0)
                if v > kv_:
                    beats = ge
                elif v < kv_:
                    beats = gt
                else:
                    beats = jnp.where(sub > k - 8 * kv_, ge, gt)
                rank[v] = rank[v] + beats
        rank = jnp.concatenate(rank, axis=0)
        bias = jnp.where((rank < n_top) & valid, 0.0, NEG_INF)
        bias = jnp.concatenate([bias] * hg, axis=1)
        if n_slc < 2 * hd:
            bias = jnp.concatenate([bias, jnp.zeros((2 * hd - n_slc, w), F32)], axis=0)
        q_aug.append(jnp.concatenate([q_rot[g], bias.astype(BF16)], axis=0))

    for g in groups:
        q_scr[g] = q_aug[g]
        m_scr[g] = jnp.full((1, w), NEG_INF, F32)
        l_scr[g] = jnp.zeros((1, w), F32)
        acc_scr[g] = jnp.zeros((hd, w), F32)

    def slc_scores(kt, slot):
        k_tile = ks_ref[pl.ds(pl.multiple_of(kt * NSA_KT, NSA_KT), NSA_KT), :]
        for g in groups:
            s_scr[slot, g] = _dot(k_tile, q_scr[g])

    def slc_update(kt, slot, diagonal):
        for g in groups:
            s = s_scr[slot, g]
            if diagonal:
                kpos = kt * NSA_KT + lax.broadcasted_iota(I32, (NSA_KT, w), 0)
                s = jnp.where(kpos <= t_lane, s, NEG_INF)
            m = m_scr[g]
            m_new = jnp.maximum(m, jnp.max(s, axis=0, keepdims=True))
            alpha = jnp.exp2(m - m_new)
            pp = jnp.exp2(s - m_new)
            l_scr[g] = l_scr[g] * alpha + jnp.sum(pp, axis=0, keepdims=True)
            acc_scr[g] = acc_scr[g] * alpha + _dot(vs_ref[kt, vrows[g], :], pp.astype(BF16))
            m_scr[g] = m_new

    n_full = s0 // NSA_KT

    def slc_pair(p, c):
        a = 2 * p
        slc_scores(a + 1, 1)
        slc_update(a, 0, False)
        slc_scores(a + 2, 0)
        slc_update(a + 1, 1, False)
        return c

    slc_scores(0, 0)
    lax.fori_loop(0, n_full // 2, slc_pair, 0)
    odd = lax.rem(n_full, 2) == 1

    @pl.when(odd)
    def _():
        slc_scores(n_full, 1)
        slc_update(n_full - 1, 0, False)
        slc_update(n_full, 1, True)

    @pl.when(jnp.logical_not(odd))
    def _():
        slc_update(n_full, 0, True)

    o_slc = [acc_scr[g] / l_scr[g] for g in groups]

    kw0 = pl.multiple_of(wt0 * LANES, LANES)
    k_win = kw_ref[pl.ds(kw0, n_win * LANES), :]
    dist = t_lane - (kw0 + lax.broadcasted_iota(I32, (n_win * LANES, w), 0))
    in_window = (dist >= 0) & (dist < WINDOW)
    o_win = []
    for g in groups:
        sm = jnp.where(in_window, _dot(k_win, q_rot[g]), NEG_INF)
        e = jnp.exp2(sm - jnp.max(sm, axis=0, keepdims=True))
        den = jnp.sum(e, axis=0, keepdims=True)
        eb = e.astype(BF16)
        ow = _dot(vw_ref[wt0, vrows[g], :], eb[0:LANES])
        for j in range(1, n_win):
            ow = ow + _dot(vw_ref[wt0 + j, vrows[g], :], eb[j * LANES:(j + 1) * LANES])
        o_win.append(ow / den)

    heads = []
    for g in groups:
        for h in range(hg):
            r = (g * hg + h) * 3
            cols = slice(h * tq, (h + 1) * tq)
            heads.append(gt_ref[r:r + 1, :] * o_cmp[g][:, cols] + gt_ref[r + 1:r + 2, :] * o_slc[g][:, cols]
                         + gt_ref[r + 2:r + 3, :] * o_win[g][:, cols])
    o_ref[...] = jnp.concatenate(heads, axis=0).T.astype(o_ref.dtype)


def _cmp_to_slc_t(seq):
    ncp = seq // CMP_STRIDE
    ns = seq // SLC_BLOCK
    cs = np.arange(ncp)[None, :] * CMP_STRIDE
    ss = np.arange(ns)[:, None] * SLC_BLOCK
    ov = np.clip(np.minimum(cs + CMP_BLOCK, ss + SLC_BLOCK) - np.maximum(cs, ss), 0, None) / CMP_BLOCK
    ov[:, ncp - 1] = 0.0
    return jnp.asarray(ov, BF16)


def nsa_mixer(qt, qrt, gt, kc, vct, ks, kw, vs, vw, batch, seq):
    tq = NSA_TQ
    nq = seq // tq
    ncp = seq // CMP_STRIDE
    n_slc = seq // SLC_BLOCK
    assert n_slc <= 2 * HEAD_DIM and n_slc % 8 == 0 and seq >= WINDOW + tq
    w = NSA_GROUP * tq
    col = lambda b, i: (0, b * nq + i)
    return pl.pallas_call(
        functools.partial(_nsa_kernel, n_top=min(SLC_TOP, n_slc)),
        grid=(batch, nq),
        in_specs=[pl.BlockSpec((NSA_Q, tq), col),
                  pl.BlockSpec((NSA_Q, tq), col),
                  pl.BlockSpec((32, tq), col),
                  pl.BlockSpec((1, ncp, 2 * HEAD_DIM), lambda b, i: (b, 0, 0)),
                  pl.BlockSpec((1, 2 * HEAD_DIM, ncp), lambda b, i: (b, 0, 0)),
                  pl.BlockSpec((n_slc, ncp), lambda b, i: (0, 0)),
                  pl.BlockSpec((seq, 2 * LANES), lambda b, i: (b, 0)),
                  pl.BlockSpec((seq, LANES), lambda b, i: (b, 0)),
                  pl.BlockSpec((seq // NSA_KT, LANES, NSA_KT), lambda b, i: (b, 0, 0)),
                  pl.BlockSpec((seq // LANES, LANES, LANES), lambda b, i: (b, 0, 0))],
        out_specs=pl.BlockSpec((tq, NSA_Q), lambda b, i: (b * nq + i, 0)),
        out_shape=jax.ShapeDtypeStruct((batch * seq, NSA_Q), BF16),
        scratch_shapes=[pltpu.VMEM((2, NSA_KV_HEADS, NSA_KT, w), F32),
                        pltpu.VMEM((NSA_KV_HEADS, 4 * HEAD_DIM, w), BF16),
                        pltpu.VMEM((NSA_KV_HEADS, 1, w), F32),
                        pltpu.VMEM((NSA_KV_HEADS, 1, w), F32),
                        pltpu.VMEM((NSA_KV_HEADS, HEAD_DIM, w), F32)],
        compiler_params=_params("parallel", "parallel"),
        name="nsa_mixer",
    )(qt, qrt, gt, kc, vct, _cmp_to_slc_t(seq), ks, kw, vs, vw)


def kernel(x, mem, positions, w_in_even, nsa_cmp_pos, nsa_cmp_w1, nsa_cmp_w2, sgu_ln_g, sgu_ln_b, sgu_w, sgu_b, w_out_even, w_in_odd, hgrn_lb_logits, hgrn_norm_g, conv_w, conv_b, w_out_odd, xattn_w_q, xattn_w_kv, xattn_w_o, ln_g, ln_b, router_w, router_b, expert_w_gu, expert_b_gu, expert_w_dn, expert_b_dn):
    batch, seq, d = x.shape
    t = batch * seq
    cos_t, sin_t = rope_tables_t(positions)
    mem2 = mem.reshape(-1, d)
    xf = x.reshape(t, d)
    for layer in range(DEPTH):
        j = layer // 2
        if layer % 2 == 0:
            qt, qrt, ks, kw, vs, vw, gt, kvc, o_b = even_proj(xf, w_in_even[j], cos_t, sin_t, sgu_ln_g[j], sgu_ln_b[j],
                                                              sgu_w[j], sgu_b[j], seq)
            kc, vct = nsa_compress(kvc, nsa_cmp_pos[j], nsa_cmp_w1[j], nsa_cmp_w2[j], batch, seq)
            o_a = nsa_mixer(qt, qrt, gt, kc, vct, ks, kw, vs, vw, batch, seq)
            w_out = w_out_even[j]
        else:
            proj, o_b = odd_proj(xf, w_in_odd[j], conv_w[j], conv_b[j], batch, seq)
            o_a = hgrn2_mixer(proj, hgrn_lb_logits, hgrn_norm_g[j], layer, batch, seq)
            w_out = w_out_odd[j]
        xf = outproj_ln(o_a, o_b, xf, w_out, ln_g[layer, 0], ln_b[layer, 0])
        kv = matmul(mem2, xattn_w_kv[layer].astype(BF16), BF16, mem.shape[1]).reshape(batch, mem.shape[1], 2 * d)
        xf = xattn_ln(xf, kv, xattn_w_q[layer], xattn_w_o[layer], ln_g[layer, 1], ln_b[layer, 1], seq)
        xf = moe_ln(xf, router_w[layer], router_b[layer], expert_w_gu, expert_b_gu, expert_w_dn, expert_b_dn,
                    ln_g[layer, 2], ln_b[layer, 2], layer)
    return xf.reshape(batch, seq, d)
```

```python
import functools

import numpy as np
import jax
import jax.numpy as jnp
from jax import lax
from jax.experimental import pallas as pl
from jax.experimental.pallas import tpu as pltpu

F32 = jnp.float32
BF16 = jnp.bfloat16
I32 = jnp.int32

D_MODEL = 1024
DEPTH = 2
HEAD_DIM = 64
NSA_HEADS = 8
NSA_KV_HEADS = 2
NSA_GROUP = NSA_HEADS // NSA_KV_HEADS
CMP_BLOCK = 32
CMP_STRIDE = 16
CMP_HIDDEN = 256
SLC_BLOCK = 64
SLC_TOP = 16
WINDOW = 512
SLC_FORCED_SCORE = 1e4
SGU_GROUPS = 4
SGU_CH = 128
SGU_CHUNK = 128
HGRN_HEADS = 4
HGRN_DK = 128
HGRN_CHUNK = 64
CONV_CH = 512
XATTN_HEADS = 4
XATTN_DIM = D_MODEL // XATTN_HEADS
N_EXPERTS = 32
TOP_K = 4
D_EXPERT = D_MODEL
SWIGLU_LIMIT = 7.0
SWIGLU_ALPHA = 1.702
ROPE_THETA = 10000.0
LN_EPS = 1e-5
RMS_EPS = 1e-6
NEG_INF = -1e30
DEEPNORM_ALPHA = (2 * DEPTH) ** 0.25
LOG2_E = 1.4426950408889634

NSA_Q = NSA_HEADS * HEAD_DIM
SGU_W = SGU_GROUPS * SGU_CH
HGRN_W = HGRN_HEADS * HGRN_DK

VMEM_LIMIT_BYTES = 56 * 1024 * 1024
LANES = 128

PROJ_TM = 1024
NSA_TQ = 256
NSA_KT = 256
HGRN_STEP = 32
HGRN_SEG = 1024
MOE_TM = 512
ROUTER_TILES = 4
MOE_BM = 512
SEG_ALIGN = 8
MOE_RT = TOP_K * MOE_TM + SEG_ALIGN * N_EXPERTS
MOE_CHUNK = 256


def _params(*sem):
    return pltpu.CompilerParams(dimension_semantics=sem, vmem_limit_bytes=VMEM_LIMIT_BYTES)


def _dot(a, b):
    return jnp.dot(a, b, preferred_element_type=F32)


def _dot_nt(a, b):
    return lax.dot_general(a, b, (((1,), (1,)), ((), ())), preferred_element_type=F32)


def _dot_tn(a, b):
    return lax.dot_general(a, b, (((0,), (0,)), ((), ())), preferred_element_type=F32)


def _gelu(x):
    return 0.5 * x * (1.0 + jnp.tanh(np.sqrt(2.0 / np.pi).astype(np.float32) * (x + 0.044715 * (x * x * x))))


def _sigmoid(x):
    return 1.0 / (1.0 + jnp.exp(-x))


def _layer_norm(y, g, b):
    mu = jnp.mean(y, axis=-1, keepdims=True)
    d = y - mu
    var = jnp.mean(d * d, axis=-1, keepdims=True)
    return d * lax.rsqrt(var + LN_EPS) * g + b


def _rope_kernel(pos_ref, inv_ref, cos_ref, sin_ref):
    ang = pos_ref[...].astype(F32) * inv_ref[...]
    cos_ref[...] = jnp.cos(ang)
    sin_ref[...] = jnp.sin(ang)


def rope_tables_t(positions):
    t = positions.size
    inv = 1.0 / (ROPE_THETA ** (jnp.arange(0, HEAD_DIM, 2, dtype=F32) / HEAD_DIM))
    tn = min(t, 4096)
    half = HEAD_DIM // 2
    return pl.pallas_call(
        _rope_kernel,
        grid=(t // tn,),
        in_specs=[pl.BlockSpec((1, tn), lambda i: (0, i)),
                  pl.BlockSpec((half, 1), lambda i: (0, 0))],
        out_specs=[pl.BlockSpec((half, tn), lambda i: (0, i))] * 2,
        out_shape=[jax.ShapeDtypeStruct((half, t), F32)] * 2,
        compiler_params=_params("parallel"),
        name="rope_tables",
    )(positions.reshape(1, t), inv.reshape(half, 1))


def _mm_kernel(x_ref, w_ref, o_ref):
    o_ref[...] = _dot(x_ref[...].astype(BF16), w_ref[...]).astype(o_ref.dtype)


def matmul(x, w, out_dtype, tm):
    m, k = x.shape
    n = w.shape[1]
    return pl.pallas_call(
        _mm_kernel,
        grid=(m // tm,),
        in_specs=[pl.BlockSpec((tm, k), lambda i: (i, 0)),
                  pl.BlockSpec((k, n), lambda i: (0, 0))],
        out_specs=pl.BlockSpec((tm, n), lambda i: (i, 0)),
        out_shape=jax.ShapeDtypeStruct((m, n), out_dtype),
        compiler_params=_params("parallel"),
        name="matmul",
    )(x, w)


def _outproj_ln_kernel(a_ref, b_ref, x_ref, wa_ref, wb_ref, g_ref, beta_ref, o_ref):
    mix = _dot(a_ref[...].astype(BF16), wa_ref[...]) + _dot(b_ref[...].astype(BF16), wb_ref[...])
    o_ref[...] = _layer_norm(DEEPNORM_ALPHA * x_ref[...] + mix, g_ref[...], beta_ref[...])


def outproj_ln(a, b, x, w_out, g, beta):
    t, d = x.shape
    na, nb = a.shape[1], b.shape[1]
    tm = PROJ_TM
    wa = w_out[:na].astype(BF16)
    wb = w_out[na:].astype(BF16)
    return pl.pallas_call(
        _outproj_ln_kernel,
        grid=(t // tm,),
        in_specs=[pl.BlockSpec((tm, na), lambda i: (i, 0)),
                  pl.BlockSpec((tm, nb), lambda i: (i, 0)),
                  pl.BlockSpec((tm, d), lambda i: (i, 0)),
                  pl.BlockSpec((na, d), lambda i: (0, 0)),
                  pl.BlockSpec((nb, d), lambda i: (0, 0)),
                  pl.BlockSpec((1, d), lambda i: (0, 0)),
                  pl.BlockSpec((1, d), lambda i: (0, 0))],
        out_specs=pl.BlockSpec((tm, d), lambda i: (i, 0)),
        out_shape=jax.ShapeDtypeStruct((t, d), F32),
        compiler_params=_params("parallel"),
        name="outproj_ln",
    )(a, b, x, wa, wb, g.reshape(1, d), beta.reshape(1, d))


def _xattn_kernel(x_ref, wq_ref, kv_ref, wo_ref, g_ref, beta_ref, o_ref):
    x = x_ref[...]
    q = _dot(x.astype(BF16), wq_ref[...])
    hw = XATTN_HEADS * XATTN_DIM
    heads = []
    for h in range(XATTN_HEADS):
        lo = h * XATTN_DIM
        qh = q[:, lo:lo + XATTN_DIM].astype(BF16)
        kh = kv_ref[0, :, lo:lo + XATTN_DIM]
        vh = kv_ref[0, :, hw + lo:hw + lo + XATTN_DIM]
        s = _dot_nt(qh, kh) * (XATTN_DIM ** -0.5)
        m = jnp.max(s, axis=-1, keepdims=True)
        e = jnp.exp(s - m)
        p = e / jnp.sum(e, axis=-1, keepdims=True)
        heads.append(_dot(p.astype(BF16), vh))
    o = jnp.concatenate(heads, axis=-1)
    xa = _dot(o.astype(BF16), wo_ref[...])
    o_ref[...] = _layer_norm(DEEPNORM_ALPHA * x + xa, g_ref[...], beta_ref[...])


def xattn_ln(x, kv, w_q, w_o, g, beta, seq):
    t, d = x.shape
    tm = PROJ_TM
    per_b = seq // tm
    mlen = kv.shape[1]
    return pl.pallas_call(
        _xattn_kernel,
        grid=(t // tm,),
        in_specs=[pl.BlockSpec((tm, d), lambda i: (i, 0)),
                  pl.BlockSpec((d, d), lambda i: (0, 0)),
                  pl.BlockSpec((1, mlen, 2 * d), lambda i: (i // per_b, 0, 0)),
                  pl.BlockSpec((d, d), lambda i: (0, 0)),
                  pl.BlockSpec((1, d), lambda i: (0, 0)),
                  pl.BlockSpec((1, d), lambda i: (0, 0))],
        out_specs=pl.BlockSpec((tm, d), lambda i: (i, 0)),
        out_shape=jax.ShapeDtypeStruct((t, d), F32),
        compiler_params=_params("parallel"),
        name="xattn_ln",
    )(x, w_q.astype(BF16), kv, w_o.astype(BF16), g.reshape(1, d), beta.reshape(1, d))


def _router_kernel(x_ref, wt_ref, b_ref, tri_ref, ltri_ref, row_ref, gate_ref, off_ref, cnt_ref, base_ref,
                   carry_ref):
    i = pl.program_id(0)

    @pl.when(i == 0)
    def _():
        carry_ref[...] = jnp.zeros_like(carry_ref)

    tm = MOE_TM
    tiles = range(x_ref.shape[0] // tm)
    cols = [slice(u * tm, (u + 1) * tm) for u in tiles]
    e_iota = lax.broadcasted_iota(I32, (N_EXPERTS, tm), 0)
    work = [_dot_nt(wt_ref[...], x_ref[cols[u], :].astype(BF16)) + b_ref[...] for u in tiles]
    vals, hots = [[] for _ in tiles], [[] for _ in tiles]
    for _ in range(TOP_K):
        for u in tiles:
            m = jnp.max(work[u], axis=0, keepdims=True)
            idx = jnp.min(jnp.where(work[u] == m, e_iota, N_EXPERTS), axis=0, keepdims=True)
            hot = e_iota == idx
            vals[u].append(m)
            hots[u].append(hot)
            work[u] = jnp.where(hot, -jnp.inf, work[u])
    hot_all, seg, off = [], [], []
    for u in tiles:
        exps = [jnp.exp(v - vals[u][0]) for v in vals[u]]
        den = exps[0] + exps[1] + exps[2] + exps[3]
        gate_ref[:, cols[u]] = jnp.concatenate([e / den for e in exps], axis=0)
        ha = jnp.zeros((N_EXPERTS, tm), F32)
        for hot in hots[u]:
            ha = ha + jnp.where(hot, 1.0, 0.0)
        hot_all.append(ha)
        n = jnp.sum(ha, axis=1, keepdims=True)
        sg = jnp.floor((n + (SEG_ALIGN - 1)) * (1.0 / SEG_ALIGN))
        seg.append(jnp.broadcast_to(sg, (N_EXPERTS, LANES)))
    rank = [_dot(hot_all[u].astype(BF16), tri_ref[...]) for u in tiles]
    off = [_dot(ltri_ref[...], seg[u].astype(BF16)) for u in tiles]
    carry = carry_ref[...]
    for u in tiles:
        where_row = off[u][:, 0:1] * SEG_ALIGN + rank[u]
        row_ref[:, cols[u]] = jnp.concatenate(
            [jnp.sum(jnp.where(hot, where_row, 0.0), axis=0, keepdims=True) for hot in hots[u]], axis=0).astype(I32)
        off_ref[u] = off[u] * SEG_ALIGN
        cnt_ref[u] = seg[u] * SEG_ALIGN
        base_ref[u] = carry
        carry = carry + seg[u] * SEG_ALIGN
    carry_ref[...] = carry


def moe_route(x, w_router, b_router):
    t, d = x.shape
    tm = MOE_TM
    nt = t // tm
    tri = jnp.asarray(np.triu(np.ones((tm, tm), np.float32), 1), BF16)
    ltri = jnp.asarray(np.tril(np.ones((N_EXPERTS, N_EXPERTS), np.float32), -1), BF16)
    per = ROUTER_TILES if nt % ROUTER_TILES == 0 else 1
    tab = pl.BlockSpec((per, N_EXPERTS, LANES), lambda i: (i, 0, 0))
    tab_shape = jax.ShapeDtypeStruct((nt, N_EXPERTS, LANES), F32)
    return pl.pallas_call(
        _router_kernel,
        grid=(nt // per,),
        in_specs=[pl.BlockSpec((per * tm, d), lambda i: (i, 0)),
                  pl.BlockSpec((N_EXPERTS, d), lambda i: (0, 0)),
                  pl.BlockSpec((N_EXPERTS, 1), lambda i: (0, 0)),
                  pl.BlockSpec((tm, tm), lambda i: (0, 0)),
                  pl.BlockSpec((N_EXPERTS, N_EXPERTS), lambda i: (0, 0))],
        out_specs=[pl.BlockSpec((TOP_K, per * tm), lambda i: (0, i)),
                   pl.BlockSpec((TOP_K, per * tm), lambda i: (0, i)),
                   tab, tab, tab],
        out_shape=[jax.ShapeDtypeStruct((TOP_K, t), I32),
                   jax.ShapeDtypeStruct((TOP_K, t), F32),
                   tab_shape, tab_shape, tab_shape],
        scratch_shapes=[pltpu.VMEM((N_EXPERTS, LANES), F32)],
        compiler_params=_params("arbitrary"),
        name="moe_router",
    )(x, w_router.T.astype(BF16), b_router.reshape(N_EXPERTS, 1), tri, ltri)


def _segment_copies(off_ref, cnt_ref, dst_ref, make_copy, wait):
    for e in range(N_EXPERTS):
        n = pl.multiple_of(cnt_ref[0, 0, e], SEG_ALIGN)

        @pl.when(n > 0)
        def _():
            cp = make_copy(pl.multiple_of(off_ref[0, 0, e], SEG_ALIGN), pl.multiple_of(dst_ref[0, 0, e], SEG_ALIGN), n)
            if wait:
                cp.wait()
            else:
                cp.start()


def _dispatch_kernel(off_ref, cnt_ref, dst_ref, poff_ref, pcnt_ref, pdst_ref, zero_ref, tail_cnt_ref, tail_dst_ref,
                     rest_ref, x_ref, row_ref, xs_hbm, buf, sem):
    i = pl.program_id(0)
    slot = lax.rem(i, 2)
    tm = x_ref.shape[0]
    xb = x_ref[...].astype(BF16)
    rows = [row_ref[k:k + 1, :] for k in range(TOP_K)]
    for c in range(MOE_RT // MOE_CHUNK):
        rr = c * MOE_CHUNK + lax.broadcasted_iota(I32, (MOE_CHUNK, tm), 0)
        perm = jnp.where(rr == rows[0], 1.0, 0.0)
        for k in range(1, TOP_K):
            perm = perm + jnp.where(rr == rows[k], 1.0, 0.0)
        buf[slot, c * MOE_CHUNK:(c + 1) * MOE_CHUNK, :] = _dot(perm.astype(BF16), xb)

    def copy_from(s):
        def copy(src_row, dst_row, n):
            return pltpu.make_async_copy(buf.at[s, pl.ds(src_row, n)], xs_hbm.at[pl.ds(dst_row, n)], sem.at[s])
        return copy

    @pl.when(i > 0)
    def _():
        _segment_copies(poff_ref, pcnt_ref, pdst_ref, copy_from(1 - slot), wait=True)

    _segment_copies(off_ref, cnt_ref, dst_ref, copy_from(slot), wait=False)

    @pl.when(i == pl.num_programs(0) - 1)
    def _():
        _segment_copies(off_ref, cnt_ref, dst_ref, copy_from(slot), wait=True)
        buf[0, 0:MOE_BM, :] = jnp.zeros((MOE_BM, buf.shape[2]), F32)
        zeros = copy_from(0)
        for wait in (False, True):
            _segment_copies(zero_ref, tail_cnt_ref, tail_dst_ref, zeros, wait=wait)

            def rest(j, c):
                cp = zeros(0, pl.multiple_of(rest_ref[0] + j * MOE_BM, MOE_BM), MOE_BM)
                cp.wait() if wait else cp.start()
                return c
            lax.fori_loop(0, rest_ref[1], rest, 0)


def _seg_spec(n_tiles, shift=0):
    def index(i):
        return (jnp.clip(i + shift, 0, n_tiles - 1), 0, 0)
    return pl.BlockSpec((1, 1, N_EXPERTS), index, memory_space=pltpu.SMEM)


def _smem_whole(shape):
    return pl.BlockSpec(shape, lambda i: (0,) * len(shape), memory_space=pltpu.SMEM)


def moe_dispatch(x, row, seg_off, seg_cnt, seg_dst, tail_cnt, tail_dst, rest, n_rows):
    t, d = x.shape
    tm = MOE_TM
    nt = t // tm
    tab = (1, 1, N_EXPERTS)
    return pl.pallas_call(
        _dispatch_kernel,
        grid=(nt,),
        in_specs=[_seg_spec(nt), _seg_spec(nt), _seg_spec(nt),
                  _seg_spec(nt, -1), _seg_spec(nt, -1), _seg_spec(nt, -1),
                  _smem_whole(tab), _smem_whole(tab), _smem_whole(tab), _smem_whole((2,)),
                  pl.BlockSpec((tm, d), lambda i: (i, 0)),
                  pl.BlockSpec((TOP_K, tm), lambda i: (0, i))],
        out_specs=pl.BlockSpec(memory_space=pl.ANY),
        out_shape=jax.ShapeDtypeStruct((n_rows, d), F32),
        scratch_shapes=[pltpu.VMEM((2, MOE_RT, d), F32), pltpu.SemaphoreType.DMA((2,))],
        compiler_params=_params("arbitrary"),
        name="moe_dispatch",
    )(seg_off, seg_cnt, seg_dst, seg_off, seg_cnt, seg_dst, jnp.zeros(tab, I32), tail_cnt.reshape(tab),
      tail_dst.reshape(tab), rest, x, row)


def _expert_kernel(blk_e_ref, nused_ref, xs_ref, wgu_ref, bgu_ref, wdn_ref, bdn_ref, ys_ref, wgu_bf, wdn_bf):
    i = pl.program_id(0)
    used = i < nused_ref[0]
    new_expert = (i == 0) | (blk_e_ref[i] != blk_e_ref[jnp.maximum(i - 1, 0)])

    @pl.when(used & new_expert)
    def _():
        rows = 128
        for r in range(0, wgu_bf.shape[0], rows):
            wgu_bf[r:r + rows, :] = wgu_ref[0, 0, r:r + rows, :].astype(BF16)
        for r in range(0, wdn_bf.shape[0], rows):
            wdn_bf[r:r + rows, :] = wdn_ref[0, 0, r:r + rows, :].astype(BF16)

    @pl.when(used)
    def _():
        h = _dot(xs_ref[...].astype(BF16), wgu_bf[...]) + bgu_ref[0, 0]
        h_gate = jnp.minimum(h[:, :D_EXPERT], SWIGLU_LIMIT)
        h_up = jnp.clip(h[:, D_EXPERT:], -SWIGLU_LIMIT, SWIGLU_LIMIT)
        act = (h_up + 1.0) * (h_gate * _sigmoid(h_gate * SWIGLU_ALPHA))
        ys_ref[...] = _dot(act.astype(BF16), wdn_bf[...]) + bdn_ref[0, 0]

    @pl.when(jnp.logical_not(used))
    def _():
        ys_ref[...] = jnp.zeros_like(ys_ref)


def moe_experts(xs, blk_e, nused, w_gu, b_gu, w_dn, b_dn, layer):
    n_rows, d = xs.shape
    bm = MOE_BM
    nb = n_rows // bm
    wsel = lambda i, be, nu: (layer, be[i], 0, 0)
    grid_spec = pltpu.PrefetchScalarGridSpec(
        num_scalar_prefetch=2,
        grid=(nb,),
        in_specs=[pl.BlockSpec((bm, d), lambda i, be, nu: (jnp.minimum(i, nu[0] - 1), 0)),
                  pl.BlockSpec((1, 1, d, 2 * D_EXPERT), wsel),
                  pl.BlockSpec((1, 1, 1, 2 * D_EXPERT), wsel),
                  pl.BlockSpec((1, 1, D_EXPERT, d), wsel),
                  pl.BlockSpec((1, 1, 1, d), wsel)],
        out_specs=pl.BlockSpec((bm, d), lambda i, be, nu: (i, 0)),
        scratch_shapes=[pltpu.VMEM((d, 2 * D_EXPERT), BF16), pltpu.VMEM((D_EXPERT, d), BF16)],
    )
    return pl.pallas_call(
        _expert_kernel,
        grid_spec=grid_spec,
        out_shape=jax.ShapeDtypeStruct((n_rows, d), F32),
        compiler_params=_params("arbitrary"),
        name="moe_experts",
    )(blk_e, nused, xs, w_gu, b_gu.reshape(DEPTH, N_EXPERTS, 1, -1), w_dn, b_dn.reshape(DEPTH, N_EXPERTS, 1, -1))


def _combine_ln_kernel(off_ref, cnt_ref, dst_ref, noff_ref, ncnt_ref, ndst_ref, ys_hbm, row_ref, gate_ref, x_ref,
                       g_ref, beta_ref, o_ref, buf, sem):
    i = pl.program_id(0)
    slot = lax.rem(i, 2)
    tm = x_ref.shape[0]

    def copy_into(s):
        def copy(buf_row, ys_row, n):
            return pltpu.make_async_copy(ys_hbm.at[pl.ds(ys_row, n)], buf.at[s, pl.ds(buf_row, n)], sem.at[s])
        return copy

    @pl.when(i == 0)
    def _():
        buf[...] = jnp.zeros_like(buf)
        _segment_copies(off_ref, cnt_ref, dst_ref, copy_into(0), wait=False)

    @pl.when(i + 1 < pl.num_programs(0))
    def _():
        _segment_copies(noff_ref, ncnt_ref, ndst_ref, copy_into(1 - slot), wait=False)

    rows = [jnp.broadcast_to(row_ref[:, k:k + 1], (tm, MOE_CHUNK)) for k in range(TOP_K)]
    gates = [jnp.broadcast_to(gate_ref[:, k:k + 1], (tm, MOE_CHUNK)) for k in range(TOP_K)]
    lane = lax.broadcasted_iota(I32, (tm, MOE_CHUNK), 1)
    _segment_copies(off_ref, cnt_ref, dst_ref, copy_into(slot), wait=True)
    ff = jnp.zeros((tm, x_ref.shape[1]), F32)
    for c in range(MOE_RT // MOE_CHUNK):
        rr = lane + c * MOE_CHUNK
        mix = jnp.where(rr == rows[0], gates[0], 0.0)
        for k in range(1, TOP_K):
            mix = mix + jnp.where(rr == rows[k], gates[k], 0.0)
        ff = ff + _dot(mix.astype(BF16), buf[slot, c * MOE_CHUNK:(c + 1) * MOE_CHUNK, :].astype(BF16))
    o_ref[...] = _layer_norm(DEEPNORM_ALPHA * x_ref[...] + ff, g_ref[...], beta_ref[...])


def moe_combine_ln(ys, row_t, gate_t, seg_off, seg_cnt, seg_dst, x, g, beta):
    t, d = x.shape
    tm = MOE_TM
    nt = t // tm
    return pl.pallas_call(
        _combine_ln_kernel,
        grid=(nt,),
        in_specs=[_seg_spec(nt), _seg_spec(nt), _seg_spec(nt),
                  _seg_spec(nt, 1), _seg_spec(nt, 1), _seg_spec(nt, 1),
                  pl.BlockSpec(memory_space=pl.ANY),
                  pl.BlockSpec((tm, TOP_K), lambda i: (i, 0)),
                  pl.BlockSpec((tm, TOP_K), lambda i: (i, 0)),
                  pl.BlockSpec((tm, d), lambda i: (i, 0)),
                  pl.BlockSpec((1, d), lambda i: (0, 0)),
                  pl.BlockSpec((1, d), lambda i: (0, 0))],
        out_specs=pl.BlockSpec((tm, d), lambda i: (i, 0)),
        out_shape=jax.ShapeDtypeStruct((t, d), F32),
        scratch_shapes=[pltpu.VMEM((2, MOE_RT, d), F32), pltpu.SemaphoreType.DMA((2,))],
        compiler_params=_params("arbitrary"),
        name="moe_combine_ln",
    )(seg_off, seg_cnt, seg_dst, seg_off, seg_cnt, seg_dst, ys, row_t, gate_t, x, g.reshape(1, d),
      beta.reshape(1, d))


def moe_ln(x, w_router, b_router, w_gu, b_gu, w_dn, b_dn, g, beta, layer):
    t, d = x.shape
    bm = MOE_BM
    nt = t // MOE_TM
    row, gate, off, cnt, base = moe_route(x, w_router, b_router)
    seg_off = off[:, :, 0].astype(I32)
    seg_cnt = cnt[:, :, 0].astype(I32)
    seg_base = base[:, :, 0].astype(I32)
    total = seg_base[-1] + seg_cnt[-1]
    padded = (total + bm - 1) // bm * bm
    pend = jnp.cumsum(padded)
    pstart = pend - padded
    n_rows = (t * TOP_K + SEG_ALIGN * N_EXPERTS * nt) // bm * bm + N_EXPERTS * bm
    nb = n_rows // bm
    blk_row = jnp.arange(nb, dtype=I32) * bm
    blk_e = jnp.minimum(jnp.sum((pend[None, :] <= blk_row[:, None]).astype(I32), axis=1), N_EXPERTS - 1)
    nused = (pend[-1] // bm).astype(I32).reshape(1)
    seg_dst = (pstart[None, :] + seg_base).reshape(nt, 1, N_EXPERTS)
    seg_off = seg_off.reshape(nt, 1, N_EXPERTS)
    seg_cnt = seg_cnt.reshape(nt, 1, N_EXPERTS)
    rest = jnp.stack([pend[-1], nb - nused[0]]).astype(I32)
    xs = moe_dispatch(x, row, seg_off, seg_cnt, seg_dst, padded - total, pstart + total, rest, n_rows)
    ys = moe_experts(xs, blk_e, nused, w_gu, b_gu, w_dn, b_dn, layer)
    return moe_combine_ln(ys, row.T, gate.T, seg_off, seg_cnt, seg_dst, x, g, beta)


def _sgu_tile(u, v, lng_ref, lnb_ref, w_ref, bs_ref, o_ref):
    tm = u.shape[0]
    row = lax.broadcasted_iota(I32, (SGU_CHUNK, SGU_CHUNK), 0)
    col = lax.broadcasted_iota(I32, (SGU_CHUNK, SGU_CHUNK), 1)
    causal = row >= col
    for g in range(SGU_GROUPS):
        lo = g * SGU_CH
        vg = _gelu(v[:, lo:lo + SGU_CH])
        vg = _layer_norm(vg, lng_ref[g:g + 1, :], lnb_ref[g:g + 1, :]).astype(BF16)
        wg = jnp.where(causal, w_ref[g], 0.0).astype(BF16)
        bias = bs_ref[:, g:g + 1]
        for n in range(tm // SGU_CHUNK):
            r0 = n * SGU_CHUNK
            mix = _dot(wg, vg[r0:r0 + SGU_CHUNK]) + bias
            o_ref[r0:r0 + SGU_CHUNK, lo:lo + SGU_CH] = (
                _gelu(u[r0:r0 + SGU_CHUNK, lo:lo + SGU_CH]) * mix).astype(o_ref.dtype)


def _odd_proj_kernel(x_ref, w_ref, cw_ref, cb_ref, hg_ref, od_ref, carry_ref):
    @pl.when(pl.program_id(1) == 0)
    def _():
        carry_ref[...] = jnp.zeros_like(carry_ref)

    pr = _dot(x_ref[...].astype(BF16), w_ref[...])
    n_h = 4 * HGRN_W
    hg_ref[...] = pr[:, :n_h]
    h, bg, cg = (pr[:, n_h + j * CONV_CH:n_h + (j + 1) * CONV_CH] for j in range(3))
    z = cg * h
    tm = z.shape[0]
    row = lax.broadcasted_iota(I32, z.shape, 0)
    prev = carry_ref[...]
    z1 = jnp.where(row == 0, prev[7:8, :], pltpu.roll(z, 1, 0))
    z2 = jnp.where(row == 0, prev[6:7, :], jnp.where(row == 1, prev[7:8, :], pltpu.roll(z, 2, 0)))
    y = cw_ref[0:1, :] * z2 + cw_ref[1:2, :] * z1 + cw_ref[2:3, :] * z + cb_ref[...]
    od_ref[...] = (bg * y).astype(od_ref.dtype)
    carry_ref[...] = z[tm - 8:tm, :]


def odd_proj(x, w_in, conv_w, conv_b, batch, seq):
    t, d = x.shape
    tm = PROJ_TM
    per_b = seq // tm
    n = w_in.shape[1]
    n_h = 4 * HGRN_W
    c = CONV_CH
    tok = lambda b, i: (b * per_b + i, 0)
    return pl.pallas_call(
        _odd_proj_kernel,
        grid=(batch, per_b),
        in_specs=[pl.BlockSpec((tm, d), tok),
                  pl.BlockSpec((d, n), lambda b, i: (0, 0)),
                  pl.BlockSpec((3, c), lambda b, i: (0, 0)),
                  pl.BlockSpec((1, c), lambda b, i: (0, 0))],
        out_specs=[pl.BlockSpec((tm, n_h), tok), pl.BlockSpec((tm, c), tok)],
        out_shape=[jax.ShapeDtypeStruct((t, n_h), F32), jax.ShapeDtypeStruct((t, c), BF16)],
        scratch_shapes=[pltpu.VMEM((8, c), F32)],
        compiler_params=_params("arbitrary", "arbitrary"),
        name="odd_proj",
    )(x, w_in.astype(BF16), conv_w, conv_b.reshape(1, c))


def _split3(x):
    hi = x.astype(BF16)
    r1 = x - hi.astype(F32)
    mid = r1.astype(BF16)
    lo = (r1 - mid.astype(F32)).astype(BF16)
    return hi, mid, lo


def _hgrn_kernel(q_ref, f_ref, i_ref, g_ref, lbl_ref, ng_ref, o_ref, state_ref, *, layer):
    c = HGRN_STEP
    dk = HGRN_DK
    n_chunks = q_ref.shape[0] // c
    lw = lbl_ref[...]
    lw = jnp.exp(lw - jnp.max(lw, axis=0, keepdims=True))
    lw = lw / jnp.sum(lw, axis=0, keepdims=True)
    lb = jnp.sum(lw[1:layer + 1], axis=0, keepdims=True)
    row = lax.broadcasted_iota(I32, (c, c), 0)
    col = lax.broadcasted_iota(I32, (c, c), 1)
    tril = jnp.where(row >= col, 1.0, 0.0).astype(BF16)
    sub = lax.broadcasted_iota(I32, (8, dk), 0)

    @pl.when(pl.program_id(1) == 0)
    def _():
        state_ref[...] = jnp.zeros_like(state_ref)

    def decay(ci):
        z = f_ref[pl.ds(pl.multiple_of(ci * c, c), c), :]
        k_all = (1.0 - lb) * _sigmoid(-z)
        log_f = jnp.log1p(-k_all)
        hi, mid, lo = _split3(log_f)
        b_all = _dot(tril, hi) + _dot(tril, mid) + _dot(tril, lo)
        return k_all, b_all * LOG2_E

    def chunk(ci, carry):
        k_all, b_all = carry
        ahead = decay(jnp.minimum(ci + 1, n_chunks - 1))
        r0 = pl.multiple_of(ci * c, c)
        q_all = q_ref[pl.ds(r0, c), :]
        v_all = i_ref[pl.ds(r0, c), :]
        g_all = g_ref[pl.ds(r0, c), :]
        heads = range(HGRN_HEADS)
        nblk = c // 8
        blk = [slice(8 * j, 8 * j + 8) for j in range(nblk)]
        q, k, v, b = ([x[:, h * dk:(h + 1) * dk] for h in heads] for x in (q_all, k_all, v_all, b_all))
        states = [state_ref[h] for h in heads]
        o = [_dot_nt((q[h] * jnp.exp2(b[h])).astype(BF16), states[h].astype(BF16)) for h in heads]
        b_last = [b[h][c - 1:c, :] for h in heads]
        for h in heads:
            kd = (k[h] * jnp.exp2(b_last[h] - b[h])).astype(BF16)
            state_ref[h] = jnp.exp2(b_last[h]) * states[h] + _dot_tn(v[h].astype(BF16), kd)

        k_dec = [jnp.concatenate([k[h][blk[j]] * jnp.exp2(b[h][8 * j + 7:8 * j + 8] - b[h][blk[j]])
                                  for j in range(nblk)], axis=0) for h in heads]
        q_dec = [{(j, j0): q[h][blk[j]] * jnp.exp2(b[h][blk[j]] - b[h][8 * j0 + 7:8 * j0 + 8])
                  for j0 in range(nblk) for j in range(j0 + 1, nblk)} for h in heads]
        acc = [[o[h][blk[j]] for j in range(nblk)] for h in heads]
        for s in range(c):
            j0 = s // 8
            for h in heads:
                vs = v[h][s:s + 1, :]
                dlt = jnp.where(sub >= s - 8 * j0, b[h][blk[j0]] - b[h][s:s + 1, :], NEG_INF)
                a = jnp.sum(q[h][blk[j0]] * k[h][s:s + 1, :] * jnp.exp2(dlt), axis=-1, keepdims=True)
                acc[h][j0] = acc[h][j0] + a * vs
                for j in range(j0 + 1, nblk):
                    a = jnp.sum(q_dec[h][j, j0] * k_dec[h][s:s + 1, :], axis=-1, keepdims=True)
                    acc[h][j] = acc[h][j] + a * vs
        outs = []
        for h in heads:
            oh = jnp.concatenate(acc[h], axis=0)
            outs.append(oh * lax.rsqrt(jnp.mean(oh * oh, axis=-1, keepdims=True) + RMS_EPS))
        o_ref[pl.ds(r0, c), :] = (jnp.concatenate(outs, axis=1) * ng_ref[...]
                                  * (g_all * _sigmoid(g_all))).astype(o_ref.dtype)
        return ahead

    lax.fori_loop(0, n_chunks, chunk, decay(0))


def hgrn2_mixer(proj, lb_logits, norm_g, layer, batch, seq):
    w = HGRN_W
    seg = min(seq, HGRN_SEG)
    per_b = seq // seg

    def spec(grp):
        return pl.BlockSpec((seg, w), lambda b, i: (b * per_b + i, grp))

    return pl.pallas_call(
        functools.partial(_hgrn_kernel, layer=layer),
        grid=(batch, per_b),
        in_specs=[spec(0), spec(1), spec(2), spec(3),
                  pl.BlockSpec((DEPTH, w), lambda b, i: (0, 0)),
                  pl.BlockSpec((1, w), lambda b, i: (0, 0))],
        out_specs=pl.BlockSpec((seg, w), lambda b, i: (b * per_b + i, 0)),
        out_shape=jax.ShapeDtypeStruct((batch * seq, w), BF16),
        scratch_shapes=[pltpu.VMEM((HGRN_HEADS, HGRN_DK, HGRN_DK), F32)],
        compiler_params=_params("arbitrary", "arbitrary"),
        name="hgrn2_mixer",
    )(proj, proj, proj, proj, lb_logits, norm_g.reshape(1, w))


EVEN_T_ROWS = NSA_Q + 4 * 2 * HEAD_DIM + 32
EVEN_S_COLS = 4 * LANES + 2 * SGU_W


def _even_proj_kernel(x_ref, wt_ref, ws_ref, cos_ref, sin_ref, lng_ref, lnb_ref, sw_ref, sb_ref, qt_ref, qrt_ref,
                      ks_ref, kw_ref, vs_ref, vw_ref, gt_ref, kvc_ref, ob_ref, *, per_b):
    tm = x_ref.shape[0]
    xb = x_ref[...].astype(BF16)
    st = _dot_nt(wt_ref[...], xb)
    cos = cos_ref[...]
    sin = sin_ref[...]
    half = HEAD_DIM // 2
    scale = HEAD_DIM ** -0.5 * LOG2_E

    def rope(blk):
        x1, x2 = blk[:half], blk[half:]
        return jnp.concatenate([x1 * cos - x2 * sin, x2 * cos + x1 * sin], axis=0)

    for hh in range(NSA_HEADS):
        blk = st[hh * HEAD_DIM:(hh + 1) * HEAD_DIM]
        qt_ref[hh * HEAD_DIM:(hh + 1) * HEAD_DIM, :] = (blk * scale).astype(BF16)
        qrt_ref[hh * HEAD_DIM:(hh + 1) * HEAD_DIM, :] = (rope(blk) * scale).astype(BF16)

    kk = jnp.concatenate([rope(st[NSA_Q + j * HEAD_DIM:NSA_Q + (j + 1) * HEAD_DIM]) for j in range(4)], axis=0)
    kk = kk.T
    pos = lax.rem(pl.program_id(0), per_b) * tm + lax.broadcasted_iota(I32, (tm, LANES), 0)
    lane = lax.broadcasted_iota(I32, (tm, LANES), 1)
    member = jnp.where(lane == pos // SLC_BLOCK, 1.0, 0.0)
    ks_ref[...] = jnp.concatenate([kk[:, :LANES], member], axis=1).astype(BF16)
    kw_ref[...] = kk[:, LANES:].astype(BF16)
    v0 = NSA_Q + 4 * HEAD_DIM
    for j in range(tm // NSA_KT):
        vs_ref[j] = st[v0:v0 + LANES, j * NSA_KT:(j + 1) * NSA_KT].astype(BF16)
    for j in range(tm // LANES):
        vw_ref[j] = st[v0 + LANES:v0 + 2 * LANES, j * LANES:(j + 1) * LANES].astype(BF16)
    gt_ref[...] = _sigmoid(st[v0 + 2 * LANES:v0 + 2 * LANES + 32])

    ss = _dot(xb, ws_ref[...])
    for j in range(4):
        kvc_ref[j] = ss[:, j * LANES:j * LANES + HEAD_DIM]
    _sgu_tile(ss[:, 4 * LANES:4 * LANES + SGU_W], ss[:, 4 * LANES + SGU_W:], lng_ref, lnb_ref, sw_ref, sb_ref, ob_ref)


def even_proj(x, w_in, cos_t, sin_t, sgu_ln_g, sgu_ln_b, sgu_w, sgu_b, seq):
    t, d = x.shape
    tm = PROJ_TM
    per_b = seq // tm
    hd = HEAD_DIM
    kv0 = NSA_Q

    def kvcols(i):
        return w_in[:, kv0 + i * 2 * hd:kv0 + (i + 1) * 2 * hd]

    g0 = kv0 + 6 * 2 * hd
    n_gates = 3 * NSA_HEADS
    wt = jnp.concatenate([w_in[:, :NSA_Q], kvcols(2), kvcols(4), kvcols(3), kvcols(5),
                          w_in[:, g0:g0 + n_gates], jnp.zeros((d, 32 - n_gates), F32)], axis=1).T.astype(BF16)
    zpad = jnp.zeros((d, LANES - hd), F32)
    cmp_cols = []
    for i in (0, 1):
        for g in range(NSA_KV_HEADS):
            cmp_cols += [w_in[:, kv0 + i * 2 * hd + g * hd:kv0 + i * 2 * hd + (g + 1) * hd], zpad]
    ws = jnp.concatenate(cmp_cols + [w_in[:, g0 + n_gates:]], axis=1).astype(BF16)
    half = hd // 2
    tok = lambda i: (i, 0)
    tok_t = lambda i: (0, i)
    return pl.pallas_call(
        functools.partial(_even_proj_kernel, per_b=per_b),
        grid=(t // tm,),
        in_specs=[pl.BlockSpec((tm, d), tok),
                  pl.BlockSpec((EVEN_T_ROWS, d), lambda i: (0, 0)),
                  pl.BlockSpec((d, EVEN_S_COLS), lambda i: (0, 0)),
                  pl.BlockSpec((half, tm), tok_t),
                  pl.BlockSpec((half, tm), tok_t),
                  pl.BlockSpec((SGU_GROUPS, SGU_CH), lambda i: (0, 0)),
                  pl.BlockSpec((SGU_GROUPS, SGU_CH), lambda i: (0, 0)),
                  pl.BlockSpec((SGU_GROUPS, SGU_CHUNK, SGU_CHUNK), lambda i: (0, 0, 0)),
                  pl.BlockSpec((SGU_CHUNK, SGU_GROUPS), lambda i: (0, 0))],
        out_specs=[pl.BlockSpec((NSA_Q, tm), tok_t),
                   pl.BlockSpec((NSA_Q, tm), tok_t),
                   pl.BlockSpec((tm, 2 * LANES), tok),
                   pl.BlockSpec((tm, LANES), tok),
                   pl.BlockSpec((tm // NSA_KT, LANES, NSA_KT), lambda i: (i, 0, 0)),
                   pl.BlockSpec((tm // LANES, LANES, LANES), lambda i: (i, 0, 0)),
                   pl.BlockSpec((32, tm), tok_t),
                   pl.BlockSpec((4, tm, hd), lambda i: (0, i, 0)),
                   pl.BlockSpec((tm, SGU_W), tok)],
        out_shape=[jax.ShapeDtypeStruct((NSA_Q, t), BF16),
                   jax.ShapeDtypeStruct((NSA_Q, t), BF16),
                   jax.ShapeDtypeStruct((t, 2 * LANES), BF16),
                   jax.ShapeDtypeStruct((t, LANES), BF16),
                   jax.ShapeDtypeStruct((t // NSA_KT, LANES, NSA_KT), BF16),
                   jax.ShapeDtypeStruct((t // LANES, LANES, LANES), BF16),
                   jax.ShapeDtypeStruct((32, t), F32),
                   jax.ShapeDtypeStruct((4, t, hd), F32),
                   jax.ShapeDtypeStruct((t, SGU_W), BF16)],
        compiler_params=_params("parallel"),
        name="even_proj",
    )(x, wt, ws, cos_t, sin_t, sgu_ln_g, sgu_ln_b, sgu_w, sgu_b.T)


def _compress_kernel(kvc_ref, pos_ref, w1_ref, w2_ref, kc_ref, vct_ref):
    ncp = kc_ref.shape[1]
    assert CMP_BLOCK == 2 * CMP_STRIDE
    for i in range(2):
        outs = []
        for g in range(NSA_KV_HEADS):
            first = jnp.zeros((ncp, CMP_HIDDEN), F32)
            second = jnp.zeros((ncp, CMP_HIDDEN), F32)
            for l in range(CMP_STRIDE):
                rows = kvc_ref[i * NSA_KV_HEADS + g, pl.ds(l, ncp, stride=CMP_STRIDE), :]
                first = first + _dot((rows + pos_ref[i, l:l + 1, :]).astype(BF16), w1_ref[i, l])
                l2 = l + CMP_STRIDE
                second = second + _dot((rows + pos_ref[i, l2:l2 + 1, :]).astype(BF16), w1_ref[i, l2])
            hid = _gelu(first + pltpu.roll(second, ncp - 1, 0))
            outs.append(_dot(hid.astype(BF16), w2_ref[i]))
        if i == 0:
            kc_ref[0] = jnp.concatenate(outs, axis=1).astype(BF16)
        else:
            vct_ref[0] = jnp.concatenate(outs, axis=1).T.astype(BF16)


def nsa_compress(kvc, cmp_pos, cmp_w1, cmp_w2, batch, seq):
    hd = HEAD_DIM
    ncp = seq // CMP_STRIDE
    w1 = cmp_w1.reshape(2, CMP_BLOCK, hd, CMP_HIDDEN).astype(BF16)
    return pl.pallas_call(
        _compress_kernel,
        grid=(batch,),
        in_specs=[pl.BlockSpec((4, seq, hd), lambda b: (0, b, 0)),
                  pl.BlockSpec((2, CMP_BLOCK, hd), lambda b: (0, 0, 0)),
                  pl.BlockSpec((2, CMP_BLOCK, hd, CMP_HIDDEN), lambda b: (0, 0, 0, 0)),
                  pl.BlockSpec((2, CMP_HIDDEN, hd), lambda b: (0, 0, 0))],
        out_specs=[pl.BlockSpec((1, ncp, 2 * hd), lambda b: (b, 0, 0)),
                   pl.BlockSpec((1, 2 * hd, ncp), lambda b: (b, 0, 0))],
        out_shape=[jax.ShapeDtypeStruct((batch, ncp, 2 * hd), BF16),
                   jax.ShapeDtypeStruct((batch, 2 * hd, ncp), BF16)],
        compiler_params=_params("parallel"),
        name="nsa_compress",
    )(kvc, cmp_pos, w1, cmp_w2.astype(BF16))


def _softmax2_cols(s, mask):
    sm = jnp.where(mask, s, NEG_INF)
    m = jnp.max(sm, axis=0, keepdims=True)
    e = jnp.exp2(sm - m)
    return jnp.where(mask, e / jnp.sum(e, axis=0, keepdims=True), 0.0)


def _nsa_kernel(qt_ref, qrt_ref, gt_ref, kc_ref, vct_ref, mct_ref, ks_ref, kw_ref, vs_ref, vw_ref, o_ref,
                s_scr, q_scr, m_scr, l_scr, acc_scr, oc_scr, ow_scr, *, n_top):
    tq = qt_ref.shape[1]
    hg = NSA_GROUP
    hd = HEAD_DIM
    groups = range(NSA_KV_HEADS)
    w = hg * tq
    ncp = kc_ref.shape[1]
    n_slc = mct_ref.shape[0]
    s0 = pl.program_id(1) * tq
    t_lane = s0 + lax.rem(lax.broadcasted_iota(I32, (1, w), 1), tq)
    t_q = s0 + lax.broadcasted_iota(I32, (1, tq), 1)
    zeros_g = jnp.zeros((hd, w), BF16)
    n_win = WINDOW // LANES + tq // LANES
    wt0 = jnp.maximum(s0 // LANES - WINDOW // LANES, 0)
    vrows = [slice(g * hd, (g + 1) * hd) for g in groups]

    def grp(ref, g):
        rows = jnp.concatenate([ref[(g * hg + h) * hd:(g * hg + h + 1) * hd, :] for h in range(hg)], axis=1)
        return jnp.concatenate([rows, zeros_g] if g == 0 else [zeros_g, rows], axis=0)

    q_rot = [grp(qrt_ref, g) for g in groups]
    kw0 = pl.multiple_of(wt0 * LANES, LANES)
    k_win = kw_ref[pl.ds(kw0, n_win * LANES), :]
    s_cmp = [_dot(kc_ref[0], grp(qt_ref, g)) for g in groups]
    s_win = [_dot(k_win, q_rot[g]) for g in groups]

    c_end = lax.broadcasted_iota(I32, (ncp, w), 0) * CMP_STRIDE + (CMP_BLOCK - 1)
    p_cmp = [_softmax2_cols(s_cmp[g], c_end <= t_lane) for g in groups]
    imp = []
    for g in groups:
        oc_scr[g] = _dot(vct_ref[0, vrows[g], :], p_cmp[g].astype(BF16))
        psum = p_cmp[g][:, 0:tq]
        for h in range(1, hg):
            psum = psum + p_cmp[g][:, h * tq:(h + 1) * tq]
        imp.append(_dot(mct_ref[...], psum.astype(BF16)))

    dist = t_lane - (kw0 + lax.broadcasted_iota(I32, (n_win * LANES, w), 0))
    in_window = (dist >= 0) & (dist < WINDOW)
    for g in groups:
        sm = jnp.where(in_window, s_win[g], NEG_INF)
        e = jnp.exp2(sm - jnp.max(sm, axis=0, keepdims=True))
        den = jnp.sum(e, axis=0, keepdims=True)
        eb = e.astype(BF16)
        ow = _dot(vw_ref[wt0, vrows[g], :], eb[0:LANES])
        for j in range(1, n_win):
            ow = ow + _dot(vw_ref[wt0 + j, vrows[g], :], eb[j * LANES:(j + 1) * LANES])
        ow_scr[g] = ow / den

    j_blk = lax.broadcasted_iota(I32, (n_slc, tq), 0)
    cur = t_q // SLC_BLOCK
    forced = (j_blk == 0) | (j_blk == cur) | (j_blk == cur - 1)
    valid = j_blk <= cur
    score = [jnp.where(forced, SLC_FORCED_SCORE, jnp.where(valid, imp[g], -1.0)) for g in groups]
    nv = n_slc // 8
    sblk = [[score[g][8 * v:8 * v + 8] for v in range(nv)] for g in groups]
    rank = [[jnp.zeros((8, tq), F32) for _ in range(nv)] for g in groups]
    sub = lax.broadcasted_iota(I32, (8, tq), 0)
    for k in range(n_slc):
        kv_ = k // 8
        for g in groups:
            sk = score[g][k:k + 1, :]
            for v in range(nv):
                ge = jnp.where(sk >= sblk[g][v], 1.0, 0.0)
                gt = jnp.where(sk > sblk[g][v], 1.0, 0.0)
                if v > kv_:
                    beats = ge
                elif v < kv_:
                    beats = gt
                else:
                    beats = jnp.where(sub > k - 8 * kv_, ge, gt)
                rank[g][v] = rank[g][v] + beats
    q_aug = []
    for g in groups:
        bias = jnp.where((jnp.concatenate(rank[g], axis=0) < n_top) & valid, 0.0, NEG_INF)
        bias = jnp.concatenate([bias] * hg, axis=1)
        if n_slc < 2 * hd:
            bias = jnp.concatenate([bias, jnp.zeros((2 * hd - n_slc, w), F32)], axis=0)
        q_aug.append(jnp.concatenate([q_rot[g], bias.astype(BF16)], axis=0))

    for g in groups:
        q_scr[g] = q_aug[g]
        m_scr[g] = jnp.full((1, w), NEG_INF, F32)
        l_scr[g] = jnp.zeros((1, w), F32)
        acc_scr[g] = jnp.zeros((hd, w), F32)

    def slc_scores(kt, slot):
        k_tile = ks_ref[pl.ds(pl.multiple_of(kt * NSA_KT, NSA_KT), NSA_KT), :]
        for g in groups:
            s_scr[slot, g] = _dot(k_tile, q_scr[g])

    def slc_update(kt, slot, diagonal):
        for g in groups:
            s = s_scr[slot, g]
            if diagonal:
                kpos = kt * NSA_KT + lax.broadcasted_iota(I32, (NSA_KT, w), 0)
                s = jnp.where(kpos <= t_lane, s, NEG_INF)
            m = m_scr[g]
            m_new = jnp.maximum(m, jnp.max(s, axis=0, keepdims=True))
            alpha = jnp.exp2(m - m_new)
            pp = jnp.exp2(s - m_new)
            l_scr[g] = l_scr[g] * alpha + jnp.sum(pp, axis=0, keepdims=True)
            acc_scr[g] = acc_scr[g] * alpha + _dot(vs_ref[kt, vrows[g], :], pp.astype(BF16))
            m_scr[g] = m_new

    n_full = s0 // NSA_KT

    def slc_pair(p, c):
        a = 2 * p
        slc_scores(a + 1, 1)
        slc_update(a, 0, False)
        slc_scores(a + 2, 0)
        slc_update(a + 1, 1, False)
        return c

    slc_scores(0, 0)
    lax.fori_loop(0, n_full // 2, slc_pair, 0)
    odd = lax.rem(n_full, 2) == 1

    @pl.when(odd)
    def _():
        slc_scores(n_full, 1)
        slc_update(n_full - 1, 0, False)
        slc_update(n_full, 1, True)

    @pl.when(jnp.logical_not(odd))
    def _():
        slc_update(n_full, 0, True)

    o_slc = [acc_scr[g] / l_scr[g] for g in groups]

    heads = []
    for g in groups:
        for h in range(hg):
            r = (g * hg + h) * 3
            cols = slice(h * tq, (h + 1) * tq)
            heads.append(gt_ref[r:r + 1, :] * oc_scr[g, :, cols] + gt_ref[r + 1:r + 2, :] * o_slc[g][:, cols]
                         + gt_ref[r + 2:r + 3, :] * ow_scr[g, :, cols])
    o_ref[...] = jnp.concatenate(heads, axis=0).T.astype(o_ref.dtype)


def _cmp_to_slc_t(seq):
    ncp = seq // CMP_STRIDE
    ns = seq // SLC_BLOCK
    cs = np.arange(ncp)[None, :] * CMP_STRIDE
    ss = np.arange(ns)[:, None] * SLC_BLOCK
    ov = np.clip(np.minimum(cs + CMP_BLOCK, ss + SLC_BLOCK) - np.maximum(cs, ss), 0, None) / CMP_BLOCK
    ov[:, ncp - 1] = 0.0
    return jnp.asarray(ov, BF16)


def nsa_mixer(qt, qrt, gt, kc, vct, ks, kw, vs, vw, batch, seq):
    tq = NSA_TQ
    nq = seq // tq
    ncp = seq // CMP_STRIDE
    n_slc = seq // SLC_BLOCK
    assert n_slc <= 2 * HEAD_DIM and n_slc % 8 == 0 and seq >= WINDOW + tq
    w = NSA_GROUP * tq
    col = lambda b, i: (0, b * nq + i)
    return pl.pallas_call(
        functools.partial(_nsa_kernel, n_top=min(SLC_TOP, n_slc)),
        grid=(batch, nq),
        in_specs=[pl.BlockSpec((NSA_Q, tq), col),
                  pl.BlockSpec((NSA_Q, tq), col),
                  pl.BlockSpec((32, tq), col),
                  pl.BlockSpec((1, ncp, 2 * HEAD_DIM), lambda b, i: (b, 0, 0)),
                  pl.BlockSpec((1, 2 * HEAD_DIM, ncp), lambda b, i: (b, 0, 0)),
                  pl.BlockSpec((n_slc, ncp), lambda b, i: (0, 0)),
                  pl.BlockSpec((seq, 2 * LANES), lambda b, i: (b, 0)),
                  pl.BlockSpec((seq, LANES), lambda b, i: (b, 0)),
                  pl.BlockSpec((seq // NSA_KT, LANES, NSA_KT), lambda b, i: (b, 0, 0)),
                  pl.BlockSpec((seq // LANES, LANES, LANES), lambda b, i: (b, 0, 0))],
        out_specs=pl.BlockSpec((tq, NSA_Q), lambda b, i: (b * nq + i, 0)),
        out_shape=jax.ShapeDtypeStruct((batch * seq, NSA_Q), BF16),
        scratch_shapes=[pltpu.VMEM((2, NSA_KV_HEADS, NSA_KT, w), F32),
                        pltpu.VMEM((NSA_KV_HEADS, 4 * HEAD_DIM, w), BF16),
                        pltpu.VMEM((NSA_KV_HEADS, 1, w), F32),
                        pltpu.VMEM((NSA_KV_HEADS, 1, w), F32),
                        pltpu.VMEM((NSA_KV_HEADS, HEAD_DIM, w), F32),
                        pltpu.VMEM((NSA_KV_HEADS, HEAD_DIM, w), F32),
                        pltpu.VMEM((NSA_KV_HEADS, HEAD_DIM, w), F32)],
        compiler_params=_params("parallel", "parallel"),
        name="nsa_mixer",
    )(qt, qrt, gt, kc, vct, _cmp_to_slc_t(seq), ks, kw, vs, vw)


def kernel(x, mem, positions, w_in_even, nsa_cmp_pos, nsa_cmp_w1, nsa_cmp_w2, sgu_ln_g, sgu_ln_b, sgu_w, sgu_b, w_out_even, w_in_odd, hgrn_lb_logits, hgrn_norm_g, conv_w, conv_b, w_out_odd, xattn_w_q, xattn_w_kv, xattn_w_o, ln_g, ln_b, router_w, router_b, expert_w_gu, expert_b_gu, expert_w_dn, expert_b_dn):
    batch, seq, d = x.shape
    t = batch * seq
    cos_t, sin_t = rope_tables_t(positions)
    mem2 = mem.reshape(-1, d)
    xf = x.reshape(t, d)
    for layer in range(DEPTH):
        j = layer // 2
        if layer % 2 == 0:
            qt, qrt, ks, kw, vs, vw, gt, kvc, o_b = even_proj(xf, w_in_even[j], cos_t, sin_t, sgu_ln_g[j], sgu_ln_b[j],
                                                              sgu_w[j], sgu_b[j], seq)
            kc, vct = nsa_compress(kvc, nsa_cmp_pos[j], nsa_cmp_w1[j], nsa_cmp_w2[j], batch, seq)
            o_a = nsa_mixer(qt, qrt, gt, kc, vct, ks, kw, vs, vw, batch, seq)
            w_out = w_out_even[j]
        else:
            proj, o_b = odd_proj(xf, w_in_odd[j], conv_w[j], conv_b[j], batch, seq)
            o_a = hgrn2_mixer(proj, hgrn_lb_logits, hgrn_norm_g[j], layer, batch, seq)
            w_out = w_out_odd[j]
        xf = outproj_ln(o_a, o_b, xf, w_out, ln_g[layer, 0], ln_b[layer, 0])
        kv = matmul(mem2, xattn_w_kv[layer].astype(BF16), BF16, mem.shape[1]).reshape(batch, mem.shape[1], 2 * d)
        xf = xattn_ln(xf, kv, xattn_w_q[layer], xattn_w_o[layer], ln_g[layer, 1], ln_b[layer, 1], seq)
        xf = moe_ln(xf, router_w[layer], router_b[layer], expert_w_gu, expert_b_gu, expert_w_dn, expert_b_dn,
                    ln_g[layer, 2], ln_b[layer, 2], layer)
    return xf.reshape(batch, seq, d)
```

```python
import functools

import numpy as np
import jax
import jax.numpy as jnp
from jax import lax
from jax.experimental import pallas as pl
from jax.experimental.pallas import tpu as pltpu

F32 = jnp.float32
BF16 = jnp.bfloat16
I32 = jnp.int32

D_MODEL = 1024
DEPTH = 2
HEAD_DIM = 64
NSA_HEADS = 8
NSA_KV_HEADS = 2
NSA_GROUP = NSA_HEADS // NSA_KV_HEADS
CMP_BLOCK = 32
CMP_STRIDE = 16
CMP_HIDDEN = 256
SLC_BLOCK = 64
SLC_TOP = 16
WINDOW = 512
SLC_FORCED_SCORE = 1e4
SGU_GROUPS = 4
SGU_CH = 128
SGU_CHUNK = 128
HGRN_HEADS = 4
HGRN_DK = 128
HGRN_CHUNK = 64
CONV_CH = 512
XATTN_HEADS = 4
XATTN_DIM = D_MODEL // XATTN_HEADS
N_EXPERTS = 32
TOP_K = 4
D_EXPERT = D_MODEL
SWIGLU_LIMIT = 7.0
SWIGLU_ALPHA = 1.702
ROPE_THETA = 10000.0
LN_EPS = 1e-5
RMS_EPS = 1e-6
NEG_INF = -1e30
DEEPNORM_ALPHA = (2 * DEPTH) ** 0.25
LOG2_E = 1.4426950408889634

NSA_Q = NSA_HEADS * HEAD_DIM
SGU_W = SGU_GROUPS * SGU_CH
HGRN_W = HGRN_HEADS * HGRN_DK

VMEM_LIMIT_BYTES = 56 * 1024 * 1024
LANES = 128

PROJ_TM = 1024
XATTN_ROW_GROUPS = 2
OUTPROJ_ROW_GROUPS = 4
EVEN_ROW_GROUPS = 2
NSA_TQ = 256
NSA_KT = 256
HGRN_STEP = 32
HGRN_SEG = 1024
MOE_TM = 512
ROUTER_TILES = 4
MOE_BM = 512
SEG_ALIGN = 8
MOE_RT = TOP_K * MOE_TM + SEG_ALIGN * N_EXPERTS
MOE_CHUNK = 256


def _params(*sem):
    return pltpu.CompilerParams(dimension_semantics=sem, vmem_limit_bytes=VMEM_LIMIT_BYTES)


def _dot(a, b):
    return jnp.dot(a, b, preferred_element_type=F32)


def _dot_nt(a, b):
    return lax.dot_general(a, b, (((1,), (1,)), ((), ())), preferred_element_type=F32)


def _dot_tn(a, b):
    return lax.dot_general(a, b, (((0,), (0,)), ((), ())), preferred_element_type=F32)


def _gelu(x):
    return 0.5 * x * (1.0 + jnp.tanh(np.sqrt(2.0 / np.pi).astype(np.float32) * (x + 0.044715 * (x * x * x))))


def _sigmoid(x):
    return 1.0 / (1.0 + jnp.exp(-x))


def _layer_norm(y, g, b):
    mu = jnp.mean(y, axis=-1, keepdims=True)
    d = y - mu
    var = jnp.mean(d * d, axis=-1, keepdims=True)
    return d * lax.rsqrt(var + LN_EPS) * g + b


def _rope_kernel(pos_ref, inv_ref, cos_ref, sin_ref):
    ang = pos_ref[...].astype(F32) * inv_ref[...]
    cos_ref[...] = jnp.cos(ang)
    sin_ref[...] = jnp.sin(ang)


def rope_tables_t(positions):
    t = positions.size
    inv = 1.0 / (ROPE_THETA ** (jnp.arange(0, HEAD_DIM, 2, dtype=F32) / HEAD_DIM))
    tn = min(t, 4096)
    half = HEAD_DIM // 2
    return pl.pallas_call(
        _rope_kernel,
        grid=(t // tn,),
        in_specs=[pl.BlockSpec((1, tn), lambda i: (0, i)),
                  pl.BlockSpec((half, 1), lambda i: (0, 0))],
        out_specs=[pl.BlockSpec((half, tn), lambda i: (0, i))] * 2,
        out_shape=[jax.ShapeDtypeStruct((half, t), F32)] * 2,
        compiler_params=_params("parallel"),
        name="rope_tables",
    )(positions.reshape(1, t), inv.reshape(half, 1))


def _mm_kernel(x_ref, w_ref, o_ref):
    o_ref[...] = _dot(x_ref[...].astype(BF16), w_ref[...]).astype(o_ref.dtype)


def matmul(x, w, out_dtype, tm):
    m, k = x.shape
    n = w.shape[1]
    return pl.pallas_call(
        _mm_kernel,
        grid=(m // tm,),
        in_specs=[pl.BlockSpec((tm, k), lambda i: (i, 0)),
                  pl.BlockSpec((k, n), lambda i: (0, 0))],
        out_specs=pl.BlockSpec((tm, n), lambda i: (i, 0)),
        out_shape=jax.ShapeDtypeStruct((m, n), out_dtype),
        compiler_params=_params("parallel"),
        name="matmul",
    )(x, w)


def _outproj_ln_kernel(a_ref, b_ref, x_ref, wa_ref, wb_ref, g_ref, beta_ref, o_ref):
    rows = a_ref.shape[0] // OUTPROJ_ROW_GROUPS
    part = [slice(r * rows, (r + 1) * rows) for r in range(OUTPROJ_ROW_GROUPS)]
    mix = [_dot(a_ref[p, :].astype(BF16), wa_ref[...]) + _dot(b_ref[p, :].astype(BF16), wb_ref[...]) for p in part]
    for p, m in zip(part, mix):
        o_ref[p, :] = _layer_norm(DEEPNORM_ALPHA * x_ref[p, :] + m, g_ref[...], beta_ref[...])


def outproj_ln(a, b, x, w_out, g, beta):
    t, d = x.shape
    na, nb = a.shape[1], b.shape[1]
    tm = min(PROJ_TM, t)
    wa = w_out[:na].astype(BF16)
    wb = w_out[na:].astype(BF16)
    return pl.pallas_call(
        _outproj_ln_kernel,
        grid=(t // tm,),
        in_specs=[pl.BlockSpec((tm, na), lambda i: (i, 0)),
                  pl.BlockSpec((tm, nb), lambda i: (i, 0)),
                  pl.BlockSpec((tm, d), lambda i: (i, 0)),
                  pl.BlockSpec((na, d), lambda i: (0, 0)),
                  pl.BlockSpec((nb, d), lambda i: (0, 0)),
                  pl.BlockSpec((1, d), lambda i: (0, 0)),
                  pl.BlockSpec((1, d), lambda i: (0, 0))],
        out_specs=pl.BlockSpec((tm, d), lambda i: (i, 0)),
        out_shape=jax.ShapeDtypeStruct((t, d), F32),
        compiler_params=_params("parallel"),
        name="outproj_ln",
    )(a, b, x, wa, wb, g.reshape(1, d), beta.reshape(1, d))


def _xattn_kernel(x_ref, wq_ref, kv_ref, wo_ref, g_ref, beta_ref, o_ref):
    hw = XATTN_HEADS * XATTN_DIM
    rows = x_ref.shape[0] // XATTN_ROW_GROUPS
    groups = range(XATTN_ROW_GROUPS)
    heads = range(XATTN_HEADS)
    x = [x_ref[r * rows:(r + 1) * rows, :] for r in groups]
    q = [_dot(x[r].astype(BF16), wq_ref[...]) for r in groups]
    s = [[_dot_nt(q[r][:, h * XATTN_DIM:(h + 1) * XATTN_DIM].astype(BF16),
                  kv_ref[0, :, h * XATTN_DIM:(h + 1) * XATTN_DIM]) * (XATTN_DIM ** -0.5) for h in heads] for r in groups]
    o = []
    for r in groups:
        parts = []
        for h in heads:
            m = jnp.max(s[r][h], axis=-1, keepdims=True)
            e = jnp.exp(s[r][h] - m)
            p = e / jnp.sum(e, axis=-1, keepdims=True)
            parts.append(_dot(p.astype(BF16), kv_ref[0, :, hw + h * XATTN_DIM:hw + (h + 1) * XATTN_DIM]))
        o.append(jnp.concatenate(parts, axis=-1))
    xa = [_dot(o[r].astype(BF16), wo_ref[...]) for r in groups]
    for r in groups:
        o_ref[r * rows:(r + 1) * rows, :] = _layer_norm(DEEPNORM_ALPHA * x[r] + xa[r], g_ref[...], beta_ref[...])


def xattn_ln(x, kv, w_q, w_o, g, beta, seq):
    t, d = x.shape
    tm = min(PROJ_TM, seq)
    per_b = seq // tm
    mlen = kv.shape[1]
    return pl.pallas_call(
        _xattn_kernel,
        grid=(t // tm,),
        in_specs=[pl.BlockSpec((tm, d), lambda i: (i, 0)),
                  pl.BlockSpec((d, d), lambda i: (0, 0)),
                  pl.BlockSpec((1, mlen, 2 * d), lambda i: (i // per_b, 0, 0)),
                  pl.BlockSpec((d, d), lambda i: (0, 0)),
                  pl.BlockSpec((1, d), lambda i: (0, 0)),
                  pl.BlockSpec((1, d), lambda i: (0, 0))],
        out_specs=pl.BlockSpec((tm, d), lambda i: (i, 0)),
        out_shape=jax.ShapeDtypeStruct((t, d), F32),
        compiler_params=_params("parallel"),
        name="xattn_ln",
    )(x, w_q.astype(BF16), kv, w_o.astype(BF16), g.reshape(1, d), beta.reshape(1, d))


def _router_kernel(x_ref, wt_ref, b_ref, tri_ref, ltri_ref, row_ref, gate_ref, off_ref, cnt_ref, base_ref,
                   carry_ref):
    i = pl.program_id(0)

    @pl.when(i == 0)
    def _():
        carry_ref[...] = jnp.zeros_like(carry_ref)

    tm = MOE_TM
    tiles = range(x_ref.shape[0] // tm)
    cols = [slice(u * tm, (u + 1) * tm) for u in tiles]
    e_iota = lax.broadcasted_iota(I32, (N_EXPERTS, tm), 0)
    work = [_dot_nt(wt_ref[...], x_ref[cols[u], :].astype(BF16)) + b_ref[...] for u in tiles]
    vals, hots = [[] for _ in tiles], [[] for _ in tiles]
    for _ in range(TOP_K):
        for u in tiles:
            m = jnp.max(work[u], axis=0, keepdims=True)
            idx = jnp.min(jnp.where(work[u] == m, e_iota, N_EXPERTS), axis=0, keepdims=True)
            hot = e_iota == idx
            vals[u].append(m)
            hots[u].append(hot)
            work[u] = jnp.where(hot, -jnp.inf, work[u])
    hot_all, seg, off = [], [], []
    for u in tiles:
        exps = [jnp.exp(v - vals[u][0]) for v in vals[u]]
        den = exps[0] + exps[1] + exps[2] + exps[3]
        gate_ref[:, cols[u]] = jnp.concatenate([e / den for e in exps], axis=0)
        ha = jnp.zeros((N_EXPERTS, tm), F32)
        for hot in hots[u]:
            ha = ha + jnp.where(hot, 1.0, 0.0)
        hot_all.append(ha)
        n = jnp.sum(ha, axis=1, keepdims=True)
        sg = jnp.floor((n + (SEG_ALIGN - 1)) * (1.0 / SEG_ALIGN))
        seg.append(jnp.broadcast_to(sg, (N_EXPERTS, LANES)))
    rank = [_dot(hot_all[u].astype(BF16), tri_ref[...]) for u in tiles]
    off = [_dot(ltri_ref[...], seg[u].astype(BF16)) for u in tiles]
    carry = carry_ref[...]
    for u in tiles:
        where_row = off[u][:, 0:1] * SEG_ALIGN + rank[u]
        row_ref[:, cols[u]] = jnp.concatenate(
            [jnp.sum(jnp.where(hot, where_row, 0.0), axis=0, keepdims=True) for hot in hots[u]], axis=0).astype(I32)
        off_ref[u] = off[u] * SEG_ALIGN
        cnt_ref[u] = seg[u] * SEG_ALIGN
        base_ref[u] = carry
        carry = carry + seg[u] * SEG_ALIGN
    carry_ref[...] = carry


def moe_route(x, w_router, b_router):
    t, d = x.shape
    tm = MOE_TM
    nt = t // tm
    tri = jnp.asarray(np.triu(np.ones((tm, tm), np.float32), 1), BF16)
    ltri = jnp.asarray(np.tril(np.ones((N_EXPERTS, N_EXPERTS), np.float32), -1), BF16)
    per = ROUTER_TILES if nt % ROUTER_TILES == 0 else 1
    tab = pl.BlockSpec((per, N_EXPERTS, LANES), lambda i: (i, 0, 0))
    tab_shape = jax.ShapeDtypeStruct((nt, N_EXPERTS, LANES), F32)
    return pl.pallas_call(
        _router_kernel,
        grid=(nt // per,),
        in_specs=[pl.BlockSpec((per * tm, d), lambda i: (i, 0)),
                  pl.BlockSpec((N_EXPERTS, d), lambda i: (0, 0)),
                  pl.BlockSpec((N_EXPERTS, 1), lambda i: (0, 0)),
                  pl.BlockSpec((tm, tm), lambda i: (0, 0)),
                  pl.BlockSpec((N_EXPERTS, N_EXPERTS), lambda i: (0, 0))],
        out_specs=[pl.BlockSpec((TOP_K, per * tm), lambda i: (0, i)),
                   pl.BlockSpec((TOP_K, per * tm), lambda i: (0, i)),
                   tab, tab, tab],
        out_shape=[jax.ShapeDtypeStruct((TOP_K, t), I32),
                   jax.ShapeDtypeStruct((TOP_K, t), F32),
                   tab_shape, tab_shape, tab_shape],
        scratch_shapes=[pltpu.VMEM((N_EXPERTS, LANES), F32)],
        compiler_params=_params("arbitrary"),
        name="moe_router",
    )(x, w_router.T.astype(BF16), b_router.reshape(N_EXPERTS, 1), tri, ltri)


def _segment_copies(off_ref, cnt_ref, dst_ref, make_copy, wait):
    for e in range(N_EXPERTS):
        n = pl.multiple_of(cnt_ref[0, 0, e], SEG_ALIGN)

        @pl.when(n > 0)
        def _():
            cp = make_copy(pl.multiple_of(off_ref[0, 0, e], SEG_ALIGN), pl.multiple_of(dst_ref[0, 0, e], SEG_ALIGN), n)
            if wait:
                cp.wait()
            else:
                cp.start()


def _dispatch_kernel(off_ref, cnt_ref, dst_ref, poff_ref, pcnt_ref, pdst_ref, zero_ref, tail_cnt_ref, tail_dst_ref,
                     rest_ref, x_ref, row_ref, xs_hbm, buf, sem):
    i = pl.program_id(0)
    slot = lax.rem(i, 2)
    tm = x_ref.shape[0]
    xb = x_ref[...].astype(BF16)
    rows = [row_ref[k:k + 1, :] for k in range(TOP_K)]
    for c in range(MOE_RT // MOE_CHUNK):
        rr = c * MOE_CHUNK + lax.broadcasted_iota(I32, (MOE_CHUNK, tm), 0)
        perm = jnp.where(rr == rows[0], 1.0, 0.0)
        for k in range(1, TOP_K):
            perm = perm + jnp.where(rr == rows[k], 1.0, 0.0)
        buf[slot, c * MOE_CHUNK:(c + 1) * MOE_CHUNK, :] = _dot(perm.astype(BF16), xb)

    def copy_from(s):
        def copy(src_row, dst_row, n):
            return pltpu.make_async_copy(buf.at[s, pl.ds(src_row, n)], xs_hbm.at[pl.ds(dst_row, n)], sem.at[s])
        return copy

    @pl.when(i > 0)
    def _():
        _segment_copies(poff_ref, pcnt_ref, pdst_ref, copy_from(1 - slot), wait=True)

    _segment_copies(off_ref, cnt_ref, dst_ref, copy_from(slot), wait=False)

    @pl.when(i == pl.num_programs(0) - 1)
    def _():
        _segment_copies(off_ref, cnt_ref, dst_ref, copy_from(slot), wait=True)
        buf[0, 0:MOE_BM, :] = jnp.zeros((MOE_BM, buf.shape[2]), F32)
        zeros = copy_from(0)
        for wait in (False, True):
            _segment_copies(zero_ref, tail_cnt_ref, tail_dst_ref, zeros, wait=wait)

            def rest(j, c):
                cp = zeros(0, pl.multiple_of(rest_ref[0] + j * MOE_BM, MOE_BM), MOE_BM)
                cp.wait() if wait else cp.start()
                return c
            lax.fori_loop(0, rest_ref[1], rest, 0)


def _seg_spec(n_tiles, shift=0):
    def index(i):
        return (jnp.clip(i + shift, 0, n_tiles - 1), 0, 0)
    return pl.BlockSpec((1, 1, N_EXPERTS), index, memory_space=pltpu.SMEM)


def _smem_whole(shape):
    return pl.BlockSpec(shape, lambda i: (0,) * len(shape), memory_space=pltpu.SMEM)


def moe_dispatch(x, row, seg_off, seg_cnt, seg_dst, tail_cnt, tail_dst, rest, n_rows):
    t, d = x.shape
    tm = MOE_TM
    nt = t // tm
    tab = (1, 1, N_EXPERTS)
    return pl.pallas_call(
        _dispatch_kernel,
        grid=(nt,),
        in_specs=[_seg_spec(nt), _seg_spec(nt), _seg_spec(nt),
                  _seg_spec(nt, -1), _seg_spec(nt, -1), _seg_spec(nt, -1),
                  _smem_whole(tab), _smem_whole(tab), _smem_whole(tab), _smem_whole((2,)),
                  pl.BlockSpec((tm, d), lambda i: (i, 0)),
                  pl.BlockSpec((TOP_K, tm), lambda i: (0, i))],
        out_specs=pl.BlockSpec(memory_space=pl.ANY),
        out_shape=jax.ShapeDtypeStruct((n_rows, d), F32),
        scratch_shapes=[pltpu.VMEM((2, MOE_RT, d), F32), pltpu.SemaphoreType.DMA((2,))],
        compiler_params=_params("arbitrary"),
        name="moe_dispatch",
    )(seg_off, seg_cnt, seg_dst, seg_off, seg_cnt, seg_dst, jnp.zeros(tab, I32), tail_cnt.reshape(tab),
      tail_dst.reshape(tab), rest, x, row)


def _expert_kernel(blk_e_ref, nused_ref, xs_ref, wgu_ref, bgu_ref, wdn_ref, bdn_ref, ys_ref, wgu_bf, wdn_bf):
    i = pl.program_id(0)
    used = i < nused_ref[0]
    new_expert = (i == 0) | (blk_e_ref[i] != blk_e_ref[jnp.maximum(i - 1, 0)])

    @pl.when(used & new_expert)
    def _():
        rows = 128
        for r in range(0, wgu_bf.shape[0], rows):
            wgu_bf[r:r + rows, :] = wgu_ref[0, 0, r:r + rows, :].astype(BF16)
        for r in range(0, wdn_bf.shape[0], rows):
            wdn_bf[r:r + rows, :] = wdn_ref[0, 0, r:r + rows, :].astype(BF16)

    @pl.when(used)
    def _():
        h = _dot(xs_ref[...].astype(BF16), wgu_bf[...]) + bgu_ref[0, 0]
        h_gate = jnp.minimum(h[:, :D_EXPERT], SWIGLU_LIMIT)
        h_up = jnp.clip(h[:, D_EXPERT:], -SWIGLU_LIMIT, SWIGLU_LIMIT)
        act = (h_up + 1.0) * (h_gate * _sigmoid(h_gate * SWIGLU_ALPHA))
        ys_ref[...] = _dot(act.astype(BF16), wdn_bf[...]) + bdn_ref[0, 0]

    @pl.when(jnp.logical_not(used))
    def _():
        ys_ref[...] = jnp.zeros_like(ys_ref)


def moe_experts(xs, blk_e, nused, w_gu, b_gu, w_dn, b_dn, layer):
    n_rows, d = xs.shape
    bm = MOE_BM
    nb = n_rows // bm
    wsel = lambda i, be, nu: (layer, be[i], 0, 0)
    grid_spec = pltpu.PrefetchScalarGridSpec(
        num_scalar_prefetch=2,
        grid=(nb,),
        in_specs=[pl.BlockSpec((bm, d), lambda i, be, nu: (jnp.minimum(i, nu[0] - 1), 0)),
                  pl.BlockSpec((1, 1, d, 2 * D_EXPERT), wsel),
                  pl.BlockSpec((1, 1, 1, 2 * D_EXPERT), wsel),
                  pl.BlockSpec((1, 1, D_EXPERT, d), wsel),
                  pl.BlockSpec((1, 1, 1, d), wsel)],
        out_specs=pl.BlockSpec((bm, d), lambda i, be, nu: (i, 0)),
        scratch_shapes=[pltpu.VMEM((d, 2 * D_EXPERT), BF16), pltpu.VMEM((D_EXPERT, d), BF16)],
    )
    return pl.pallas_call(
        _expert_kernel,
        grid_spec=grid_spec,
        out_shape=jax.ShapeDtypeStruct((n_rows, d), F32),
        compiler_params=_params("arbitrary"),
        name="moe_experts",
    )(blk_e, nused, xs, w_gu, b_gu.reshape(DEPTH, N_EXPERTS, 1, -1), w_dn, b_dn.reshape(DEPTH, N_EXPERTS, 1, -1))


def _combine_ln_kernel(off_ref, cnt_ref, dst_ref, noff_ref, ncnt_ref, ndst_ref, ys_hbm, row_ref, gate_ref, x_ref,
                       g_ref, beta_ref, o_ref, buf, sem):
    i = pl.program_id(0)
    slot = lax.rem(i, 2)
    tm = x_ref.shape[0]

    def copy_into(s):
        def copy(buf_row, ys_row, n):
            return pltpu.make_async_copy(ys_hbm.at[pl.ds(ys_row, n)], buf.at[s, pl.ds(buf_row, n)], sem.at[s])
        return copy

    @pl.when(i == 0)
    def _():
        buf[...] = jnp.zeros_like(buf)
        _segment_copies(off_ref, cnt_ref, dst_ref, copy_into(0), wait=False)

    @pl.when(i + 1 < pl.num_programs(0))
    def _():
        _segment_copies(noff_ref, ncnt_ref, ndst_ref, copy_into(1 - slot), wait=False)

    rows = [jnp.broadcast_to(row_ref[:, k:k + 1], (tm, MOE_CHUNK)) for k in range(TOP_K)]
    gates = [jnp.broadcast_to(gate_ref[:, k:k + 1], (tm, MOE_CHUNK)) for k in range(TOP_K)]
    lane = lax.broadcasted_iota(I32, (tm, MOE_CHUNK), 1)
    _segment_copies(off_ref, cnt_ref, dst_ref, copy_into(slot), wait=True)
    ff = jnp.zeros((tm, x_ref.shape[1]), F32)
    for c in range(MOE_RT // MOE_CHUNK):
        rr = lane + c * MOE_CHUNK
        mix = jnp.where(rr == rows[0], gates[0], 0.0)
        for k in range(1, TOP_K):
            mix = mix + jnp.where(rr == rows[k], gates[k], 0.0)
        ff = ff + _dot(mix.astype(BF16), buf[slot, c * MOE_CHUNK:(c + 1) * MOE_CHUNK, :].astype(BF16))
    o_ref[...] = _layer_norm(DEEPNORM_ALPHA * x_ref[...] + ff, g_ref[...], beta_ref[...])


def moe_combine_ln(ys, row_t, gate_t, seg_off, seg_cnt, seg_dst, x, g, beta):
    t, d = x.shape
    tm = MOE_TM
    nt = t // tm
    return pl.pallas_call(
        _combine_ln_kernel,
        grid=(nt,),
        in_specs=[_seg_spec(nt), _seg_spec(nt), _seg_spec(nt),
                  _seg_spec(nt, 1), _seg_spec(nt, 1), _seg_spec(nt, 1),
                  pl.BlockSpec(memory_space=pl.ANY),
                  pl.BlockSpec((tm, TOP_K), lambda i: (i, 0)),
                  pl.BlockSpec((tm, TOP_K), lambda i: (i, 0)),
                  pl.BlockSpec((tm, d), lambda i: (i, 0)),
                  pl.BlockSpec((1, d), lambda i: (0, 0)),
                  pl.BlockSpec((1, d), lambda i: (0, 0))],
        out_specs=pl.BlockSpec((tm, d), lambda i: (i, 0)),
        out_shape=jax.ShapeDtypeStruct((t, d), F32),
        scratch_shapes=[pltpu.VMEM((2, MOE_RT, d), F32), pltpu.SemaphoreType.DMA((2,))],
        compiler_params=_params("arbitrary"),
        name="moe_combine_ln",
    )(seg_off, seg_cnt, seg_dst, seg_off, seg_cnt, seg_dst, ys, row_t, gate_t, x, g.reshape(1, d),
      beta.reshape(1, d))


def moe_ln(x, w_router, b_router, w_gu, b_gu, w_dn, b_dn, g, beta, layer):
    t, d = x.shape
    bm = MOE_BM
    nt = t // MOE_TM
    row, gate, off, cnt, base = moe_route(x, w_router, b_router)
    seg_off = off[:, :, 0].astype(I32)
    seg_cnt = cnt[:, :, 0].astype(I32)
    seg_base = base[:, :, 0].astype(I32)
    total = seg_base[-1] + seg_cnt[-1]
    padded = (total + bm - 1) // bm * bm
    pend = jnp.cumsum(padded)
    pstart = pend - padded
    n_rows = (t * TOP_K + SEG_ALIGN * N_EXPERTS * nt) // bm * bm + N_EXPERTS * bm
    nb = n_rows // bm
    blk_row = jnp.arange(nb, dtype=I32) * bm
    blk_e = jnp.minimum(jnp.sum((pend[None, :] <= blk_row[:, None]).astype(I32), axis=1), N_EXPERTS - 1)
    nused = (pend[-1] // bm).astype(I32).reshape(1)
    seg_dst = (pstart[None, :] + seg_base).reshape(nt, 1, N_EXPERTS)
    seg_off = seg_off.reshape(nt, 1, N_EXPERTS)
    seg_cnt = seg_cnt.reshape(nt, 1, N_EXPERTS)
    rest = jnp.stack([pend[-1], nb - nused[0]]).astype(I32)
    xs = moe_dispatch(x, row, seg_off, seg_cnt, seg_dst, padded - total, pstart + total, rest, n_rows)
    ys = moe_experts(xs, blk_e, nused, w_gu, b_gu, w_dn, b_dn, layer)
    return moe_combine_ln(ys, row.T, gate.T, seg_off, seg_cnt, seg_dst, x, g, beta)


def _sgu_tile(u, v, lng_ref, lnb_ref, w_ref, bs_ref, o_ref, row0):
    tm = u.shape[0]
    row = lax.broadcasted_iota(I32, (SGU_CHUNK, SGU_CHUNK), 0)
    col = lax.broadcasted_iota(I32, (SGU_CHUNK, SGU_CHUNK), 1)
    causal = row >= col
    for g in range(SGU_GROUPS):
        lo = g * SGU_CH
        vg = _gelu(v[:, lo:lo + SGU_CH])
        vg = _layer_norm(vg, lng_ref[g:g + 1, :], lnb_ref[g:g + 1, :]).astype(BF16)
        wg = jnp.where(causal, w_ref[g], 0.0).astype(BF16)
        bias = bs_ref[:, g:g + 1]
        for n in range(tm // SGU_CHUNK):
            r0 = n * SGU_CHUNK
            mix = _dot(wg, vg[r0:r0 + SGU_CHUNK]) + bias
            o_ref[row0 + r0:row0 + r0 + SGU_CHUNK, lo:lo + SGU_CH] = (
                _gelu(u[r0:r0 + SGU_CHUNK, lo:lo + SGU_CH]) * mix).astype(o_ref.dtype)


def _odd_proj_kernel(x_ref, w_ref, cw_ref, cb_ref, hg_ref, od_ref, carry_ref):
    @pl.when(pl.program_id(1) == 0)
    def _():
        carry_ref[...] = jnp.zeros_like(carry_ref)

    pr = _dot(x_ref[...].astype(BF16), w_ref[...])
    n_h = 4 * HGRN_W
    hg_ref[...] = pr[:, :n_h]
    h, bg, cg = (pr[:, n_h + j * CONV_CH:n_h + (j + 1) * CONV_CH] for j in range(3))
    z = cg * h
    tm = z.shape[0]
    row = lax.broadcasted_iota(I32, z.shape, 0)
    prev = carry_ref[...]
    z1 = jnp.where(row == 0, prev[7:8, :], pltpu.roll(z, 1, 0))
    z2 = jnp.where(row == 0, prev[6:7, :], jnp.where(row == 1, prev[7:8, :], pltpu.roll(z, 2, 0)))
    y = cw_ref[0:1, :] * z2 + cw_ref[1:2, :] * z1 + cw_ref[2:3, :] * z + cb_ref[...]
    od_ref[...] = (bg * y).astype(od_ref.dtype)
    carry_ref[...] = z[tm - 8:tm, :]


def odd_proj(x, w_in, conv_w, conv_b, batch, seq):
    t, d = x.shape
    tm = min(PROJ_TM, seq)
    per_b = seq // tm
    n = w_in.shape[1]
    n_h = 4 * HGRN_W
    c = CONV_CH
    tok = lambda b, i: (b * per_b + i, 0)
    return pl.pallas_call(
        _odd_proj_kernel,
        grid=(batch, per_b),
        in_specs=[pl.BlockSpec((tm, d), tok),
                  pl.BlockSpec((d, n), lambda b, i: (0, 0)),
                  pl.BlockSpec((3, c), lambda b, i: (0, 0)),
                  pl.BlockSpec((1, c), lambda b, i: (0, 0))],
        out_specs=[pl.BlockSpec((tm, n_h), tok), pl.BlockSpec((tm, c), tok)],
        out_shape=[jax.ShapeDtypeStruct((t, n_h), F32), jax.ShapeDtypeStruct((t, c), BF16)],
        scratch_shapes=[pltpu.VMEM((8, c), F32)],
        compiler_params=_params("arbitrary", "arbitrary"),
        name="odd_proj",
    )(x, w_in.astype(BF16), conv_w, conv_b.reshape(1, c))


def _split3(x):
    hi = x.astype(BF16)
    r1 = x - hi.astype(F32)
    mid = r1.astype(BF16)
    lo = (r1 - mid.astype(F32)).astype(BF16)
    return hi, mid, lo


def _hgrn_kernel(q_ref, f_ref, i_ref, g_ref, lbl_ref, ng_ref, o_ref, state_ref, *, layer):
    c = HGRN_STEP
    dk = HGRN_DK
    n_chunks = q_ref.shape[0] // c
    lw = lbl_ref[...]
    lw = jnp.exp(lw - jnp.max(lw, axis=0, keepdims=True))
    lw = lw / jnp.sum(lw, axis=0, keepdims=True)
    lb = jnp.sum(lw[1:layer + 1], axis=0, keepdims=True)
    row = lax.broadcasted_iota(I32, (c, c), 0)
    col = lax.broadcasted_iota(I32, (c, c), 1)
    tril = jnp.where(row >= col, 1.0, 0.0).astype(BF16)
    sub = lax.broadcasted_iota(I32, (8, dk), 0)

    @pl.when(pl.program_id(1) == 0)
    def _():
        state_ref[...] = jnp.zeros_like(state_ref)

    def decay(ci):
        z = f_ref[pl.ds(pl.multiple_of(ci * c, c), c), :]
        k_all = (1.0 - lb) * _sigmoid(-z)
        log_f = jnp.log1p(-k_all)
        hi, mid, lo = _split3(log_f)
        b_all = _dot(tril, hi) + _dot(tril, mid) + _dot(tril, lo)
        return k_all, b_all * LOG2_E

    def chunk(ci, carry):
        k_all, b_all = carry
        ahead = decay(jnp.minimum(ci + 1, n_chunks - 1))
        r0 = pl.multiple_of(ci * c, c)
        q_all = q_ref[pl.ds(r0, c), :]
        v_all = i_ref[pl.ds(r0, c), :]
        g_all = g_ref[pl.ds(r0, c), :]
        heads = range(HGRN_HEADS)
        nblk = c // 8
        blk = [slice(8 * j, 8 * j + 8) for j in range(nblk)]
        q, k, v, b = ([x[:, h * dk:(h + 1) * dk] for h in heads] for x in (q_all, k_all, v_all, b_all))
        states = [state_ref[h] for h in heads]
        o = [_dot_nt((q[h] * jnp.exp2(b[h])).astype(BF16), states[h].astype(BF16)) for h in heads]
        b_last = [b[h][c - 1:c, :] for h in heads]
        for h in heads:
            kd = (k[h] * jnp.exp2(b_last[h] - b[h])).astype(BF16)
            state_ref[h] = jnp.exp2(b_last[h]) * states[h] + _dot_tn(v[h].astype(BF16), kd)

        k_dec = [jnp.concatenate([k[h][blk[j]] * jnp.exp2(b[h][8 * j + 7:8 * j + 8] - b[h][blk[j]])
                                  for j in range(nblk)], axis=0) for h in heads]
        q_dec = [{(j, j0): q[h][blk[j]] * jnp.exp2(b[h][blk[j]] - b[h][8 * j0 + 7:8 * j0 + 8])
                  for j0 in range(nblk) for j in range(j0 + 1, nblk)} for h in heads]
        acc = [[o[h][blk[j]] for j in range(nblk)] for h in heads]
        for s in range(c):
            j0 = s // 8
            for h in heads:
                vs = v[h][s:s + 1, :]
                dlt = jnp.where(sub >= s - 8 * j0, b[h][blk[j0]] - b[h][s:s + 1, :], NEG_INF)
                a = jnp.sum(q[h][blk[j0]] * k[h][s:s + 1, :] * jnp.exp2(dlt), axis=-1, keepdims=True)
                acc[h][j0] = acc[h][j0] + a * vs
                for j in range(j0 + 1, nblk):
                    a = jnp.sum(q_dec[h][j, j0] * k_dec[h][s:s + 1, :], axis=-1, keepdims=True)
                    acc[h][j] = acc[h][j] + a * vs
        outs = []
        for h in heads:
            oh = jnp.concatenate(acc[h], axis=0)
            outs.append(oh * lax.rsqrt(jnp.mean(oh * oh, axis=-1, keepdims=True) + RMS_EPS))
        o_ref[pl.ds(r0, c), :] = (jnp.concatenate(outs, axis=1) * ng_ref[...]
                                  * (g_all * _sigmoid(g_all))).astype(o_ref.dtype)
        return ahead

    lax.fori_loop(0, n_chunks, chunk, decay(0))


def hgrn2_mixer(proj, lb_logits, norm_g, layer, batch, seq):
    w = HGRN_W
    seg = min(seq, HGRN_SEG)
    per_b = seq // seg

    def spec(grp):
        return pl.BlockSpec((seg, w), lambda b, i: (b * per_b + i, grp))

    return pl.pallas_call(
        functools.partial(_hgrn_kernel, layer=layer),
        grid=(batch, per_b),
        in_specs=[spec(0), spec(1), spec(2), spec(3),
                  pl.BlockSpec((DEPTH, w), lambda b, i: (0, 0)),
                  pl.BlockSpec((1, w), lambda b, i: (0, 0))],
        out_specs=pl.BlockSpec((seg, w), lambda b, i: (b * per_b + i, 0)),
        out_shape=jax.ShapeDtypeStruct((batch * seq, w), BF16),
        scratch_shapes=[pltpu.VMEM((HGRN_HEADS, HGRN_DK, HGRN_DK), F32)],
        compiler_params=_params("arbitrary", "arbitrary"),
        name="hgrn2_mixer",
    )(proj, proj, proj, proj, lb_logits, norm_g.reshape(1, w))


EVEN_T_ROWS = NSA_Q + 4 * 2 * HEAD_DIM + 32
EVEN_S_COLS = 4 * LANES + 2 * SGU_W


def _even_proj_kernel(x_ref, wt_ref, ws_ref, cos_ref, sin_ref, lng_ref, lnb_ref, sw_ref, sb_ref, qt_ref, qrt_ref,
                      ks_ref, kw_ref, vs_ref, vw_ref, gt_ref, kvc_ref, ob_ref, *, per_b):
    tm = x_ref.shape[0]
    half = HEAD_DIM // 2
    scale = HEAD_DIM ** -0.5 * LOG2_E
    v0 = NSA_Q + 4 * HEAD_DIM
    rows = tm // EVEN_ROW_GROUPS
    part = [slice(r * rows, (r + 1) * rows) for r in range(EVEN_ROW_GROUPS)]
    xb = [x_ref[p, :].astype(BF16) for p in part]
    st_all = [_dot_nt(wt_ref[...], xb_r) for xb_r in xb]
    ss_all = [_dot(xb_r, ws_ref[...]) for xb_r in xb]
    for r, p in enumerate(part):
        st, ss = st_all[r], ss_all[r]
        cos = cos_ref[:, p]
        sin = sin_ref[:, p]

        def rope(blk, cos=cos, sin=sin):
            x1, x2 = blk[:half], blk[half:]
            return jnp.concatenate([x1 * cos - x2 * sin, x2 * cos + x1 * sin], axis=0)

        for hh in range(NSA_HEADS):
            blk = st[hh * HEAD_DIM:(hh + 1) * HEAD_DIM]
            qt_ref[hh * HEAD_DIM:(hh + 1) * HEAD_DIM, p] = (blk * scale).astype(BF16)
            qrt_ref[hh * HEAD_DIM:(hh + 1) * HEAD_DIM, p] = (rope(blk) * scale).astype(BF16)

        kk = jnp.concatenate([rope(st[NSA_Q + j * HEAD_DIM:NSA_Q + (j + 1) * HEAD_DIM]) for j in range(4)], axis=0)
        kk = kk.T
        pos = (lax.rem(pl.program_id(0), per_b) * tm + r * rows) + lax.broadcasted_iota(I32, (rows, LANES), 0)
        lane = lax.broadcasted_iota(I32, (rows, LANES), 1)
        member = jnp.where(lane == pos // SLC_BLOCK, 1.0, 0.0)
        ks_ref[p, :] = jnp.concatenate([kk[:, :LANES], member], axis=1).astype(BF16)
        kw_ref[p, :] = kk[:, LANES:].astype(BF16)
        for j in range(rows // NSA_KT):
            vs_ref[r * (rows // NSA_KT) + j] = st[v0:v0 + LANES, j * NSA_KT:(j + 1) * NSA_KT].astype(BF16)
        for j in range(rows // LANES):
            vw_ref[r * (rows // LANES) + j] = st[v0 + LANES:v0 + 2 * LANES, j * LANES:(j + 1) * LANES].astype(BF16)
        gt_ref[:, p] = _sigmoid(st[v0 + 2 * LANES:v0 + 2 * LANES + 32])

        for j in range(4):
            kvc_ref[j, p, :] = ss[:, j * LANES:j * LANES + HEAD_DIM]
        _sgu_tile(ss[:, 4 * LANES:4 * LANES + SGU_W], ss[:, 4 * LANES + SGU_W:], lng_ref, lnb_ref, sw_ref, sb_ref,
                  ob_ref, r * rows)


def even_proj(x, w_in, cos_t, sin_t, sgu_ln_g, sgu_ln_b, sgu_w, sgu_b, seq):
    t, d = x.shape
    tm = min(PROJ_TM, seq)
    per_b = seq // tm
    hd = HEAD_DIM
    kv0 = NSA_Q

    def kvcols(i):
        return w_in[:, kv0 + i * 2 * hd:kv0 + (i + 1) * 2 * hd]

    g0 = kv0 + 6 * 2 * hd
    n_gates = 3 * NSA_HEADS
    wt = jnp.concatenate([w_in[:, :NSA_Q], kvcols(2), kvcols(4), kvcols(3), kvcols(5),
                          w_in[:, g0:g0 + n_gates], jnp.zeros((d, 32 - n_gates), F32)], axis=1).T.astype(BF16)
    zpad = jnp.zeros((d, LANES - hd), F32)
    cmp_cols = []
    for i in (0, 1):
        for g in range(NSA_KV_HEADS):
            cmp_cols += [w_in[:, kv0 + i * 2 * hd + g * hd:kv0 + i * 2 * hd + (g + 1) * hd], zpad]
    ws = jnp.concatenate(cmp_cols + [w_in[:, g0 + n_gates:]], axis=1).astype(BF16)
    half = hd // 2
    tok = lambda i: (i, 0)
    tok_t = lambda i: (0, i)
    return pl.pallas_call(
        functools.partial(_even_proj_kernel, per_b=per_b),
        grid=(t // tm,),
        in_specs=[pl.BlockSpec((tm, d), tok),
                  pl.BlockSpec((EVEN_T_ROWS, d), lambda i: (0, 0)),
                  pl.BlockSpec((d, EVEN_S_COLS), lambda i: (0, 0)),
                  pl.BlockSpec((half, tm), tok_t),
                  pl.BlockSpec((half, tm), tok_t),
                  pl.BlockSpec((SGU_GROUPS, SGU_CH), lambda i: (0, 0)),
                  pl.BlockSpec((SGU_GROUPS, SGU_CH), lambda i: (0, 0)),
                  pl.BlockSpec((SGU_GROUPS, SGU_CHUNK, SGU_CHUNK), lambda i: (0, 0, 0)),
                  pl.BlockSpec((SGU_CHUNK, SGU_GROUPS), lambda i: (0, 0))],
        out_specs=[pl.BlockSpec((NSA_Q, tm), tok_t),
                   pl.BlockSpec((NSA_Q, tm), tok_t),
                   pl.BlockSpec((tm, 2 * LANES), tok),
                   pl.BlockSpec((tm, LANES), tok),
                   pl.BlockSpec((tm // NSA_KT, LANES, NSA_KT), lambda i: (i, 0, 0)),
                   pl.BlockSpec((tm // LANES, LANES, LANES), lambda i: (i, 0, 0)),
                   pl.BlockSpec((32, tm), tok_t),
                   pl.BlockSpec((4, tm, hd), lambda i: (0, i, 0)),
                   pl.BlockSpec((tm, SGU_W), tok)],
        out_shape=[jax.ShapeDtypeStruct((NSA_Q, t), BF16),
                   jax.ShapeDtypeStruct((NSA_Q, t), BF16),
                   jax.ShapeDtypeStruct((t, 2 * LANES), BF16),
                   jax.ShapeDtypeStruct((t, LANES), BF16),
                   jax.ShapeDtypeStruct((t // NSA_KT, LANES, NSA_KT), BF16),
                   jax.ShapeDtypeStruct((t // LANES, LANES, LANES), BF16),
                   jax.ShapeDtypeStruct((32, t), F32),
                   jax.ShapeDtypeStruct((4, t, hd), F32),
                   jax.ShapeDtypeStruct((t, SGU_W), BF16)],
        compiler_params=_params("parallel"),
        name="even_proj",
    )(x, wt, ws, cos_t, sin_t, sgu_ln_g, sgu_ln_b, sgu_w, sgu_b.T)


def _compress_kernel(kvc_ref, pos_ref, w1_ref, w2_ref, kc_ref, vct_ref):
    ncp = kc_ref.shape[1]
    assert CMP_BLOCK == 2 * CMP_STRIDE
    for i in range(2):
        outs = []
        for g in range(NSA_KV_HEADS):
            first = jnp.zeros((ncp, CMP_HIDDEN), F32)
            second = jnp.zeros((ncp, CMP_HIDDEN), F32)
            for l in range(CMP_STRIDE):
                rows = kvc_ref[i * NSA_KV_HEADS + g, pl.ds(l, ncp, stride=CMP_STRIDE), :]
                first = first + _dot((rows + pos_ref[i, l:l + 1, :]).astype(BF16), w1_ref[i, l])
                l2 = l + CMP_STRIDE
                second = second + _dot((rows + pos_ref[i, l2:l2 + 1, :]).astype(BF16), w1_ref[i, l2])
            hid = _gelu(first + pltpu.roll(second, ncp - 1, 0))
            outs.append(_dot(hid.astype(BF16), w2_ref[i]))
        if i == 0:
            kc_ref[0] = jnp.concatenate(outs, axis=1).astype(BF16)
        else:
            vct_ref[0] = jnp.concatenate(outs, axis=1).T.astype(BF16)


def nsa_compress(kvc, cmp_pos, cmp_w1, cmp_w2, batch, seq):
    hd = HEAD_DIM
    ncp = seq // CMP_STRIDE
    w1 = cmp_w1.reshape(2, CMP_BLOCK, hd, CMP_HIDDEN).astype(BF16)
    return pl.pallas_call(
        _compress_kernel,
        grid=(batch,),
        in_specs=[pl.BlockSpec((4, seq, hd), lambda b: (0, b, 0)),
                  pl.BlockSpec((2, CMP_BLOCK, hd), lambda b: (0, 0, 0)),
                  pl.BlockSpec((2, CMP_BLOCK, hd, CMP_HIDDEN), lambda b: (0, 0, 0, 0)),
                  pl.BlockSpec((2, CMP_HIDDEN, hd), lambda b: (0, 0, 0))],
        out_specs=[pl.BlockSpec((1, ncp, 2 * hd), lambda b: (b, 0, 0)),
                   pl.BlockSpec((1, 2 * hd, ncp), lambda b: (b, 0, 0))],
        out_shape=[jax.ShapeDtypeStruct((batch, ncp, 2 * hd), BF16),
                   jax.ShapeDtypeStruct((batch, 2 * hd, ncp), BF16)],
        compiler_params=_params("parallel"),
        name="nsa_compress",
    )(kvc, cmp_pos, w1, cmp_w2.astype(BF16))


def _softmax2_cols(s, mask):
    sm = jnp.where(mask, s, NEG_INF)
    m = jnp.max(sm, axis=0, keepdims=True)
    e = jnp.exp2(sm - m)
    return jnp.where(mask, e / jnp.sum(e, axis=0, keepdims=True), 0.0)


def _nsa_kernel(qt_ref, qrt_ref, gt_ref, kc_ref, vct_ref, mct_ref, ks_ref, kw_ref, vs_ref, vw_ref, o_ref,
                s_scr, q_scr, m_scr, l_scr, acc_scr, oc_scr, ow_scr, *, n_top):
    tq = qt_ref.shape[1]
    hg = NSA_GROUP
    hd = HEAD_DIM
    groups = range(NSA_KV_HEADS)
    w = hg * tq
    ncp = kc_ref.shape[1]
    n_slc = mct_ref.shape[0]
    s0 = pl.program_id(1) * tq
    t_lane = s0 + lax.rem(lax.broadcasted_iota(I32, (1, w), 1), tq)
    t_q = s0 + lax.broadcasted_iota(I32, (1, tq), 1)
    zeros_g = jnp.zeros((hd, w), BF16)
    n_win = WINDOW // LANES + tq // LANES
    wt0 = jnp.maximum(s0 // LANES - WINDOW // LANES, 0)
    vrows = [slice(g * hd, (g + 1) * hd) for g in groups]

    def grp(ref, g):
        rows = jnp.concatenate([ref[(g * hg + h) * hd:(g * hg + h + 1) * hd, :] for h in range(hg)], axis=1)
        return jnp.concatenate([rows, zeros_g] if g == 0 else [zeros_g, rows], axis=0)

    q_rot = [grp(qrt_ref, g) for g in groups]
    kw0 = pl.multiple_of(wt0 * LANES, LANES)
    k_win = kw_ref[pl.ds(kw0, n_win * LANES), :]
    s_cmp = [_dot(kc_ref[0], grp(qt_ref, g)) for g in groups]
    s_win = [_dot(k_win, q_rot[g]) for g in groups]

    c_end = lax.broadcasted_iota(I32, (ncp, w), 0) * CMP_STRIDE + (CMP_BLOCK - 1)
    p_cmp = [_softmax2_cols(s_cmp[g], c_end <= t_lane) for g in groups]
    imp = []
    for g in groups:
        oc_scr[g] = _dot(vct_ref[0, vrows[g], :], p_cmp[g].astype(BF16))
        psum = p_cmp[g][:, 0:tq]
        for h in range(1, hg):
            psum = psum + p_cmp[g][:, h * tq:(h + 1) * tq]
        imp.append(_dot(mct_ref[...], psum.astype(BF16)))

    dist = t_lane - (kw0 + lax.broadcasted_iota(I32, (n_win * LANES, w), 0))
    in_window = (dist >= 0) & (dist < WINDOW)
    for g in groups:
        sm = jnp.where(in_window, s_win[g], NEG_INF)
        e = jnp.exp2(sm - jnp.max(sm, axis=0, keepdims=True))
        den = jnp.sum(e, axis=0, keepdims=True)
        eb = e.astype(BF16)
        ow = _dot(vw_ref[wt0, vrows[g], :], eb[0:LANES])
        for j in range(1, n_win):
            ow = ow + _dot(vw_ref[wt0 + j, vrows[g], :], eb[j * LANES:(j + 1) * LANES])
        ow_scr[g] = ow / den

    j_blk = lax.broadcasted_iota(I32, (n_slc, tq), 0)
    cur = t_q // SLC_BLOCK
    forced = (j_blk == 0) | (j_blk == cur) | (j_blk == cur - 1)
    valid = j_blk <= cur
    score = [jnp.where(forced, SLC_FORCED_SCORE, jnp.where(valid, imp[g], -1.0)) for g in groups]
    nv = n_slc // 8
    sblk = [[score[g][8 * v:8 * v + 8] for v in range(nv)] for g in groups]
    rank = [[jnp.zeros((8, tq), F32) for _ in range(nv)] for g in groups]
    sub = lax.broadcasted_iota(I32, (8, tq), 0)
    for k in range(n_slc):
        kv_ = k // 8
        for g in groups:
            sk = score[g][k:k + 1, :]
            for v in range(nv):
                ge = jnp.where(sk >= sblk[g][v], 1.0, 0.0)
                gt = jnp.where(sk > sblk[g][v], 1.0, 0.0)
                if v > kv_:
                    beats = ge
                elif v < kv_:
                    beats = gt
                else:
                    beats = jnp.where(sub > k - 8 * kv_, ge, gt)
                rank[g][v] = rank[g][v] + beats
    q_aug = []
    for g in groups:
        bias = jnp.where((jnp.concatenate(rank[g], axis=0) < n_top) & valid, 0.0, NEG_INF)
        bias = jnp.concatenate([bias] * hg, axis=1)
        if n_slc < 2 * hd:
            bias = jnp.concatenate([bias, jnp.zeros((2 * hd - n_slc, w), F32)], axis=0)
        q_aug.append(jnp.concatenate([q_rot[g], bias.astype(BF16)], axis=0))

    for g in groups:
        q_scr[g] = q_aug[g]
        m_scr[g] = jnp.full((1, w), NEG_INF, F32)
        l_scr[g] = jnp.zeros((1, w), F32)
        acc_scr[g] = jnp.zeros((hd, w), F32)

    def slc_scores(kt, slot):
        k_tile = ks_ref[pl.ds(pl.multiple_of(kt * NSA_KT, NSA_KT), NSA_KT), :]
        for g in groups:
            s_scr[slot, g] = _dot(k_tile, q_scr[g])

    def slc_update(kt, slot, diagonal):
        s = [s_scr[slot, g] for g in groups]
        if diagonal:
            causal = kt * NSA_KT + lax.broadcasted_iota(I32, (NSA_KT, w), 0) <= t_lane
            s = [jnp.where(causal, s[g], NEG_INF) for g in groups]
        m_old = [m_scr[g] for g in groups]
        m_new = [jnp.maximum(m_old[g], jnp.max(s[g], axis=0, keepdims=True)) for g in groups]
        pp = [jnp.exp2(s[g] - m_new[g]) for g in groups]
        for g in groups:
            alpha = jnp.exp2(m_old[g] - m_new[g])
            l_scr[g] = l_scr[g] * alpha + jnp.sum(pp[g], axis=0, keepdims=True)
            acc_scr[g] = acc_scr[g] * alpha + _dot(vs_ref[kt, vrows[g], :], pp[g].astype(BF16))
            m_scr[g] = m_new[g]

    n_full = s0 // NSA_KT

    def slc_pair(p, c):
        a = 2 * p
        slc_scores(a + 1, 1)
        slc_update(a, 0, False)
        slc_scores(a + 2, 0)
        slc_update(a + 1, 1, False)
        return c

    slc_scores(0, 0)
    lax.fori_loop(0, n_full // 2, slc_pair, 0)
    odd = lax.rem(n_full, 2) == 1

    @pl.when(odd)
    def _():
        slc_scores(n_full, 1)
        slc_update(n_full - 1, 0, False)
        slc_update(n_full, 1, True)

    @pl.when(jnp.logical_not(odd))
    def _():
        slc_update(n_full, 0, True)

    o_slc = [acc_scr[g] / l_scr[g] for g in groups]

    heads = []
    for g in groups:
        for h in range(hg):
            r = (g * hg + h) * 3
            cols = slice(h * tq, (h + 1) * tq)
            heads.append(gt_ref[r:r + 1, :] * oc_scr[g, :, cols] + gt_ref[r + 1:r + 2, :] * o_slc[g][:, cols]
                         + gt_ref[r + 2:r + 3, :] * ow_scr[g, :, cols])
    o_ref[...] = jnp.concatenate(heads, axis=0).T.astype(o_ref.dtype)


def _cmp_to_slc_t(seq):
    ncp = seq // CMP_STRIDE
    ns = seq // SLC_BLOCK
    cs = np.arange(ncp)[None, :] * CMP_STRIDE
    ss = np.arange(ns)[:, None] * SLC_BLOCK
    ov = np.clip(np.minimum(cs + CMP_BLOCK, ss + SLC_BLOCK) - np.maximum(cs, ss), 0, None) / CMP_BLOCK
    ov[:, ncp - 1] = 0.0
    return jnp.asarray(ov, BF16)


def nsa_mixer(qt, qrt, gt, kc, vct, ks, kw, vs, vw, batch, seq):
    tq = NSA_TQ
    nq = seq // tq
    ncp = seq // CMP_STRIDE
    n_slc = seq // SLC_BLOCK
    assert n_slc <= 2 * HEAD_DIM and n_slc % 8 == 0 and seq >= WINDOW + tq
    w = NSA_GROUP * tq
    col = lambda b, i: (0, b * nq + i)
    return pl.pallas_call(
        functools.partial(_nsa_kernel, n_top=min(SLC_TOP, n_slc)),
        grid=(batch, nq),
        in_specs=[pl.BlockSpec((NSA_Q, tq), col),
                  pl.BlockSpec((NSA_Q, tq), col),
                  pl.BlockSpec((32, tq), col),
                  pl.BlockSpec((1, ncp, 2 * HEAD_DIM), lambda b, i: (b, 0, 0)),
                  pl.BlockSpec((1, 2 * HEAD_DIM, ncp), lambda b, i: (b, 0, 0)),
                  pl.BlockSpec((n_slc, ncp), lambda b, i: (0, 0)),
                  pl.BlockSpec((seq, 2 * LANES), lambda b, i: (b, 0)),
                  pl.BlockSpec((seq, LANES), lambda b, i: (b, 0)),
                  pl.BlockSpec((seq // NSA_KT, LANES, NSA_KT), lambda b, i: (b, 0, 0)),
                  pl.BlockSpec((seq // LANES, LANES, LANES), lambda b, i: (b, 0, 0))],
        out_specs=pl.BlockSpec((tq, NSA_Q), lambda b, i: (b * nq + i, 0)),
        out_shape=jax.ShapeDtypeStruct((batch * seq, NSA_Q), BF16),
        scratch_shapes=[pltpu.VMEM((2, NSA_KV_HEADS, NSA_KT, w), F32),
                        pltpu.VMEM((NSA_KV_HEADS, 4 * HEAD_DIM, w), BF16),
                        pltpu.VMEM((NSA_KV_HEADS, 1, w), F32),
                        pltpu.VMEM((NSA_KV_HEADS, 1, w), F32),
                        pltpu.VMEM((NSA_KV_HEADS, HEAD_DIM, w), F32),
                        pltpu.VMEM((NSA_KV_HEADS, HEAD_DIM, w), F32),
                        pltpu.VMEM((NSA_KV_HEADS, HEAD_DIM, w), F32)],
        compiler_params=_params("parallel", "parallel"),
        name="nsa_mixer",
    )(qt, qrt, gt, kc, vct, _cmp_to_slc_t(seq), ks, kw, vs, vw)


def kernel(x, mem, positions, w_in_even, nsa_cmp_pos, nsa_cmp_w1, nsa_cmp_w2, sgu_ln_g, sgu_ln_b, sgu_w, sgu_b, w_out_even, w_in_odd, hgrn_lb_logits, hgrn_norm_g, conv_w, conv_b, w_out_odd, xattn_w_q, xattn_w_kv, xattn_w_o, ln_g, ln_b, router_w, router_b, expert_w_gu, expert_b_gu, expert_w_dn, expert_b_dn):
    batch, seq, d = x.shape
    t = batch * seq
    cos_t, sin_t = rope_tables_t(positions)
    mem2 = mem.reshape(-1, d)
    xf = x.reshape(t, d)
    for layer in range(DEPTH):
        j = layer // 2
        if layer % 2 == 0:
            qt, qrt, ks, kw, vs, vw, gt, kvc, o_b = even_proj(xf, w_in_even[j], cos_t, sin_t, sgu_ln_g[j], sgu_ln_b[j],
                                                              sgu_w[j], sgu_b[j], seq)
            kc, vct = nsa_compress(kvc, nsa_cmp_pos[j], nsa_cmp_w1[j], nsa_cmp_w2[j], batch, seq)
            o_a = nsa_mixer(qt, qrt, gt, kc, vct, ks, kw, vs, vw, batch, seq)
            w_out = w_out_even[j]
        else:
            proj, o_b = odd_proj(xf, w_in_odd[j], conv_w[j], conv_b[j], batch, seq)
            o_a = hgrn2_mixer(proj, hgrn_lb_logits, hgrn_norm_g[j], layer, batch, seq)
            w_out = w_out_odd[j]
        xf = outproj_ln(o_a, o_b, xf, w_out, ln_g[layer, 0], ln_b[layer, 0])
        kv = matmul(mem2, xattn_w_kv[layer].astype(BF16), BF16, mem.shape[1]).reshape(batch, mem.shape[1], 2 * d)
        xf = xattn_ln(xf, kv, xattn_w_q[layer], xattn_w_o[layer], ln_g[layer, 1], ln_b[layer, 1], seq)
        xf = moe_ln(xf, router_w[layer], router_b[layer], expert_w_gu, expert_b_gu, expert_w_dn, expert_b_dn,
                    ln_g[layer, 2], ln_b[layer, 2], layer)
    return xf.reshape(batch, seq, d)
```

```python
import functools

import numpy as np
import jax
import jax.numpy as jnp
from jax import lax
from jax.experimental import pallas as pl
from jax.experimental.pallas import tpu as pltpu

F32 = jnp.float32
BF16 = jnp.bfloat16
I32 = jnp.int32

D_MODEL = 1024
DEPTH = 2
HEAD_DIM = 64
NSA_HEADS = 8
NSA_KV_HEADS = 2
NSA_GROUP = NSA_HEADS // NSA_KV_HEADS
CMP_BLOCK = 32
CMP_STRIDE = 16
CMP_HIDDEN = 256
SLC_BLOCK = 64
SLC_TOP = 16
WINDOW = 512
SLC_FORCED_SCORE = 1e4
SGU_GROUPS = 4
SGU_CH = 128
SGU_CHUNK = 128
HGRN_HEADS = 4
HGRN_DK = 128
CONV_CH = 512
XATTN_HEADS = 4
XATTN_DIM = D_MODEL // XATTN_HEADS
N_EXPERTS = 32
TOP_K = 4
D_EXPERT = D_MODEL
SWIGLU_LIMIT = 7.0
SWIGLU_ALPHA = 1.702
ROPE_THETA = 10000.0
LN_EPS = 1e-5
RMS_EPS = 1e-6
NEG_INF = -1e30
DEEPNORM_ALPHA = (2 * DEPTH) ** 0.25
LOG2_E = 1.4426950408889634

NSA_Q = NSA_HEADS * HEAD_DIM
SGU_W = SGU_GROUPS * SGU_CH
HGRN_W = HGRN_HEADS * HGRN_DK

VMEM_LIMIT_BYTES = 56 * 1024 * 1024
LANES = 128

PROJ_TM = 1024
XATTN_ROW_GROUPS = 4
OUTPROJ_ROW_GROUPS = 4
EVEN_ROW_GROUPS = 2
NSA_TQ = 256
NSA_KT = 256
HGRN_STEP = 32
HGRN_SEG = 1024
MOE_TM = 512
ROUTER_TILES = 4
MOE_BM = 512
SEG_ALIGN = 8
MOE_RT = TOP_K * MOE_TM + SEG_ALIGN * N_EXPERTS
MOE_CHUNK = 256


def _params(*sem):
    return pltpu.CompilerParams(dimension_semantics=sem, vmem_limit_bytes=VMEM_LIMIT_BYTES)


def _dot(a, b):
    return jnp.dot(a, b, preferred_element_type=F32)


def _dot_nt(a, b):
    return lax.dot_general(a, b, (((1,), (1,)), ((), ())), preferred_element_type=F32)


def _dot_tn(a, b):
    return lax.dot_general(a, b, (((0,), (0,)), ((), ())), preferred_element_type=F32)


def _gelu(x):
    return 0.5 * x * (1.0 + jnp.tanh(np.sqrt(2.0 / np.pi).astype(np.float32) * (x + 0.044715 * (x * x * x))))


def _sigmoid(x):
    return 1.0 / (1.0 + jnp.exp(-x))


def _layer_norm(y, g, b):
    mu = jnp.mean(y, axis=-1, keepdims=True)
    d = y - mu
    var = jnp.mean(d * d, axis=-1, keepdims=True)
    return d * lax.rsqrt(var + LN_EPS) * g + b


def _rope_kernel(pos_ref, inv_ref, cos_ref, sin_ref):
    ang = pos_ref[...].astype(F32) * inv_ref[...]
    cos_ref[...] = jnp.cos(ang)
    sin_ref[...] = jnp.sin(ang)


def rope_tables_t(positions):
    t = positions.size
    inv = 1.0 / (ROPE_THETA ** (jnp.arange(0, HEAD_DIM, 2, dtype=F32) / HEAD_DIM))
    tn = min(t, 4096)
    half = HEAD_DIM // 2
    return pl.pallas_call(
        _rope_kernel,
        grid=(t // tn,),
        in_specs=[pl.BlockSpec((1, tn), lambda i: (0, i)),
                  pl.BlockSpec((half, 1), lambda i: (0, 0))],
        out_specs=[pl.BlockSpec((half, tn), lambda i: (0, i))] * 2,
        out_shape=[jax.ShapeDtypeStruct((half, t), F32)] * 2,
        compiler_params=_params("parallel"),
        name="rope_tables",
    )(positions.reshape(1, t), inv.reshape(half, 1))


def _mm_kernel(x_ref, w_ref, o_ref):
    o_ref[...] = _dot(x_ref[...].astype(BF16), w_ref[...]).astype(o_ref.dtype)


def matmul(x, w, out_dtype, tm):
    m, k = x.shape
    n = w.shape[1]
    return pl.pallas_call(
        _mm_kernel,
        grid=(m // tm,),
        in_specs=[pl.BlockSpec((tm, k), lambda i: (i, 0)),
                  pl.BlockSpec((k, n), lambda i: (0, 0))],
        out_specs=pl.BlockSpec((tm, n), lambda i: (i, 0)),
        out_shape=jax.ShapeDtypeStruct((m, n), out_dtype),
        compiler_params=_params("parallel"),
        name="matmul",
    )(x, w)


def _outproj_ln_kernel(a_ref, b_ref, x_ref, wa_ref, wb_ref, g_ref, beta_ref, o_ref):
    rows = a_ref.shape[0] // OUTPROJ_ROW_GROUPS
    part = [slice(r * rows, (r + 1) * rows) for r in range(OUTPROJ_ROW_GROUPS)]
    mix = [_dot(a_ref[p, :].astype(BF16), wa_ref[...]) + _dot(b_ref[p, :].astype(BF16), wb_ref[...]) for p in part]
    for p, m in zip(part, mix):
        o_ref[p, :] = _layer_norm(DEEPNORM_ALPHA * x_ref[p, :] + m, g_ref[...], beta_ref[...])


def outproj_ln(a, b, x, w_out, g, beta):
    t, d = x.shape
    na, nb = a.shape[1], b.shape[1]
    tm = min(PROJ_TM, t)
    wa = w_out[:na].astype(BF16)
    wb = w_out[na:].astype(BF16)
    return pl.pallas_call(
        _outproj_ln_kernel,
        grid=(t // tm,),
        in_specs=[pl.BlockSpec((tm, na), lambda i: (i, 0)),
                  pl.BlockSpec((tm, nb), lambda i: (i, 0)),
                  pl.BlockSpec((tm, d), lambda i: (i, 0)),
                  pl.BlockSpec((na, d), lambda i: (0, 0)),
                  pl.BlockSpec((nb, d), lambda i: (0, 0)),
                  pl.BlockSpec((1, d), lambda i: (0, 0)),
                  pl.BlockSpec((1, d), lambda i: (0, 0))],
        out_specs=pl.BlockSpec((tm, d), lambda i: (i, 0)),
        out_shape=jax.ShapeDtypeStruct((t, d), F32),
        compiler_params=_params("parallel"),
        name="outproj_ln",
    )(a, b, x, wa, wb, g.reshape(1, d), beta.reshape(1, d))


def _xattn_kernel(x_ref, wq_ref, kv_ref, wo_ref, g_ref, beta_ref, o_ref):
    hw = XATTN_HEADS * XATTN_DIM
    rows = x_ref.shape[0] // XATTN_ROW_GROUPS
    groups = range(XATTN_ROW_GROUPS)
    heads = range(XATTN_HEADS)
    x = [x_ref[r * rows:(r + 1) * rows, :] for r in groups]
    q = [_dot(x[r].astype(BF16), wq_ref[...]) for r in groups]
    s = [[_dot_nt(q[r][:, h * XATTN_DIM:(h + 1) * XATTN_DIM].astype(BF16),
                  kv_ref[0, :, h * XATTN_DIM:(h + 1) * XATTN_DIM]) * (XATTN_DIM ** -0.5) for h in heads] for r in groups]
    o = []
    for r in groups:
        parts = []
        for h in heads:
            m = jnp.max(s[r][h], axis=-1, keepdims=True)
            e = jnp.exp(s[r][h] - m)
            p = e / jnp.sum(e, axis=-1, keepdims=True)
            parts.append(_dot(p.astype(BF16), kv_ref[0, :, hw + h * XATTN_DIM:hw + (h + 1) * XATTN_DIM]))
        o.append(jnp.concatenate(parts, axis=-1))
    xa = [_dot(o[r].astype(BF16), wo_ref[...]) for r in groups]
    for r in groups:
        o_ref[r * rows:(r + 1) * rows, :] = _layer_norm(DEEPNORM_ALPHA * x[r] + xa[r], g_ref[...], beta_ref[...])


def xattn_ln(x, kv, w_q, w_o, g, beta, seq):
    t, d = x.shape
    tm = min(PROJ_TM, seq)
    per_b = seq // tm
    mlen = kv.shape[1]
    return pl.pallas_call(
        _xattn_kernel,
        grid=(t // tm,),
        in_specs=[pl.BlockSpec((tm, d), lambda i: (i, 0)),
                  pl.BlockSpec((d, d), lambda i: (0, 0)),
                  pl.BlockSpec((1, mlen, 2 * d), lambda i: (i // per_b, 0, 0)),
                  pl.BlockSpec((d, d), lambda i: (0, 0)),
                  pl.BlockSpec((1, d), lambda i: (0, 0)),
                  pl.BlockSpec((1, d), lambda i: (0, 0))],
        out_specs=pl.BlockSpec((tm, d), lambda i: (i, 0)),
        out_shape=jax.ShapeDtypeStruct((t, d), F32),
        compiler_params=_params("parallel"),
        name="xattn_ln",
    )(x, w_q.astype(BF16), kv, w_o.astype(BF16), g.reshape(1, d), beta.reshape(1, d))


def _router_kernel(x_ref, wt_ref, b_ref, tri_ref, ltri_ref, row_ref, gate_ref, off_ref, cnt_ref, base_ref,
                   carry_ref):
    i = pl.program_id(0)

    @pl.when(i == 0)
    def _():
        carry_ref[...] = jnp.zeros_like(carry_ref)

    tm = MOE_TM
    tiles = range(x_ref.shape[0] // tm)
    cols = [slice(u * tm, (u + 1) * tm) for u in tiles]
    e_iota = lax.broadcasted_iota(I32, (N_EXPERTS, tm), 0)
    work = [_dot_nt(wt_ref[...], x_ref[cols[u], :].astype(BF16)) + b_ref[...] for u in tiles]
    vals, hots = [[] for _ in tiles], [[] for _ in tiles]
    for _ in range(TOP_K):
        for u in tiles:
            m = jnp.max(work[u], axis=0, keepdims=True)
            idx = jnp.min(jnp.where(work[u] == m, e_iota, N_EXPERTS), axis=0, keepdims=True)
            hot = e_iota == idx
            vals[u].append(m)
            hots[u].append(hot)
            work[u] = jnp.where(hot, -jnp.inf, work[u])
    hot_all, seg, off = [], [], []
    for u in tiles:
        exps = [jnp.exp(v - vals[u][0]) for v in vals[u]]
        den = exps[0] + exps[1] + exps[2] + exps[3]
        gate_ref[:, cols[u]] = jnp.concatenate([e / den for e in exps], axis=0)
        ha = jnp.zeros((N_EXPERTS, tm), F32)
        for hot in hots[u]:
            ha = ha + jnp.where(hot, 1.0, 0.0)
        hot_all.append(ha)
        n = jnp.sum(ha, axis=1, keepdims=True)
        sg = jnp.floor((n + (SEG_ALIGN - 1)) * (1.0 / SEG_ALIGN))
        seg.append(jnp.broadcast_to(sg, (N_EXPERTS, LANES)))
    rank = [_dot(hot_all[u].astype(BF16), tri_ref[...]) for u in tiles]
    off = [_dot(ltri_ref[...], seg[u].astype(BF16)) for u in tiles]
    carry = carry_ref[...]
    for u in tiles:
        where_row = off[u][:, 0:1] * SEG_ALIGN + rank[u]
        row_ref[:, cols[u]] = jnp.concatenate(
            [jnp.sum(jnp.where(hot, where_row, 0.0), axis=0, keepdims=True) for hot in hots[u]], axis=0).astype(I32)
        off_ref[u] = off[u] * SEG_ALIGN
        cnt_ref[u] = seg[u] * SEG_ALIGN
        base_ref[u] = carry
        carry = carry + seg[u] * SEG_ALIGN
    carry_ref[...] = carry


def moe_route(x, w_router, b_router):
    t, d = x.shape
    tm = MOE_TM
    nt = t // tm
    tri = jnp.asarray(np.triu(np.ones((tm, tm), np.float32), 1), BF16)
    ltri = jnp.asarray(np.tril(np.ones((N_EXPERTS, N_EXPERTS), np.float32), -1), BF16)
    per = ROUTER_TILES if nt % ROUTER_TILES == 0 else 1
    tab = pl.BlockSpec((per, N_EXPERTS, LANES), lambda i: (i, 0, 0))
    tab_shape = jax.ShapeDtypeStruct((nt, N_EXPERTS, LANES), F32)
    return pl.pallas_call(
        _router_kernel,
        grid=(nt // per,),
        in_specs=[pl.BlockSpec((per * tm, d), lambda i: (i, 0)),
                  pl.BlockSpec((N_EXPERTS, d), lambda i: (0, 0)),
                  pl.BlockSpec((N_EXPERTS, 1), lambda i: (0, 0)),
                  pl.BlockSpec((tm, tm), lambda i: (0, 0)),
                  pl.BlockSpec((N_EXPERTS, N_EXPERTS), lambda i: (0, 0))],
        out_specs=[pl.BlockSpec((TOP_K, per * tm), lambda i: (0, i)),
                   pl.BlockSpec((TOP_K, per * tm), lambda i: (0, i)),
                   tab, tab, tab],
        out_shape=[jax.ShapeDtypeStruct((TOP_K, t), I32),
                   jax.ShapeDtypeStruct((TOP_K, t), F32),
                   tab_shape, tab_shape, tab_shape],
        scratch_shapes=[pltpu.VMEM((N_EXPERTS, LANES), F32)],
        compiler_params=_params("arbitrary"),
        name="moe_router",
    )(x, w_router.T.astype(BF16), b_router.reshape(N_EXPERTS, 1), tri, ltri)


def _segment_copies(off_ref, cnt_ref, dst_ref, make_copy, wait):
    for e in range(N_EXPERTS):
        n = pl.multiple_of(cnt_ref[0, 0, e], SEG_ALIGN)

        @pl.when(n > 0)
        def _():
            cp = make_copy(pl.multiple_of(off_ref[0, 0, e], SEG_ALIGN), pl.multiple_of(dst_ref[0, 0, e], SEG_ALIGN), n)
            if wait:
                cp.wait()
            else:
                cp.start()


def _dispatch_kernel(off_ref, cnt_ref, dst_ref, poff_ref, pcnt_ref, pdst_ref, zero_ref, tail_cnt_ref, tail_dst_ref,
                     rest_ref, x_ref, row_ref, xs_hbm, buf, sem):
    i = pl.program_id(0)
    slot = lax.rem(i, 2)
    tm = x_ref.shape[0]
    xb = x_ref[...].astype(BF16)
    rows = [row_ref[k:k + 1, :] for k in range(TOP_K)]
    for c in range(MOE_RT // MOE_CHUNK):
        rr = c * MOE_CHUNK + lax.broadcasted_iota(I32, (MOE_CHUNK, tm), 0)
        perm = jnp.where(rr == rows[0], 1.0, 0.0)
        for k in range(1, TOP_K):
            perm = perm + jnp.where(rr == rows[k], 1.0, 0.0)
        buf[slot, c * MOE_CHUNK:(c + 1) * MOE_CHUNK, :] = _dot(perm.astype(BF16), xb)

    def copy_from(s):
        def copy(src_row, dst_row, n):
            return pltpu.make_async_copy(buf.at[s, pl.ds(src_row, n)], xs_hbm.at[pl.ds(dst_row, n)], sem.at[s])
        return copy

    @pl.when(i > 0)
    def _():
        _segment_copies(poff_ref, pcnt_ref, pdst_ref, copy_from(1 - slot), wait=True)

    _segment_copies(off_ref, cnt_ref, dst_ref, copy_from(slot), wait=False)

    @pl.when(i == pl.num_programs(0) - 1)
    def _():
        _segment_copies(off_ref, cnt_ref, dst_ref, copy_from(slot), wait=True)
        buf[0, 0:MOE_BM, :] = jnp.zeros((MOE_BM, buf.shape[2]), F32)
        zeros = copy_from(0)
        for wait in (False, True):
            _segment_copies(zero_ref, tail_cnt_ref, tail_dst_ref, zeros, wait=wait)

            def rest(j, c):
                cp = zeros(0, pl.multiple_of(rest_ref[0] + j * MOE_BM, MOE_BM), MOE_BM)
                cp.wait() if wait else cp.start()
                return c
            lax.fori_loop(0, rest_ref[1], rest, 0)


def _seg_spec(n_tiles, shift=0):
    def index(i):
        return (jnp.clip(i + shift, 0, n_tiles - 1), 0, 0)
    return pl.BlockSpec((1, 1, N_EXPERTS), index, memory_space=pltpu.SMEM)


def _smem_whole(shape):
    return pl.BlockSpec(shape, lambda i: (0,) * len(shape), memory_space=pltpu.SMEM)


def moe_dispatch(x, row, seg_off, seg_cnt, seg_dst, tail_cnt, tail_dst, rest, n_rows):
    t, d = x.shape
    tm = MOE_TM
    nt = t // tm
    tab = (1, 1, N_EXPERTS)
    return pl.pallas_call(
        _dispatch_kernel,
        grid=(nt,),
        in_specs=[_seg_spec(nt), _seg_spec(nt), _seg_spec(nt),
                  _seg_spec(nt, -1), _seg_spec(nt, -1), _seg_spec(nt, -1),
                  _smem_whole(tab), _smem_whole(tab), _smem_whole(tab), _smem_whole((2,)),
                  pl.BlockSpec((tm, d), lambda i: (i, 0)),
                  pl.BlockSpec((TOP_K, tm), lambda i: (0, i))],
        out_specs=pl.BlockSpec(memory_space=pl.ANY),
        out_shape=jax.ShapeDtypeStruct((n_rows, d), F32),
        scratch_shapes=[pltpu.VMEM((2, MOE_RT, d), F32), pltpu.SemaphoreType.DMA((2,))],
        compiler_params=_params("arbitrary"),
        name="moe_dispatch",
    )(seg_off, seg_cnt, seg_dst, seg_off, seg_cnt, seg_dst, jnp.zeros(tab, I32), tail_cnt.reshape(tab),
      tail_dst.reshape(tab), rest, x, row)


def _expert_kernel(blk_e_ref, nused_ref, xs_ref, wgu_ref, bgu_ref, wdn_ref, bdn_ref, ys_ref, wgu_bf, wdn_bf):
    i = pl.program_id(0)
    used = i < nused_ref[0]
    new_expert = (i == 0) | (blk_e_ref[i] != blk_e_ref[jnp.maximum(i - 1, 0)])

    @pl.when(used & new_expert)
    def _():
        rows = 128
        for r in range(0, wgu_bf.shape[0], rows):
            wgu_bf[r:r + rows, :] = wgu_ref[0, 0, r:r + rows, :].astype(BF16)
        for r in range(0, wdn_bf.shape[0], rows):
            wdn_bf[r:r + rows, :] = wdn_ref[0, 0, r:r + rows, :].astype(BF16)

    @pl.when(used)
    def _():
        h = _dot(xs_ref[...].astype(BF16), wgu_bf[...]) + bgu_ref[0, 0]
        h_gate = jnp.minimum(h[:, :D_EXPERT], SWIGLU_LIMIT)
        h_up = jnp.clip(h[:, D_EXPERT:], -SWIGLU_LIMIT, SWIGLU_LIMIT)
        act = (h_up + 1.0) * (h_gate * _sigmoid(h_gate * SWIGLU_ALPHA))
        ys_ref[...] = _dot(act.astype(BF16), wdn_bf[...]) + bdn_ref[0, 0]

    @pl.when(jnp.logical_not(used))
    def _():
        ys_ref[...] = jnp.zeros_like(ys_ref)


def moe_experts(xs, blk_e, nused, w_gu, b_gu, w_dn, b_dn, layer):
    n_rows, d = xs.shape
    bm = MOE_BM
    nb = n_rows // bm
    wsel = lambda i, be, nu: (layer, be[i], 0, 0)
    grid_spec = pltpu.PrefetchScalarGridSpec(
        num_scalar_prefetch=2,
        grid=(nb,),
        in_specs=[pl.BlockSpec((bm, d), lambda i, be, nu: (jnp.minimum(i, nu[0] - 1), 0)),
                  pl.BlockSpec((1, 1, d, 2 * D_EXPERT), wsel),
                  pl.BlockSpec((1, 1, 1, 2 * D_EXPERT), wsel),
                  pl.BlockSpec((1, 1, D_EXPERT, d), wsel),
                  pl.BlockSpec((1, 1, 1, d), wsel)],
        out_specs=pl.BlockSpec((bm, d), lambda i, be, nu: (i, 0)),
        scratch_shapes=[pltpu.VMEM((d, 2 * D_EXPERT), BF16), pltpu.VMEM((D_EXPERT, d), BF16)],
    )
    return pl.pallas_call(
        _expert_kernel,
        grid_spec=grid_spec,
        out_shape=jax.ShapeDtypeStruct((n_rows, d), F32),
        compiler_params=_params("arbitrary"),
        name="moe_experts",
    )(blk_e, nused, xs, w_gu, b_gu.reshape(DEPTH, N_EXPERTS, 1, -1), w_dn, b_dn.reshape(DEPTH, N_EXPERTS, 1, -1))


def _combine_ln_kernel(off_ref, cnt_ref, dst_ref, noff_ref, ncnt_ref, ndst_ref, ys_hbm, row_ref, gate_ref, x_ref,
                       g_ref, beta_ref, o_ref, buf, sem):
    i = pl.program_id(0)
    slot = lax.rem(i, 2)
    tm = x_ref.shape[0]

    def copy_into(s):
        def copy(buf_row, ys_row, n):
            return pltpu.make_async_copy(ys_hbm.at[pl.ds(ys_row, n)], buf.at[s, pl.ds(buf_row, n)], sem.at[s])
        return copy

    @pl.when(i == 0)
    def _():
        buf[...] = jnp.zeros_like(buf)
        _segment_copies(off_ref, cnt_ref, dst_ref, copy_into(0), wait=False)

    @pl.when(i + 1 < pl.num_programs(0))
    def _():
        _segment_copies(noff_ref, ncnt_ref, ndst_ref, copy_into(1 - slot), wait=False)

    rows = [jnp.broadcast_to(row_ref[:, k:k + 1], (tm, MOE_CHUNK)) for k in range(TOP_K)]
    gates = [jnp.broadcast_to(gate_ref[:, k:k + 1], (tm, MOE_CHUNK)) for k in range(TOP_K)]
    lane = lax.broadcasted_iota(I32, (tm, MOE_CHUNK), 1)
    _segment_copies(off_ref, cnt_ref, dst_ref, copy_into(slot), wait=True)
    ff = jnp.zeros((tm, x_ref.shape[1]), F32)
    for c in range(MOE_RT // MOE_CHUNK):
        rr = lane + c * MOE_CHUNK
        mix = jnp.where(rr == rows[0], gates[0], 0.0)
        for k in range(1, TOP_K):
            mix = mix + jnp.where(rr == rows[k], gates[k], 0.0)
        ff = ff + _dot(mix.astype(BF16), buf[slot, c * MOE_CHUNK:(c + 1) * MOE_CHUNK, :].astype(BF16))
    o_ref[...] = _layer_norm(DEEPNORM_ALPHA * x_ref[...] + ff, g_ref[...], beta_ref[...])


def moe_combine_ln(ys, row_t, gate_t, seg_off, seg_cnt, seg_dst, x, g, beta):
    t, d = x.shape
    tm = MOE_TM
    nt = t // tm
    return pl.pallas_call(
        _combine_ln_kernel,
        grid=(nt,),
        in_specs=[_seg_spec(nt), _seg_spec(nt), _seg_spec(nt),
                  _seg_spec(nt, 1), _seg_spec(nt, 1), _seg_spec(nt, 1),
                  pl.BlockSpec(memory_space=pl.ANY),
                  pl.BlockSpec((tm, TOP_K), lambda i: (i, 0)),
                  pl.BlockSpec((tm, TOP_K), lambda i: (i, 0)),
                  pl.BlockSpec((tm, d), lambda i: (i, 0)),
                  pl.BlockSpec((1, d), lambda i: (0, 0)),
                  pl.BlockSpec((1, d), lambda i: (0, 0))],
        out_specs=pl.BlockSpec((tm, d), lambda i: (i, 0)),
        out_shape=jax.ShapeDtypeStruct((t, d), F32),
        scratch_shapes=[pltpu.VMEM((2, MOE_RT, d), F32), pltpu.SemaphoreType.DMA((2,))],
        compiler_params=_params("arbitrary"),
        name="moe_combine_ln",
    )(seg_off, seg_cnt, seg_dst, seg_off, seg_cnt, seg_dst, ys, row_t, gate_t, x, g.reshape(1, d),
      beta.reshape(1, d))


def moe_ln(x, w_router, b_router, w_gu, b_gu, w_dn, b_dn, g, beta, layer):
    t, d = x.shape
    bm = MOE_BM
    nt = t // MOE_TM
    row, gate, off, cnt, base = moe_route(x, w_router, b_router)
    seg_off = off[:, :, 0].astype(I32)
    seg_cnt = cnt[:, :, 0].astype(I32)
    seg_base = base[:, :, 0].astype(I32)
    total = seg_base[-1] + seg_cnt[-1]
    padded = (total + bm - 1) // bm * bm
    pend = jnp.cumsum(padded)
    pstart = pend - padded
    n_rows = (t * TOP_K + SEG_ALIGN * N_EXPERTS * nt) // bm * bm + N_EXPERTS * bm
    nb = n_rows // bm
    blk_row = jnp.arange(nb, dtype=I32) * bm
    blk_e = jnp.minimum(jnp.sum((pend[None, :] <= blk_row[:, None]).astype(I32), axis=1), N_EXPERTS - 1)
    nused = (pend[-1] // bm).astype(I32).reshape(1)
    seg_dst = (pstart[None, :] + seg_base).reshape(nt, 1, N_EXPERTS)
    seg_off = seg_off.reshape(nt, 1, N_EXPERTS)
    seg_cnt = seg_cnt.reshape(nt, 1, N_EXPERTS)
    rest = jnp.stack([pend[-1], nb - nused[0]]).astype(I32)
    xs = moe_dispatch(x, row, seg_off, seg_cnt, seg_dst, padded - total, pstart + total, rest, n_rows)
    ys = moe_experts(xs, blk_e, nused, w_gu, b_gu, w_dn, b_dn, layer)
    return moe_combine_ln(ys, row.T, gate.T, seg_off, seg_cnt, seg_dst, x, g, beta)


def _sgu_tile(u, v, lng_ref, lnb_ref, w_ref, bs_ref, o_ref, row0):
    tm = u.shape[0]
    row = lax.broadcasted_iota(I32, (SGU_CHUNK, SGU_CHUNK), 0)
    col = lax.broadcasted_iota(I32, (SGU_CHUNK, SGU_CHUNK), 1)
    causal = row >= col
    for g in range(SGU_GROUPS):
        lo = g * SGU_CH
        vg = _gelu(v[:, lo:lo + SGU_CH])
        vg = _layer_norm(vg, lng_ref[g:g + 1, :], lnb_ref[g:g + 1, :]).astype(BF16)
        wg = jnp.where(causal, w_ref[g], 0.0).astype(BF16)
        bias = bs_ref[:, g:g + 1]
        for n in range(tm // SGU_CHUNK):
            r0 = n * SGU_CHUNK
            mix = _dot(wg, vg[r0:r0 + SGU_CHUNK]) + bias
            o_ref[row0 + r0:row0 + r0 + SGU_CHUNK, lo:lo + SGU_CH] = (
                _gelu(u[r0:r0 + SGU_CHUNK, lo:lo + SGU_CH]) * mix).astype(o_ref.dtype)


def _odd_proj_kernel(x_ref, w_ref, cw_ref, cb_ref, hg_ref, od_ref, carry_ref):
    @pl.when(pl.program_id(1) == 0)
    def _():
        carry_ref[...] = jnp.zeros_like(carry_ref)

    pr = _dot(x_ref[...].astype(BF16), w_ref[...])
    n_h = 4 * HGRN_W
    hg_ref[...] = pr[:, :n_h]
    h, bg, cg = (pr[:, n_h + j * CONV_CH:n_h + (j + 1) * CONV_CH] for j in range(3))
    z = cg * h
    tm = z.shape[0]
    row = lax.broadcasted_iota(I32, z.shape, 0)
    prev = carry_ref[...]
    z1 = jnp.where(row == 0, prev[7:8, :], pltpu.roll(z, 1, 0))
    z2 = jnp.where(row == 0, prev[6:7, :], jnp.where(row == 1, prev[7:8, :], pltpu.roll(z, 2, 0)))
    y = cw_ref[0:1, :] * z2 + cw_ref[1:2, :] * z1 + cw_ref[2:3, :] * z + cb_ref[...]
    od_ref[...] = (bg * y).astype(od_ref.dtype)
    carry_ref[...] = z[tm - 8:tm, :]


def odd_proj(x, w_in, conv_w, conv_b, batch, seq):
    t, d = x.shape
    tm = min(PROJ_TM, seq)
    per_b = seq // tm
    n = w_in.shape[1]
    n_h = 4 * HGRN_W
    c = CONV_CH
    tok = lambda b, i: (b * per_b + i, 0)
    return pl.pallas_call(
        _odd_proj_kernel,
        grid=(batch, per_b),
        in_specs=[pl.BlockSpec((tm, d), tok),
                  pl.BlockSpec((d, n), lambda b, i: (0, 0)),
                  pl.BlockSpec((3, c), lambda b, i: (0, 0)),
                  pl.BlockSpec((1, c), lambda b, i: (0, 0))],
        out_specs=[pl.BlockSpec((tm, n_h), tok), pl.BlockSpec((tm, c), tok)],
        out_shape=[jax.ShapeDtypeStruct((t, n_h), F32), jax.ShapeDtypeStruct((t, c), BF16)],
        scratch_shapes=[pltpu.VMEM((8, c), F32)],
        compiler_params=_params("arbitrary", "arbitrary"),
        name="odd_proj",
    )(x, w_in.astype(BF16), conv_w, conv_b.reshape(1, c))


def _split3(x):
    hi = x.astype(BF16)
    r1 = x - hi.astype(F32)
    mid = r1.astype(BF16)
    lo = (r1 - mid.astype(F32)).astype(BF16)
    return hi, mid, lo


def _hgrn_kernel(q_ref, f_ref, i_ref, g_ref, lbl_ref, ng_ref, o_ref, state_ref, *, layer):
    c = HGRN_STEP
    dk = HGRN_DK
    n_chunks = q_ref.shape[0] // c
    lw = lbl_ref[...]
    lw = jnp.exp(lw - jnp.max(lw, axis=0, keepdims=True))
    lw = lw / jnp.sum(lw, axis=0, keepdims=True)
    lb = jnp.sum(lw[1:layer + 1], axis=0, keepdims=True)
    row = lax.broadcasted_iota(I32, (c, c), 0)
    col = lax.broadcasted_iota(I32, (c, c), 1)
    tril = jnp.where(row >= col, 1.0, 0.0).astype(BF16)
    sub = lax.broadcasted_iota(I32, (8, dk), 0)

    @pl.when(pl.program_id(1) == 0)
    def _():
        state_ref[...] = jnp.zeros_like(state_ref)

    def decay(ci):
        z = f_ref[pl.ds(pl.multiple_of(ci * c, c), c), :]
        k_all = (1.0 - lb) * _sigmoid(-z)
        log_f = jnp.log1p(-k_all)
        hi, mid, lo = _split3(log_f)
        b_all = _dot(tril, hi) + _dot(tril, mid) + _dot(tril, lo)
        return k_all, b_all * LOG2_E

    def chunk(ci, carry):
        k_all, b_all = carry
        ahead = decay(jnp.minimum(ci + 1, n_chunks - 1))
        r0 = pl.multiple_of(ci * c, c)
        q_all = q_ref[pl.ds(r0, c), :]
        v_all = i_ref[pl.ds(r0, c), :]
        g_all = g_ref[pl.ds(r0, c), :]
        heads = range(HGRN_HEADS)
        nblk = c // 8
        blk = [slice(8 * j, 8 * j + 8) for j in range(nblk)]
        q, k, v, b = ([x[:, h * dk:(h + 1) * dk] for h in heads] for x in (q_all, k_all, v_all, b_all))
        states = [state_ref[h] for h in heads]
        o = [_dot_nt((q[h] * jnp.exp2(b[h])).astype(BF16), states[h].astype(BF16)) for h in heads]
        b_last = [b[h][c - 1:c, :] for h in heads]
        for h in heads:
            kd = (k[h] * jnp.exp2(b_last[h] - b[h])).astype(BF16)
            state_ref[h] = jnp.exp2(b_last[h]) * states[h] + _dot_tn(v[h].astype(BF16), kd)

        k_dec = [jnp.concatenate([k[h][blk[j]] * jnp.exp2(b[h][8 * j + 7:8 * j + 8] - b[h][blk[j]])
                                  for j in range(nblk)], axis=0) for h in heads]
        q_dec = [{(j, j0): q[h][blk[j]] * jnp.exp2(b[h][blk[j]] - b[h][8 * j0 + 7:8 * j0 + 8])
                  for j0 in range(nblk) for j in range(j0 + 1, nblk)} for h in heads]
        acc = [[o[h][blk[j]] for j in range(nblk)] for h in heads]
        for s in range(c):
            j0 = s // 8
            for h in heads:
                vs = v[h][s:s + 1, :]
                dlt = jnp.where(sub >= s - 8 * j0, b[h][blk[j0]] - b[h][s:s + 1, :], NEG_INF)
                a = jnp.sum(q[h][blk[j0]] * k[h][s:s + 1, :] * jnp.exp2(dlt), axis=-1, keepdims=True)
                acc[h][j0] = acc[h][j0] + a * vs
                for j in range(j0 + 1, nblk):
                    a = jnp.sum(q_dec[h][j, j0] * k_dec[h][s:s + 1, :], axis=-1, keepdims=True)
                    acc[h][j] = acc[h][j] + a * vs
        outs = []
        for h in heads:
            oh = jnp.concatenate(acc[h], axis=0)
            outs.append(oh * lax.rsqrt(jnp.mean(oh * oh, axis=-1, keepdims=True) + RMS_EPS))
        o_ref[pl.ds(r0, c), :] = (jnp.concatenate(outs, axis=1) * ng_ref[...]
                                  * (g_all * _sigmoid(g_all))).astype(o_ref.dtype)
        return ahead

    lax.fori_loop(0, n_chunks, chunk, decay(0))


def hgrn2_mixer(proj, lb_logits, norm_g, layer, batch, seq):
    w = HGRN_W
    seg = min(seq, HGRN_SEG)
    per_b = seq // seg

    def spec(grp):
        return pl.BlockSpec((seg, w), lambda b, i: (b * per_b + i, grp))

    return pl.pallas_call(
        functools.partial(_hgrn_kernel, layer=layer),
        grid=(batch, per_b),
        in_specs=[spec(0), spec(1), spec(2), spec(3),
                  pl.BlockSpec((DEPTH, w), lambda b, i: (0, 0)),
                  pl.BlockSpec((1, w), lambda b, i: (0, 0))],
        out_specs=pl.BlockSpec((seg, w), lambda b, i: (b * per_b + i, 0)),
        out_shape=jax.ShapeDtypeStruct((batch * seq, w), BF16),
        scratch_shapes=[pltpu.VMEM((HGRN_HEADS, HGRN_DK, HGRN_DK), F32)],
        compiler_params=_params("arbitrary", "arbitrary"),
        name="hgrn2_mixer",
    )(proj, proj, proj, proj, lb_logits, norm_g.reshape(1, w))


EVEN_T_ROWS = NSA_Q + 4 * 2 * HEAD_DIM + 32
EVEN_S_COLS = 4 * LANES + 2 * SGU_W


def _even_proj_kernel(x_ref, wt_ref, ws_ref, cos_ref, sin_ref, lng_ref, lnb_ref, sw_ref, sb_ref, qt_ref, qrt_ref,
                      ks_ref, kw_ref, vs_ref, vw_ref, gt_ref, kvc_ref, ob_ref, *, per_b):
    tm = x_ref.shape[0]
    half = HEAD_DIM // 2
    scale = HEAD_DIM ** -0.5 * LOG2_E
    v0 = NSA_Q + 4 * HEAD_DIM
    rows = tm // EVEN_ROW_GROUPS
    part = [slice(r * rows, (r + 1) * rows) for r in range(EVEN_ROW_GROUPS)]
    xb = [x_ref[p, :].astype(BF16) for p in part]
    st_all = [_dot_nt(wt_ref[...], xb_r) for xb_r in xb]
    ss_all = [_dot(xb_r, ws_ref[...]) for xb_r in xb]
    for r, p in enumerate(part):
        st, ss = st_all[r], ss_all[r]
        cos = cos_ref[:, p]
        sin = sin_ref[:, p]

        def rope(blk, cos=cos, sin=sin):
            x1, x2 = blk[:half], blk[half:]
            return jnp.concatenate([x1 * cos - x2 * sin, x2 * cos + x1 * sin], axis=0)

        for hh in range(NSA_HEADS):
            blk = st[hh * HEAD_DIM:(hh + 1) * HEAD_DIM]
            qt_ref[hh * HEAD_DIM:(hh + 1) * HEAD_DIM, p] = (blk * scale).astype(BF16)
            qrt_ref[hh * HEAD_DIM:(hh + 1) * HEAD_DIM, p] = (rope(blk) * scale).astype(BF16)

        kk = jnp.concatenate([rope(st[NSA_Q + j * HEAD_DIM:NSA_Q + (j + 1) * HEAD_DIM]) for j in range(4)], axis=0)
        kk = kk.T
        pos = (lax.rem(pl.program_id(0), per_b) * tm + r * rows) + lax.broadcasted_iota(I32, (rows, LANES), 0)
        lane = lax.broadcasted_iota(I32, (rows, LANES), 1)
        member = jnp.where(lane == pos // SLC_BLOCK, 1.0, 0.0)
        ks_ref[p, :] = jnp.concatenate([kk[:, :LANES], member], axis=1).astype(BF16)
        kw_ref[p, :] = kk[:, LANES:].astype(BF16)
        for j in range(rows // NSA_KT):
            vs_ref[r * (rows // NSA_KT) + j] = st[v0:v0 + LANES, j * NSA_KT:(j + 1) * NSA_KT].astype(BF16)
        for j in range(rows // LANES):
            vw_ref[r * (rows // LANES) + j] = st[v0 + LANES:v0 + 2 * LANES, j * LANES:(j + 1) * LANES].astype(BF16)
        gt_ref[:, p] = _sigmoid(st[v0 + 2 * LANES:v0 + 2 * LANES + 32])

        for j in range(4):
            kvc_ref[j, p, :] = ss[:, j * LANES:j * LANES + HEAD_DIM]
        _sgu_tile(ss[:, 4 * LANES:4 * LANES + SGU_W], ss[:, 4 * LANES + SGU_W:], lng_ref, lnb_ref, sw_ref, sb_ref,
                  ob_ref, r * rows)


def even_proj(x, w_in, cos_t, sin_t, sgu_ln_g, sgu_ln_b, sgu_w, sgu_b, seq):
    t, d = x.shape
    tm = min(PROJ_TM, seq)
    per_b = seq // tm
    hd = HEAD_DIM
    kv0 = NSA_Q

    def kvcols(i):
        return w_in[:, kv0 + i * 2 * hd:kv0 + (i + 1) * 2 * hd]

    g0 = kv0 + 6 * 2 * hd
    n_gates = 3 * NSA_HEADS
    wt = jnp.concatenate([w_in[:, :NSA_Q], kvcols(2), kvcols(4), kvcols(3), kvcols(5),
                          w_in[:, g0:g0 + n_gates], jnp.zeros((d, 32 - n_gates), F32)], axis=1).T.astype(BF16)
    zpad = jnp.zeros((d, LANES - hd), F32)
    cmp_cols = []
    for i in (0, 1):
        for g in range(NSA_KV_HEADS):
            cmp_cols += [w_in[:, kv0 + i * 2 * hd + g * hd:kv0 + i * 2 * hd + (g + 1) * hd], zpad]
    ws = jnp.concatenate(cmp_cols + [w_in[:, g0 + n_gates:]], axis=1).astype(BF16)
    half = hd // 2
    tok = lambda i: (i, 0)
    tok_t = lambda i: (0, i)
    return pl.pallas_call(
        functools.partial(_even_proj_kernel, per_b=per_b),
        grid=(t // tm,),
        in_specs=[pl.BlockSpec((tm, d), tok),
                  pl.BlockSpec((EVEN_T_ROWS, d), lambda i: (0, 0)),
                  pl.BlockSpec((d, EVEN_S_COLS), lambda i: (0, 0)),
                  pl.BlockSpec((half, tm), tok_t),
                  pl.BlockSpec((half, tm), tok_t),
                  pl.BlockSpec((SGU_GROUPS, SGU_CH), lambda i: (0, 0)),
                  pl.BlockSpec((SGU_GROUPS, SGU_CH), lambda i: (0, 0)),
                  pl.BlockSpec((SGU_GROUPS, SGU_CHUNK, SGU_CHUNK), lambda i: (0, 0, 0)),
                  pl.BlockSpec((SGU_CHUNK, SGU_GROUPS), lambda i: (0, 0))],
        out_specs=[pl.BlockSpec((NSA_Q, tm), tok_t),
                   pl.BlockSpec((NSA_Q, tm), tok_t),
                   pl.BlockSpec((tm, 2 * LANES), tok),
                   pl.BlockSpec((tm, LANES), tok),
                   pl.BlockSpec((tm // NSA_KT, LANES, NSA_KT), lambda i: (i, 0, 0)),
                   pl.BlockSpec((tm // LANES, LANES, LANES), lambda i: (i, 0, 0)),
                   pl.BlockSpec((32, tm), tok_t),
                   pl.BlockSpec((4, tm, hd), lambda i: (0, i, 0)),
                   pl.BlockSpec((tm, SGU_W), tok)],
        out_shape=[jax.ShapeDtypeStruct((NSA_Q, t), BF16),
                   jax.ShapeDtypeStruct((NSA_Q, t), BF16),
                   jax.ShapeDtypeStruct((t, 2 * LANES), BF16),
                   jax.ShapeDtypeStruct((t, LANES), BF16),
                   jax.ShapeDtypeStruct((t // NSA_KT, LANES, NSA_KT), BF16),
                   jax.ShapeDtypeStruct((t // LANES, LANES, LANES), BF16),
                   jax.ShapeDtypeStruct((32, t), F32),
                   jax.ShapeDtypeStruct((4, t, hd), F32),
                   jax.ShapeDtypeStruct((t, SGU_W), BF16)],
        compiler_params=_params("parallel"),
        name="even_proj",
    )(x, wt, ws, cos_t, sin_t, sgu_ln_g, sgu_ln_b, sgu_w, sgu_b.T)


def _compress_kernel(kvc_ref, pos_ref, w1_ref, w2_ref, kc_ref, vct_ref):
    ncp = kc_ref.shape[1]
    assert CMP_BLOCK == 2 * CMP_STRIDE
    for i in range(2):
        outs = []
        for g in range(NSA_KV_HEADS):
            first = jnp.zeros((ncp, CMP_HIDDEN), F32)
            second = jnp.zeros((ncp, CMP_HIDDEN), F32)
            for l in range(CMP_STRIDE):
                rows = kvc_ref[i * NSA_KV_HEADS + g, pl.ds(l, ncp, stride=CMP_STRIDE), :]
                first = first + _dot((rows + pos_ref[i, l:l + 1, :]).astype(BF16), w1_ref[i, l])
                l2 = l + CMP_STRIDE
                second = second + _dot((rows + pos_ref[i, l2:l2 + 1, :]).astype(BF16), w1_ref[i, l2])
            hid = _gelu(first + pltpu.roll(second, ncp - 1, 0))
            outs.append(_dot(hid.astype(BF16), w2_ref[i]))
        if i == 0:
            kc_ref[0] = jnp.concatenate(outs, axis=1).astype(BF16)
        else:
            vct_ref[0] = jnp.concatenate(outs, axis=1).T.astype(BF16)


def nsa_compress(kvc, cmp_pos, cmp_w1, cmp_w2, batch, seq):
    hd = HEAD_DIM
    ncp = seq // CMP_STRIDE
    w1 = cmp_w1.reshape(2, CMP_BLOCK, hd, CMP_HIDDEN).astype(BF16)
    return pl.pallas_call(
        _compress_kernel,
        grid=(batch,),
        in_specs=[pl.BlockSpec((4, seq, hd), lambda b: (0, b, 0)),
                  pl.BlockSpec((2, CMP_BLOCK, hd), lambda b: (0, 0, 0)),
                  pl.BlockSpec((2, CMP_BLOCK, hd, CMP_HIDDEN), lambda b: (0, 0, 0, 0)),
                  pl.BlockSpec((2, CMP_HIDDEN, hd), lambda b: (0, 0, 0))],
        out_specs=[pl.BlockSpec((1, ncp, 2 * hd), lambda b: (b, 0, 0)),
                   pl.BlockSpec((1, 2 * hd, ncp), lambda b: (b, 0, 0))],
        out_shape=[jax.ShapeDtypeStruct((batch, ncp, 2 * hd), BF16),
                   jax.ShapeDtypeStruct((batch, 2 * hd, ncp), BF16)],
        compiler_params=_params("parallel"),
        name="nsa_compress",
    )(kvc, cmp_pos, w1, cmp_w2.astype(BF16))


def _softmax2_cols(s, mask):
    sm = jnp.where(mask, s, NEG_INF)
    m = jnp.max(sm, axis=0, keepdims=True)
    e = jnp.exp2(sm - m)
    return jnp.where(mask, e / jnp.sum(e, axis=0, keepdims=True), 0.0)


def _nsa_kernel(qt_ref, qrt_ref, gt_ref, kc_ref, vct_ref, mct_ref, ks_ref, kw_ref, vs_ref, vw_ref, o_ref,
                s_scr, q_scr, m_scr, l_scr, acc_scr, oc_scr, ow_scr, *, n_top):
    tq = qt_ref.shape[1]
    hg = NSA_GROUP
    hd = HEAD_DIM
    groups = range(NSA_KV_HEADS)
    w = hg * tq
    ncp = kc_ref.shape[1]
    n_slc = mct_ref.shape[0]
    s0 = pl.program_id(1) * tq
    t_lane = s0 + lax.rem(lax.broadcasted_iota(I32, (1, w), 1), tq)
    t_q = s0 + lax.broadcasted_iota(I32, (1, tq), 1)
    zeros_g = jnp.zeros((hd, w), BF16)
    n_win = WINDOW // LANES + tq // LANES
    wt0 = jnp.maximum(s0 // LANES - WINDOW // LANES, 0)
    vrows = [slice(g * hd, (g + 1) * hd) for g in groups]

    def grp(ref, g):
        rows = jnp.concatenate([ref[(g * hg + h) * hd:(g * hg + h + 1) * hd, :] for h in range(hg)], axis=1)
        return jnp.concatenate([rows, zeros_g] if g == 0 else [zeros_g, rows], axis=0)

    q_rot = [grp(qrt_ref, g) for g in groups]
    kw0 = pl.multiple_of(wt0 * LANES, LANES)
    k_win = kw_ref[pl.ds(kw0, n_win * LANES), :]
    s_cmp = [_dot(kc_ref[0], grp(qt_ref, g)) for g in groups]
    s_win = [_dot(k_win, q_rot[g]) for g in groups]

    c_end = lax.broadcasted_iota(I32, (ncp, w), 0) * CMP_STRIDE + (CMP_BLOCK - 1)
    p_cmp = [_softmax2_cols(s_cmp[g], c_end <= t_lane) for g in groups]
    imp = []
    for g in groups:
        oc_scr[g] = _dot(vct_ref[0, vrows[g], :], p_cmp[g].astype(BF16))
        psum = p_cmp[g][:, 0:tq]
        for h in range(1, hg):
            psum = psum + p_cmp[g][:, h * tq:(h + 1) * tq]
        imp.append(_dot(mct_ref[...], psum.astype(BF16)))

    dist = t_lane - (kw0 + lax.broadcasted_iota(I32, (n_win * LANES, w), 0))
    in_window = (dist >= 0) & (dist < WINDOW)
    for g in groups:
        sm = jnp.where(in_window, s_win[g], NEG_INF)
        e = jnp.exp2(sm - jnp.max(sm, axis=0, keepdims=True))
        den = jnp.sum(e, axis=0, keepdims=True)
        eb = e.astype(BF16)
        ow = _dot(vw_ref[wt0, vrows[g], :], eb[0:LANES])
        for j in range(1, n_win):
            ow = ow + _dot(vw_ref[wt0 + j, vrows[g], :], eb[j * LANES:(j + 1) * LANES])
        ow_scr[g] = ow / den

    j_blk = lax.broadcasted_iota(I32, (n_slc, tq), 0)
    cur = t_q // SLC_BLOCK
    forced = (j_blk == 0) | (j_blk == cur) | (j_blk == cur - 1)
    valid = j_blk <= cur
    score = [jnp.where(forced, SLC_FORCED_SCORE, jnp.where(valid, imp[g], -1.0)) for g in groups]
    nv = n_slc // 8
    sblk = [[score[g][8 * v:8 * v + 8] for v in range(nv)] for g in groups]
    rank = [[jnp.zeros((8, tq), F32) for _ in range(nv)] for g in groups]
    sub = lax.broadcasted_iota(I32, (8, tq), 0)
    for k in range(n_slc):
        kv_ = k // 8
        for g in groups:
            sk = score[g][k:k + 1, :]
            for v in range(nv):
                ge = jnp.where(sk >= sblk[g][v], 1.0, 0.0)
                gt = jnp.where(sk > sblk[g][v], 1.0, 0.0)
                if v > kv_:
                    beats = ge
                elif v < kv_:
                    beats = gt
                else:
                    beats = jnp.where(sub > k - 8 * kv_, ge, gt)
                rank[g][v] = rank[g][v] + beats
    q_aug = []
    for g in groups:
        bias = jnp.where((jnp.concatenate(rank[g], axis=0) < n_top) & valid, 0.0, NEG_INF)
        bias = jnp.concatenate([bias] * hg, axis=1)
        if n_slc < 2 * hd:
            bias = jnp.concatenate([bias, jnp.zeros((2 * hd - n_slc, w), F32)], axis=0)
        q_aug.append(jnp.concatenate([q_rot[g], bias.astype(BF16)], axis=0))

    for g in groups:
        q_scr[g] = q_aug[g]
        m_scr[g] = jnp.full((1, w), NEG_INF, F32)
        l_scr[g] = jnp.zeros((1, w), F32)
        acc_scr[g] = jnp.zeros((hd, w), F32)

    def slc_scores(kt, slot):
        k_tile = ks_ref[pl.ds(pl.multiple_of(kt * NSA_KT, NSA_KT), NSA_KT), :]
        for g in groups:
            s_scr[slot, g] = _dot(k_tile, q_scr[g])

    def slc_update(kt, slot, diagonal):
        s = [s_scr[slot, g] for g in groups]
        if diagonal:
            causal = kt * NSA_KT + lax.broadcasted_iota(I32, (NSA_KT, w), 0) <= t_lane
            s = [jnp.where(causal, s[g], NEG_INF) for g in groups]
        m_old = [m_scr[g] for g in groups]
        m_new = [jnp.maximum(m_old[g], jnp.max(s[g], axis=0, keepdims=True)) for g in groups]
        pp = [jnp.exp2(s[g] - m_new[g]) for g in groups]
        for g in groups:
            alpha = jnp.exp2(m_old[g] - m_new[g])
            l_scr[g] = l_scr[g] * alpha + jnp.sum(pp[g], axis=0, keepdims=True)
            acc_scr[g] = acc_scr[g] * alpha + _dot(vs_ref[kt, vrows[g], :], pp[g].astype(BF16))
            m_scr[g] = m_new[g]

    n_full = s0 // NSA_KT

    def slc_pair(p, c):
        a = 2 * p
        slc_scores(a + 1, 1)
        slc_update(a, 0, False)
        slc_scores(a + 2, 0)
        slc_update(a + 1, 1, False)
        return c

    slc_scores(0, 0)
    lax.fori_loop(0, n_full // 2, slc_pair, 0)
    odd = lax.rem(n_full, 2) == 1

    @pl.when(odd)
    def _():
        slc_scores(n_full, 1)
        slc_update(n_full - 1, 0, False)
        slc_update(n_full, 1, True)

    @pl.when(jnp.logical_not(odd))
    def _():
        slc_update(n_full, 0, True)

    o_slc = [acc_scr[g] / l_scr[g] for g in groups]

    heads = []
    for g in groups:
        for h in range(hg):
            r = (g * hg + h) * 3
            cols = slice(h * tq, (h + 1) * tq)
            heads.append(gt_ref[r:r + 1, :] * oc_scr[g, :, cols] + gt_ref[r + 1:r + 2, :] * o_slc[g][:, cols]
                         + gt_ref[r + 2:r + 3, :] * ow_scr[g, :, cols])
    o_ref[...] = jnp.concatenate(heads, axis=0).T.astype(o_ref.dtype)


def _cmp_to_slc_t(seq):
    ncp = seq // CMP_STRIDE
    ns = seq // SLC_BLOCK
    cs = np.arange(ncp)[None, :] * CMP_STRIDE
    ss = np.arange(ns)[:, None] * SLC_BLOCK
    ov = np.clip(np.minimum(cs + CMP_BLOCK, ss + SLC_BLOCK) - np.maximum(cs, ss), 0, None) / CMP_BLOCK
    ov[:, ncp - 1] = 0.0
    return jnp.asarray(ov, BF16)


def nsa_mixer(qt, qrt, gt, kc, vct, ks, kw, vs, vw, batch, seq):
    tq = NSA_TQ
    nq = seq // tq
    ncp = seq // CMP_STRIDE
    n_slc = seq // SLC_BLOCK
    assert n_slc <= 2 * HEAD_DIM and n_slc % 8 == 0 and seq >= WINDOW + tq
    assert NSA_KT % tq == 0
    w = NSA_GROUP * tq
    col = lambda b, i: (0, b * nq + i)
    return pl.pallas_call(
        functools.partial(_nsa_kernel, n_top=min(SLC_TOP, n_slc)),
        grid=(batch, nq),
        in_specs=[pl.BlockSpec((NSA_Q, tq), col),
                  pl.BlockSpec((NSA_Q, tq), col),
                  pl.BlockSpec((32, tq), col),
                  pl.BlockSpec((1, ncp, 2 * HEAD_DIM), lambda b, i: (b, 0, 0)),
                  pl.BlockSpec((1, 2 * HEAD_DIM, ncp), lambda b, i: (b, 0, 0)),
                  pl.BlockSpec((n_slc, ncp), lambda b, i: (0, 0)),
                  pl.BlockSpec((seq, 2 * LANES), lambda b, i: (b, 0)),
                  pl.BlockSpec((seq, LANES), lambda b, i: (b, 0)),
                  pl.BlockSpec((seq // NSA_KT, LANES, NSA_KT), lambda b, i: (b, 0, 0)),
                  pl.BlockSpec((seq // LANES, LANES, LANES), lambda b, i: (b, 0, 0))],
        out_specs=pl.BlockSpec((tq, NSA_Q), lambda b, i: (b * nq + i, 0)),
        out_shape=jax.ShapeDtypeStruct((batch * seq, NSA_Q), BF16),
        scratch_shapes=[pltpu.VMEM((2, NSA_KV_HEADS, NSA_KT, w), F32),
                        pltpu.VMEM((NSA_KV_HEADS, 4 * HEAD_DIM, w), BF16),
                        pltpu.VMEM((NSA_KV_HEADS, 1, w), F32),
                        pltpu.VMEM((NSA_KV_HEADS, 1, w), F32),
                        pltpu.VMEM((NSA_KV_HEADS, HEAD_DIM, w), F32),
                        pltpu.VMEM((NSA_KV_HEADS, HEAD_DIM, w), F32),
                        pltpu.VMEM((NSA_KV_HEADS, HEAD_DIM, w), F32)],
        compiler_params=_params("parallel", "parallel"),
        name="nsa_mixer",
    )(qt, qrt, gt, kc, vct, _cmp_to_slc_t(seq), ks, kw, vs, vw)


def kernel(x, mem, positions, w_in_even, nsa_cmp_pos, nsa_cmp_w1, nsa_cmp_w2, sgu_ln_g, sgu_ln_b, sgu_w, sgu_b, w_out_even, w_in_odd, hgrn_lb_logits, hgrn_norm_g, conv_w, conv_b, w_out_odd, xattn_w_q, xattn_w_kv, xattn_w_o, ln_g, ln_b, router_w, router_b, expert_w_gu, expert_b_gu, expert_w_dn, expert_b_dn):
    batch, seq, d = x.shape
    t = batch * seq
    cos_t, sin_t = rope_tables_t(positions)
    mem2 = mem.reshape(-1, d)
    xf = x.reshape(t, d)
    for layer in range(DEPTH):
        j = layer // 2
        if layer % 2 == 0:
            qt, qrt, ks, kw, vs, vw, gt, kvc, o_b = even_proj(xf, w_in_even[j], cos_t, sin_t, sgu_ln_g[j], sgu_ln_b[j],
                                                              sgu_w[j], sgu_b[j], seq)
            kc, vct = nsa_compress(kvc, nsa_cmp_pos[j], nsa_cmp_w1[j], nsa_cmp_w2[j], batch, seq)
            o_a = nsa_mixer(qt, qrt, gt, kc, vct, ks, kw, vs, vw, batch, seq)
            w_out = w_out_even[j]
        else:
            proj, o_b = odd_proj(xf, w_in_odd[j], conv_w[j], conv_b[j], batch, seq)
            o_a = hgrn2_mixer(proj, hgrn_lb_logits, hgrn_norm_g[j], layer, batch, seq)
            w_out = w_out_odd[j]
        xf = outproj_ln(o_a, o_b, xf, w_out, ln_g[layer, 0], ln_b[layer, 0])
        kv = matmul(mem2, xattn_w_kv[layer].astype(BF16), BF16, mem.shape[1]).reshape(batch, mem.shape[1], 2 * d)
        xf = xattn_ln(xf, kv, xattn_w_q[layer], xattn_w_o[layer], ln_g[layer, 1], ln_b[layer, 1], seq)
        xf = moe_ln(xf, router_w[layer], router_b[layer], expert_w_gu, expert_b_gu, expert_w_dn, expert_b_dn,
                    ln_g[layer, 2], ln_b[layer, 2], layer)
    return xf.reshape(batch, seq, d)
```

```python
import functools

import numpy as np
import jax
import jax.numpy as jnp
from jax import lax
from jax.experimental import pallas as pl
from jax.experimental.pallas import tpu as pltpu

F32 = jnp.float32
BF16 = jnp.bfloat16
I32 = jnp.int32

D_MODEL = 1024
DEPTH = 2
HEAD_DIM = 64
NSA_HEADS = 8
NSA_KV_HEADS = 2
NSA_GROUP = NSA_HEADS // NSA_KV_HEADS
CMP_BLOCK = 32
CMP_STRIDE = 16
CMP_HIDDEN = 256
SLC_BLOCK = 64
SLC_TOP = 16
WINDOW = 512
SLC_FORCED_SCORE = 1e4
SGU_GROUPS = 4
SGU_CH = 128
SGU_CHUNK = 128
HGRN_HEADS = 4
HGRN_DK = 128
HGRN_CHUNK = 64
CONV_CH = 512
XATTN_HEADS = 4
XATTN_DIM = D_MODEL // XATTN_HEADS
N_EXPERTS = 32
TOP_K = 4
D_EXPERT = D_MODEL
SWIGLU_LIMIT = 7.0
SWIGLU_ALPHA = 1.702
ROPE_THETA = 10000.0
LN_EPS = 1e-5
RMS_EPS = 1e-6
NEG_INF = -1e30
DEEPNORM_ALPHA = (2 * DEPTH) ** 0.25
LOG2_E = 1.4426950408889634

NSA_Q = NSA_HEADS * HEAD_DIM
SGU_W = SGU_GROUPS * SGU_CH
HGRN_W = HGRN_HEADS * HGRN_DK

VMEM_LIMIT_BYTES = 56 * 1024 * 1024
LANES = 128

PROJ_TM = 1024
XATTN_ROW_GROUPS = 2
OUTPROJ_ROW_GROUPS = 4
EVEN_ROW_GROUPS = 2
NSA_TQ = 256
NSA_KT = 256
HGRN_STEP = 32
HGRN_SEG = 1024
MOE_TM = 512
ROUTER_TILES = 4
MOE_BM = 512
SEG_ALIGN = 8
MOE_RT = TOP_K * MOE_TM + SEG_ALIGN * N_EXPERTS
MOE_CHUNK = 256


def _params(*sem):
    return pltpu.CompilerParams(dimension_semantics=sem, vmem_limit_bytes=VMEM_LIMIT_BYTES)


def _dot(a, b):
    return jnp.dot(a, b, preferred_element_type=F32)


def _dot_nt(a, b):
    return lax.dot_general(a, b, (((1,), (1,)), ((), ())), preferred_element_type=F32)


def _dot_tn(a, b):
    return lax.dot_general(a, b, (((0,), (0,)), ((), ())), preferred_element_type=F32)


def _gelu(x):
    return 0.5 * x * (1.0 + jnp.tanh(np.sqrt(2.0 / np.pi).astype(np.float32) * (x + 0.044715 * (x * x * x))))


def _sigmoid(x):
    return 1.0 / (1.0 + jnp.exp(-x))


def _layer_norm(y, g, b):
    mu = jnp.mean(y, axis=-1, keepdims=True)
    d = y - mu
    var = jnp.mean(d * d, axis=-1, keepdims=True)
    return d * lax.rsqrt(var + LN_EPS) * g + b


def _rope_kernel(pos_ref, inv_ref, cos_ref, sin_ref):
    ang = pos_ref[...].astype(F32) * inv_ref[...]
    cos_ref[...] = jnp.cos(ang)
    sin_ref[...] = jnp.sin(ang)


def rope_tables_t(positions):
    t = positions.size
    inv = 1.0 / (ROPE_THETA ** (jnp.arange(0, HEAD_DIM, 2, dtype=F32) / HEAD_DIM))
    tn = min(t, 4096)
    half = HEAD_DIM // 2
    return pl.pallas_call(
        _rope_kernel,
        grid=(t // tn,),
        in_specs=[pl.BlockSpec((1, tn), lambda i: (0, i)),
                  pl.BlockSpec((half, 1), lambda i: (0, 0))],
        out_specs=[pl.BlockSpec((half, tn), lambda i: (0, i))] * 2,
        out_shape=[jax.ShapeDtypeStruct((half, t), F32)] * 2,
        compiler_params=_params("parallel"),
        name="rope_tables",
    )(positions.reshape(1, t), inv.reshape(half, 1))


def _mm_kernel(x_ref, w_ref, o_ref):
    o_ref[...] = _dot(x_ref[...].astype(BF16), w_ref[...]).astype(o_ref.dtype)


def matmul(x, w, out_dtype, tm):
    m, k = x.shape
    n = w.shape[1]
    return pl.pallas_call(
        _mm_kernel,
        grid=(m // tm,),
        in_specs=[pl.BlockSpec((tm, k), lambda i: (i, 0)),
                  pl.BlockSpec((k, n), lambda i: (0, 0))],
        out_specs=pl.BlockSpec((tm, n), lambda i: (i, 0)),
        out_shape=jax.ShapeDtypeStruct((m, n), out_dtype),
        compiler_params=_params("parallel"),
        name="matmul",
    )(x, w)


def _outproj_ln_kernel(a_ref, b_ref, x_ref, wa_ref, wb_ref, g_ref, beta_ref, o_ref):
    rows = a_ref.shape[0] // OUTPROJ_ROW_GROUPS
    part = [slice(r * rows, (r + 1) * rows) for r in range(OUTPROJ_ROW_GROUPS)]
    mix = [_dot(a_ref[p, :].astype(BF16), wa_ref[...]) + _dot(b_ref[p, :].astype(BF16), wb_ref[...]) for p in part]
    for p, m in zip(part, mix):
        o_ref[p, :] = _layer_norm(DEEPNORM_ALPHA * x_ref[p, :] + m, g_ref[...], beta_ref[...])


def outproj_ln(a, b, x, w_out, g, beta):
    t, d = x.shape
    na, nb = a.shape[1], b.shape[1]
    tm = min(PROJ_TM, t)
    wa = w_out[:na].astype(BF16)
    wb = w_out[na:].astype(BF16)
    return pl.pallas_call(
        _outproj_ln_kernel,
        grid=(t // tm,),
        in_specs=[pl.BlockSpec((tm, na), lambda i: (i, 0)),
                  pl.BlockSpec((tm, nb), lambda i: (i, 0)),
                  pl.BlockSpec((tm, d), lambda i: (i, 0)),
                  pl.BlockSpec((na, d), lambda i: (0, 0)),
                  pl.BlockSpec((nb, d), lambda i: (0, 0)),
                  pl.BlockSpec((1, d), lambda i: (0, 0)),
                  pl.BlockSpec((1, d), lambda i: (0, 0))],
        out_specs=pl.BlockSpec((tm, d), lambda i: (i, 0)),
        out_shape=jax.ShapeDtypeStruct((t, d), F32),
        compiler_params=_params("parallel"),
        name="outproj_ln",
    )(a, b, x, wa, wb, g.reshape(1, d), beta.reshape(1, d))


def _xattn_kernel(x_ref, wq_ref, kv_ref, wo_ref, g_ref, beta_ref, o_ref):
    hw = XATTN_HEADS * XATTN_DIM
    rows = x_ref.shape[0] // XATTN_ROW_GROUPS
    groups = range(XATTN_ROW_GROUPS)
    heads = range(XATTN_HEADS)
    x = [x_ref[r * rows:(r + 1) * rows, :] for r in groups]
    q = [_dot(x[r].astype(BF16), wq_ref[...]) for r in groups]
    s = [[_dot_nt(q[r][:, h * XATTN_DIM:(h + 1) * XATTN_DIM].astype(BF16),
                  kv_ref[0, :, h * XATTN_DIM:(h + 1) * XATTN_DIM]) * (XATTN_DIM ** -0.5) for h in heads] for r in groups]
    o = []
    for r in groups:
        parts = []
        for h in heads:
            m = jnp.max(s[r][h], axis=-1, keepdims=True)
            e = jnp.exp(s[r][h] - m)
            p = e / jnp.sum(e, axis=-1, keepdims=True)
            parts.append(_dot(p.astype(BF16), kv_ref[0, :, hw + h * XATTN_DIM:hw + (h + 1) * XATTN_DIM]))
        o.append(jnp.concatenate(parts, axis=-1))
    xa = [_dot(o[r].astype(BF16), wo_ref[...]) for r in groups]
    for r in groups:
        o_ref[r * rows:(r + 1) * rows, :] = _layer_norm(DEEPNORM_ALPHA * x[r] + xa[r], g_ref[...], beta_ref[...])


def xattn_ln(x, kv, w_q, w_o, g, beta, seq):
    t, d = x.shape
    tm = min(PROJ_TM, seq)
    per_b = seq // tm
    mlen = kv.shape[1]
    return pl.pallas_call(
        _xattn_kernel,
        grid=(t // tm,),
        in_specs=[pl.BlockSpec((tm, d), lambda i: (i, 0)),
                  pl.BlockSpec((d, d), lambda i: (0, 0)),
                  pl.BlockSpec((1, mlen, 2 * d), lambda i: (i // per_b, 0, 0)),
                  pl.BlockSpec((d, d), lambda i: (0, 0)),
                  pl.BlockSpec((1, d), lambda i: (0, 0)),
                  pl.BlockSpec((1, d), lambda i: (0, 0))],
        out_specs=pl.BlockSpec((tm, d), lambda i: (i, 0)),
        out_shape=jax.ShapeDtypeStruct((t, d), F32),
        compiler_params=_params("parallel"),
        name="xattn_ln",
    )(x, w_q.astype(BF16), kv, w_o.astype(BF16), g.reshape(1, d), beta.reshape(1, d))


def _router_kernel(x_ref, wt_ref, b_ref, tri_ref, ltri_ref, row_ref, gate_ref, off_ref, cnt_ref, base_ref,
                   carry_ref):
    i = pl.program_id(0)

    @pl.when(i == 0)
    def _():
        carry_ref[...] = jnp.zeros_like(carry_ref)

    tm = MOE_TM
    tiles = range(x_ref.shape[0] // tm)
    cols = [slice(u * tm, (u + 1) * tm) for u in tiles]
    e_iota = lax.broadcasted_iota(I32, (N_EXPERTS, tm), 0)
    work = [_dot_nt(wt_ref[...], x_ref[cols[u], :].astype(BF16)) + b_ref[...] for u in tiles]
    vals, hots = [[] for _ in tiles], [[] for _ in tiles]
    for _ in range(TOP_K):
        for u in tiles:
            m = jnp.max(work[u], axis=0, keepdims=True)
            idx = jnp.min(jnp.where(work[u] == m, e_iota, N_EXPERTS), axis=0, keepdims=True)
            hot = e_iota == idx
            vals[u].append(m)
            hots[u].append(hot)
            work[u] = jnp.where(hot, -jnp.inf, work[u])
    hot_all, seg, off = [], [], []
    for u in tiles:
        exps = [jnp.exp(v - vals[u][0]) for v in vals[u]]
        den = exps[0] + exps[1] + exps[2] + exps[3]
        gate_ref[:, cols[u]] = jnp.concatenate([e / den for e in exps], axis=0)
        ha = jnp.zeros((N_EXPERTS, tm), F32)
        for hot in hots[u]:
            ha = ha + jnp.where(hot, 1.0, 0.0)
        hot_all.append(ha)
        n = jnp.sum(ha, axis=1, keepdims=True)
        sg = jnp.floor((n + (SEG_ALIGN - 1)) * (1.0 / SEG_ALIGN))
        seg.append(jnp.broadcast_to(sg, (N_EXPERTS, LANES)))
    rank = [_dot(hot_all[u].astype(BF16), tri_ref[...]) for u in tiles]
    off = [_dot(ltri_ref[...], seg[u].astype(BF16)) for u in tiles]
    carry = carry_ref[...]
    for u in tiles:
        where_row = off[u][:, 0:1] * SEG_ALIGN + rank[u]
        row_ref[:, cols[u]] = jnp.concatenate(
            [jnp.sum(jnp.where(hot, where_row, 0.0), axis=0, keepdims=True) for hot in hots[u]], axis=0).astype(I32)
        off_ref[u] = off[u] * SEG_ALIGN
        cnt_ref[u] = seg[u] * SEG_ALIGN
        base_ref[u] = carry
        carry = carry + seg[u] * SEG_ALIGN
    carry_ref[...] = carry


def moe_route(x, w_router, b_router):
    t, d = x.shape
    tm = MOE_TM
    nt = t // tm
    tri = jnp.asarray(np.triu(np.ones((tm, tm), np.float32), 1), BF16)
    ltri = jnp.asarray(np.tril(np.ones((N_EXPERTS, N_EXPERTS), np.float32), -1), BF16)
    per = ROUTER_TILES if nt % ROUTER_TILES == 0 else 1
    tab = pl.BlockSpec((per, N_EXPERTS, LANES), lambda i: (i, 0, 0))
    tab_shape = jax.ShapeDtypeStruct((nt, N_EXPERTS, LANES), F32)
    return pl.pallas_call(
        _router_kernel,
        grid=(nt // per,),
        in_specs=[pl.BlockSpec((per * tm, d), lambda i: (i, 0)),
                  pl.BlockSpec((N_EXPERTS, d), lambda i: (0, 0)),
                  pl.BlockSpec((N_EXPERTS, 1), lambda i: (0, 0)),
                  pl.BlockSpec((tm, tm), lambda i: (0, 0)),
                  pl.BlockSpec((N_EXPERTS, N_EXPERTS), lambda i: (0, 0))],
        out_specs=[pl.BlockSpec((TOP_K, per * tm), lambda i: (0, i)),
                   pl.BlockSpec((TOP_K, per * tm), lambda i: (0, i)),
                   tab, tab, tab],
        out_shape=[jax.ShapeDtypeStruct((TOP_K, t), I32),
                   jax.ShapeDtypeStruct((TOP_K, t), F32),
                   tab_shape, tab_shape, tab_shape],
        scratch_shapes=[pltpu.VMEM((N_EXPERTS, LANES), F32)],
        compiler_params=_params("arbitrary"),
        name="moe_router",
    )(x, w_router.T.astype(BF16), b_router.reshape(N_EXPERTS, 1), tri, ltri)


def _segment_copies(off_ref, cnt_ref, dst_ref, make_copy, wait):
    for e in range(N_EXPERTS):
        n = pl.multiple_of(cnt_ref[0, 0, e], SEG_ALIGN)

        @pl.when(n > 0)
        def _():
            cp = make_copy(pl.multiple_of(off_ref[0, 0, e], SEG_ALIGN), pl.multiple_of(dst_ref[0, 0, e], SEG_ALIGN), n)
            if wait:
                cp.wait()
            else:
                cp.start(priority=e % 2)


def _dispatch_kernel(off_ref, cnt_ref, dst_ref, poff_ref, pcnt_ref, pdst_ref, zero_ref, tail_cnt_ref, tail_dst_ref,
                     rest_ref, x_ref, row_ref, xs_hbm, buf, sem):
    i = pl.program_id(0)
    slot = lax.rem(i, 2)
    tm = x_ref.shape[0]
    xb = x_ref[...].astype(BF16)
    rows = [row_ref[k:k + 1, :] for k in range(TOP_K)]
    for c in range(MOE_RT // MOE_CHUNK):
        rr = c * MOE_CHUNK + lax.broadcasted_iota(I32, (MOE_CHUNK, tm), 0)
        perm = jnp.where(rr == rows[0], 1.0, 0.0)
        for k in range(1, TOP_K):
            perm = perm + jnp.where(rr == rows[k], 1.0, 0.0)
        buf[slot, c * MOE_CHUNK:(c + 1) * MOE_CHUNK, :] = _dot(perm.astype(BF16), xb)

    def copy_from(s):
        def copy(src_row, dst_row, n):
            return pltpu.make_async_copy(buf.at[s, pl.ds(src_row, n)], xs_hbm.at[pl.ds(dst_row, n)], sem.at[s])
        return copy

    @pl.when(i > 0)
    def _():
        _segment_copies(poff_ref, pcnt_ref, pdst_ref, copy_from(1 - slot), wait=True)

    _segment_copies(off_ref, cnt_ref, dst_ref, copy_from(slot), wait=False)

    @pl.when(i == pl.num_programs(0) - 1)
    def _():
        _segment_copies(off_ref, cnt_ref, dst_ref, copy_from(slot), wait=True)
        buf[0, 0:MOE_BM, :] = jnp.zeros((MOE_BM, buf.shape[2]), F32)
        zeros = copy_from(0)
        for wait in (False, True):
            _segment_copies(zero_ref, tail_cnt_ref, tail_dst_ref, zeros, wait=wait)

            def rest(j, c):
                cp = zeros(0, pl.multiple_of(rest_ref[0] + j * MOE_BM, MOE_BM), MOE_BM)
                cp.wait() if wait else cp.start()
                return c
            lax.fori_loop(0, rest_ref[1], rest, 0)


def _seg_spec(n_tiles, shift=0):
    def index(i):
        return (jnp.clip(i + shift, 0, n_tiles - 1), 0, 0)
    return pl.BlockSpec((1, 1, N_EXPERTS), index, memory_space=pltpu.SMEM)


def _smem_whole(shape):
    return pl.BlockSpec(shape, lambda i: (0,) * len(shape), memory_space=pltpu.SMEM)


def moe_dispatch(x, row, seg_off, seg_cnt, seg_dst, tail_cnt, tail_dst, rest, n_rows):
    t, d = x.shape
    tm = MOE_TM
    nt = t // tm
    tab = (1, 1, N_EXPERTS)
    return pl.pallas_call(
        _dispatch_kernel,
        grid=(nt,),
        in_specs=[_seg_spec(nt), _seg_spec(nt), _seg_spec(nt),
                  _seg_spec(nt, -1), _seg_spec(nt, -1), _seg_spec(nt, -1),
                  _smem_whole(tab), _smem_whole(tab), _smem_whole(tab), _smem_whole((2,)),
                  pl.BlockSpec((tm, d), lambda i: (i, 0)),
                  pl.BlockSpec((TOP_K, tm), lambda i: (0, i))],
        out_specs=pl.BlockSpec(memory_space=pl.ANY),
        out_shape=jax.ShapeDtypeStruct((n_rows, d), F32),
        scratch_shapes=[pltpu.VMEM((2, MOE_RT, d), F32), pltpu.SemaphoreType.DMA((2,))],
        compiler_params=_params("arbitrary"),
        name="moe_dispatch",
    )(seg_off, seg_cnt, seg_dst, seg_off, seg_cnt, seg_dst, jnp.zeros(tab, I32), tail_cnt.reshape(tab),
      tail_dst.reshape(tab), rest, x, row)


def _expert_kernel(blk_e_ref, nused_ref, xs_ref, wgu_ref, bgu_ref, wdn_ref, bdn_ref, ys_ref, wgu_bf, wdn_bf):
    i = pl.program_id(0)
    used = i < nused_ref[0]
    new_expert = (i == 0) | (blk_e_ref[i] != blk_e_ref[jnp.maximum(i - 1, 0)])

    @pl.when(used & new_expert)
    def _():
        rows = 128
        for r in range(0, wgu_bf.shape[0], rows):
            wgu_bf[r:r + rows, :] = wgu_ref[0, 0, r:r + rows, :].astype(BF16)
        for r in range(0, wdn_bf.shape[0], rows):
            wdn_bf[r:r + rows, :] = wdn_ref[0, 0, r:r + rows, :].astype(BF16)

    @pl.when(used)
    def _():
        h = _dot(xs_ref[...].astype(BF16), wgu_bf[...]) + bgu_ref[0, 0]
        h_gate = jnp.minimum(h[:, :D_EXPERT], SWIGLU_LIMIT)
        h_up = jnp.clip(h[:, D_EXPERT:], -SWIGLU_LIMIT, SWIGLU_LIMIT)
        act = (h_up + 1.0) * (h_gate * _sigmoid(h_gate * SWIGLU_ALPHA))
        ys_ref[...] = _dot(act.astype(BF16), wdn_bf[...]) + bdn_ref[0, 0]

    @pl.when(jnp.logical_not(used))
    def _():
        ys_ref[...] = jnp.zeros_like(ys_ref)


def moe_experts(xs, blk_e, nused, w_gu, b_gu, w_dn, b_dn, layer):
    n_rows, d = xs.shape
    bm = MOE_BM
    nb = n_rows // bm
    wsel = lambda i, be, nu: (layer, be[i], 0, 0)
    grid_spec = pltpu.PrefetchScalarGridSpec(
        num_scalar_prefetch=2,
        grid=(nb,),
        in_specs=[pl.BlockSpec((bm, d), lambda i, be, nu: (jnp.minimum(i, nu[0] - 1), 0)),
                  pl.BlockSpec((1, 1, d, 2 * D_EXPERT), wsel),
                  pl.BlockSpec((1, 1, 1, 2 * D_EXPERT), wsel),
                  pl.BlockSpec((1, 1, D_EXPERT, d), wsel),
                  pl.BlockSpec((1, 1, 1, d), wsel)],
        out_specs=pl.BlockSpec((bm, d), lambda i, be, nu: (i, 0)),
        scratch_shapes=[pltpu.VMEM((d, 2 * D_EXPERT), BF16), pltpu.VMEM((D_EXPERT, d), BF16)],
    )
    return pl.pallas_call(
        _expert_kernel,
        grid_spec=grid_spec,
        out_shape=jax.ShapeDtypeStruct((n_rows, d), F32),
        compiler_params=_params("arbitrary"),
        name="moe_experts",
    )(blk_e, nused, xs, w_gu, b_gu.reshape(DEPTH, N_EXPERTS, 1, -1), w_dn, b_dn.reshape(DEPTH, N_EXPERTS, 1, -1))


def _combine_ln_kernel(off_ref, cnt_ref, dst_ref, noff_ref, ncnt_ref, ndst_ref, ys_hbm, row_ref, gate_ref, x_ref,
                       g_ref, beta_ref, o_ref, buf, sem):
    i = pl.program_id(0)
    slot = lax.rem(i, 2)
    tm = x_ref.shape[0]

    def copy_into(s):
        def copy(buf_row, ys_row, n):
            return pltpu.make_async_copy(ys_hbm.at[pl.ds(ys_row, n)], buf.at[s, pl.ds(buf_row, n)], sem.at[s])
        return copy

    @pl.when(i == 0)
    def _():
        buf[...] = jnp.zeros_like(buf)
        _segment_copies(off_ref, cnt_ref, dst_ref, copy_into(0), wait=False)

    @pl.when(i + 1 < pl.num_programs(0))
    def _():
        _segment_copies(noff_ref, ncnt_ref, ndst_ref, copy_into(1 - slot), wait=False)

    rows = [jnp.broadcast_to(row_ref[:, k:k + 1], (tm, MOE_CHUNK)) for k in range(TOP_K)]
    gates = [jnp.broadcast_to(gate_ref[:, k:k + 1], (tm, MOE_CHUNK)) for k in range(TOP_K)]
    lane = lax.broadcasted_iota(I32, (tm, MOE_CHUNK), 1)
    _segment_copies(off_ref, cnt_ref, dst_ref, copy_into(slot), wait=True)
    ff = jnp.zeros((tm, x_ref.shape[1]), F32)
    for c in range(MOE_RT // MOE_CHUNK):
        rr = lane + c * MOE_CHUNK
        mix = jnp.where(rr == rows[0], gates[0], 0.0)
        for k in range(1, TOP_K):
            mix = mix + jnp.where(rr == rows[k], gates[k], 0.0)
        ff = ff + _dot(mix.astype(BF16), buf[slot, c * MOE_CHUNK:(c + 1) * MOE_CHUNK, :].astype(BF16))
    o_ref[...] = _layer_norm(DEEPNORM_ALPHA * x_ref[...] + ff, g_ref[...], beta_ref[...])


def moe_combine_ln(ys, row_t, gate_t, seg_off, seg_cnt, seg_dst, x, g, beta):
    t, d = x.shape
    tm = MOE_TM
    nt = t // tm
    return pl.pallas_call(
        _combine_ln_kernel,
        grid=(nt,),
        in_specs=[_seg_spec(nt), _seg_spec(nt), _seg_spec(nt),
                  _seg_spec(nt, 1), _seg_spec(nt, 1), _seg_spec(nt, 1),
                  pl.BlockSpec(memory_space=pl.ANY),
                  pl.BlockSpec((tm, TOP_K), lambda i: (i, 0)),
                  pl.BlockSpec((tm, TOP_K), lambda i: (i, 0)),
                  pl.BlockSpec((tm, d), lambda i: (i, 0)),
                  pl.BlockSpec((1, d), lambda i: (0, 0)),
                  pl.BlockSpec((1, d), lambda i: (0, 0))],
        out_specs=pl.BlockSpec((tm, d), lambda i: (i, 0)),
        out_shape=jax.ShapeDtypeStruct((t, d), F32),
        scratch_shapes=[pltpu.VMEM((2, MOE_RT, d), F32), pltpu.SemaphoreType.DMA((2,))],
        compiler_params=_params("arbitrary"),
        name="moe_combine_ln",
    )(seg_off, seg_cnt, seg_dst, seg_off, seg_cnt, seg_dst, ys, row_t, gate_t, x, g.reshape(1, d),
      beta.reshape(1, d))


def moe_ln(x, w_router, b_router, w_gu, b_gu, w_dn, b_dn, g, beta, layer):
    t, d = x.shape
    bm = MOE_BM
    nt = t // MOE_TM
    row, gate, off, cnt, base = moe_route(x, w_router, b_router)
    seg_off = off[:, :, 0].astype(I32)
    seg_cnt = cnt[:, :, 0].astype(I32)
    seg_base = base[:, :, 0].astype(I32)
    total = seg_base[-1] + seg_cnt[-1]
    padded = (total + bm - 1) // bm * bm
    pend = jnp.cumsum(padded)
    pstart = pend - padded
    n_rows = (t * TOP_K + SEG_ALIGN * N_EXPERTS * nt) // bm * bm + N_EXPERTS * bm
    nb = n_rows // bm
    blk_row = jnp.arange(nb, dtype=I32) * bm
    blk_e = jnp.minimum(jnp.sum((pend[None, :] <= blk_row[:, None]).astype(I32), axis=1), N_EXPERTS - 1)
    nused = (pend[-1] // bm).astype(I32).reshape(1)
    seg_dst = (pstart[None, :] + seg_base).reshape(nt, 1, N_EXPERTS)
    seg_off = seg_off.reshape(nt, 1, N_EXPERTS)
    seg_cnt = seg_cnt.reshape(nt, 1, N_EXPERTS)
    rest = jnp.stack([pend[-1], nb - nused[0]]).astype(I32)
    xs = moe_dispatch(x, row, seg_off, seg_cnt, seg_dst, padded - total, pstart + total, rest, n_rows)
    ys = moe_experts(xs, blk_e, nused, w_gu, b_gu, w_dn, b_dn, layer)
    return moe_combine_ln(ys, row.T, gate.T, seg_off, seg_cnt, seg_dst, x, g, beta)


def _sgu_tile(u, v, lng_ref, lnb_ref, w_ref, bs_ref, o_ref, row0):
    tm = u.shape[0]
    row = lax.broadcasted_iota(I32, (SGU_CHUNK, SGU_CHUNK), 0)
    col = lax.broadcasted_iota(I32, (SGU_CHUNK, SGU_CHUNK), 1)
    causal = row >= col
    for g in range(SGU_GROUPS):
        lo = g * SGU_CH
        vg = _gelu(v[:, lo:lo + SGU_CH])
        vg = _layer_norm(vg, lng_ref[g:g + 1, :], lnb_ref[g:g + 1, :]).astype(BF16)
        wg = jnp.where(causal, w_ref[g], 0.0).astype(BF16)
        bias = bs_ref[:, g:g + 1]
        for n in range(tm // SGU_CHUNK):
            r0 = n * SGU_CHUNK
            mix = _dot(wg, vg[r0:r0 + SGU_CHUNK]) + bias
            o_ref[row0 + r0:row0 + r0 + SGU_CHUNK, lo:lo + SGU_CH] = (
                _gelu(u[r0:r0 + SGU_CHUNK, lo:lo + SGU_CH]) * mix).astype(o_ref.dtype)


def _odd_proj_kernel(x_ref, w_ref, cw_ref, cb_ref, hg_ref, od_ref, carry_ref):
    @pl.when(pl.program_id(1) == 0)
    def _():
        carry_ref[...] = jnp.zeros_like(carry_ref)

    pr = _dot(x_ref[...].astype(BF16), w_ref[...])
    n_h = 4 * HGRN_W
    hg_ref[...] = pr[:, :n_h]
    h, bg, cg = (pr[:, n_h + j * CONV_CH:n_h + (j + 1) * CONV_CH] for j in range(3))
    z = cg * h
    tm = z.shape[0]
    row = lax.broadcasted_iota(I32, z.shape, 0)
    prev = carry_ref[...]
    z1 = jnp.where(row == 0, prev[7:8, :], pltpu.roll(z, 1, 0))
    z2 = jnp.where(row == 0, prev[6:7, :], jnp.where(row == 1, prev[7:8, :], pltpu.roll(z, 2, 0)))
    y = cw_ref[0:1, :] * z2 + cw_ref[1:2, :] * z1 + cw_ref[2:3, :] * z + cb_ref[...]
    od_ref[...] = (bg * y).astype(od_ref.dtype)
    carry_ref[...] = z[tm - 8:tm, :]


def odd_proj(x, w_in, conv_w, conv_b, batch, seq):
    t, d = x.shape
    tm = min(PROJ_TM, seq)
    per_b = seq // tm
    n = w_in.shape[1]
    n_h = 4 * HGRN_W
    c = CONV_CH
    tok = lambda b, i: (b * per_b + i, 0)
    return pl.pallas_call(
        _odd_proj_kernel,
        grid=(batch, per_b),
        in_specs=[pl.BlockSpec((tm, d), tok),
                  pl.BlockSpec((d, n), lambda b, i: (0, 0)),
                  pl.BlockSpec((3, c), lambda b, i: (0, 0)),
                  pl.BlockSpec((1, c), lambda b, i: (0, 0))],
        out_specs=[pl.BlockSpec((tm, n_h), tok), pl.BlockSpec((tm, c), tok)],
        out_shape=[jax.ShapeDtypeStruct((t, n_h), F32), jax.ShapeDtypeStruct((t, c), BF16)],
        scratch_shapes=[pltpu.VMEM((8, c), F32)],
        compiler_params=_params("arbitrary", "arbitrary"),
        name="odd_proj",
    )(x, w_in.astype(BF16), conv_w, conv_b.reshape(1, c))


def _split3(x):
    hi = x.astype(BF16)
    r1 = x - hi.astype(F32)
    mid = r1.astype(BF16)
    lo = (r1 - mid.astype(F32)).astype(BF16)
    return hi, mid, lo


def _hgrn_kernel(q_ref, f_ref, i_ref, g_ref, lbl_ref, ng_ref, o_ref, state_ref, *, layer):
    c = HGRN_STEP
    dk = HGRN_DK
    n_chunks = q_ref.shape[0] // c
    lw = lbl_ref[...]
    lw = jnp.exp(lw - jnp.max(lw, axis=0, keepdims=True))
    lw = lw / jnp.sum(lw, axis=0, keepdims=True)
    lb = jnp.sum(lw[1:layer + 1], axis=0, keepdims=True)
    row = lax.broadcasted_iota(I32, (c, c), 0)
    col = lax.broadcasted_iota(I32, (c, c), 1)
    tril = jnp.where(row >= col, 1.0, 0.0).astype(BF16)
    sub = lax.broadcasted_iota(I32, (8, dk), 0)

    @pl.when(pl.program_id(1) == 0)
    def _():
        state_ref[...] = jnp.zeros_like(state_ref)

    def decay(ci):
        z = f_ref[pl.ds(pl.multiple_of(ci * c, c), c), :]
        k_all = (1.0 - lb) * _sigmoid(-z)
        log_f = jnp.log1p(-k_all)
        hi, mid, lo = _split3(log_f)
        b_all = _dot(tril, hi) + _dot(tril, mid) + _dot(tril, lo)
        return k_all, b_all * LOG2_E

    def chunk(ci, carry):
        k_all, b_all = carry
        ahead = decay(jnp.minimum(ci + 1, n_chunks - 1))
        r0 = pl.multiple_of(ci * c, c)
        q_all = q_ref[pl.ds(r0, c), :]
        v_all = i_ref[pl.ds(r0, c), :]
        g_all = g_ref[pl.ds(r0, c), :]
        heads = range(HGRN_HEADS)
        nblk = c // 8
        blk = [slice(8 * j, 8 * j + 8) for j in range(nblk)]
        q, k, v, b = ([x[:, h * dk:(h + 1) * dk] for h in heads] for x in (q_all, k_all, v_all, b_all))
        states = [state_ref[h] for h in heads]
        o = [_dot_nt((q[h] * jnp.exp2(b[h])).astype(BF16), states[h].astype(BF16)) for h in heads]
        b_last = [b[h][c - 1:c, :] for h in heads]
        for h in heads:
            kd = (k[h] * jnp.exp2(b_last[h] - b[h])).astype(BF16)
            state_ref[h] = jnp.exp2(b_last[h]) * states[h] + _dot_tn(v[h].astype(BF16), kd)

        k_dec = [jnp.concatenate([k[h][blk[j]] * jnp.exp2(b[h][8 * j + 7:8 * j + 8] - b[h][blk[j]])
                                  for j in range(nblk)], axis=0) for h in heads]
        q_dec = [{(j, j0): q[h][blk[j]] * jnp.exp2(b[h][blk[j]] - b[h][8 * j0 + 7:8 * j0 + 8])
                  for j0 in range(nblk) for j in range(j0 + 1, nblk)} for h in heads]
        acc = [[o[h][blk[j]] for j in range(nblk)] for h in heads]
        for s in range(c):
            j0 = s // 8
            for h in heads:
                vs = v[h][s:s + 1, :]
                dlt = jnp.where(sub >= s - 8 * j0, b[h][blk[j0]] - b[h][s:s + 1, :], NEG_INF)
                a = jnp.sum(q[h][blk[j0]] * k[h][s:s + 1, :] * jnp.exp2(dlt), axis=-1, keepdims=True)
                acc[h][j0] = acc[h][j0] + a * vs
                for j in range(j0 + 1, nblk):
                    a = jnp.sum(q_dec[h][j, j0] * k_dec[h][s:s + 1, :], axis=-1, keepdims=True)
                    acc[h][j] = acc[h][j] + a * vs
        outs = []
        for h in heads:
            oh = jnp.concatenate(acc[h], axis=0)
            outs.append(oh * lax.rsqrt(jnp.mean(oh * oh, axis=-1, keepdims=True) + RMS_EPS))
        o_ref[pl.ds(r0, c), :] = (jnp.concatenate(outs, axis=1) * ng_ref[...]
                                  * (g_all * _sigmoid(g_all))).astype(o_ref.dtype)
        return ahead

    lax.fori_loop(0, n_chunks, chunk, decay(0))


def hgrn2_mixer(proj, lb_logits, norm_g, layer, batch, seq):
    w = HGRN_W
    seg = min(seq, HGRN_SEG)
    per_b = seq // seg

    def spec(grp):
        return pl.BlockSpec((seg, w), lambda b, i: (b * per_b + i, grp))

    return pl.pallas_call(
        functools.partial(_hgrn_kernel, layer=layer),
        grid=(batch, per_b),
        in_specs=[spec(0), spec(1), spec(2), spec(3),
                  pl.BlockSpec((DEPTH, w), lambda b, i: (0, 0)),
                  pl.BlockSpec((1, w), lambda b, i: (0, 0))],
        out_specs=pl.BlockSpec((seg, w), lambda b, i: (b * per_b + i, 0)),
        out_shape=jax.ShapeDtypeStruct((batch * seq, w), BF16),
        scratch_shapes=[pltpu.VMEM((HGRN_HEADS, HGRN_DK, HGRN_DK), F32)],
        compiler_params=_params("arbitrary", "arbitrary"),
        name="hgrn2_mixer",
    )(proj, proj, proj, proj, lb_logits, norm_g.reshape(1, w))


EVEN_T_ROWS = NSA_Q + 4 * 2 * HEAD_DIM + 32
EVEN_S_COLS = 4 * LANES + 2 * SGU_W


def _even_proj_kernel(x_ref, wt_ref, ws_ref, cos_ref, sin_ref, lng_ref, lnb_ref, sw_ref, sb_ref, qt_ref, qrt_ref,
                      ks_ref, kw_ref, vs_ref, vw_ref, gt_ref, kvc_ref, ob_ref, *, per_b):
    tm = x_ref.shape[0]
    half = HEAD_DIM // 2
    scale = HEAD_DIM ** -0.5 * LOG2_E
    v0 = NSA_Q + 4 * HEAD_DIM
    rows = tm // EVEN_ROW_GROUPS
    part = [slice(r * rows, (r + 1) * rows) for r in range(EVEN_ROW_GROUPS)]
    xb = [x_ref[p, :].astype(BF16) for p in part]
    st_all = [_dot_nt(wt_ref[...], xb_r) for xb_r in xb]
    ss_all = [_dot(xb_r, ws_ref[...]) for xb_r in xb]
    for r, p in enumerate(part):
        st, ss = st_all[r], ss_all[r]
        cos = cos_ref[:, p]
        sin = sin_ref[:, p]

        def rope(blk, cos=cos, sin=sin):
            x1, x2 = blk[:half], blk[half:]
            return jnp.concatenate([x1 * cos - x2 * sin, x2 * cos + x1 * sin], axis=0)

        for hh in range(NSA_HEADS):
            blk = st[hh * HEAD_DIM:(hh + 1) * HEAD_DIM]
            qt_ref[hh * HEAD_DIM:(hh + 1) * HEAD_DIM, p] = (blk * scale).astype(BF16)
            qrt_ref[hh * HEAD_DIM:(hh + 1) * HEAD_DIM, p] = (rope(blk) * scale).astype(BF16)

        kk = jnp.concatenate([rope(st[NSA_Q + j * HEAD_DIM:NSA_Q + (j + 1) * HEAD_DIM]) for j in range(4)], axis=0)
        kk = kk.T
        pos = (lax.rem(pl.program_id(0), per_b) * tm + r * rows) + lax.broadcasted_iota(I32, (rows, LANES), 0)
        lane = lax.broadcasted_iota(I32, (rows, LANES), 1)
        member = jnp.where(lane == pos // SLC_BLOCK, 1.0, 0.0)
        ks_ref[p, :] = jnp.concatenate([kk[:, :LANES], member], axis=1).astype(BF16)
        kw_ref[p, :] = kk[:, LANES:].astype(BF16)
        for j in range(rows // NSA_KT):
            vs_ref[r * (rows // NSA_KT) + j] = st[v0:v0 + LANES, j * NSA_KT:(j + 1) * NSA_KT].astype(BF16)
        for j in range(rows // LANES):
            vw_ref[r * (rows // LANES) + j] = st[v0 + LANES:v0 + 2 * LANES, j * LANES:(j + 1) * LANES].astype(BF16)
        gt_ref[:, p] = _sigmoid(st[v0 + 2 * LANES:v0 + 2 * LANES + 32])

        for j in range(4):
            kvc_ref[j, p, :] = ss[:, j * LANES:j * LANES + HEAD_DIM]
        _sgu_tile(ss[:, 4 * LANES:4 * LANES + SGU_W], ss[:, 4 * LANES + SGU_W:], lng_ref, lnb_ref, sw_ref, sb_ref,
                  ob_ref, r * rows)


def even_proj(x, w_in, cos_t, sin_t, sgu_ln_g, sgu_ln_b, sgu_w, sgu_b, seq):
    t, d = x.shape
    tm = min(PROJ_TM, seq)
    per_b = seq // tm
    hd = HEAD_DIM
    kv0 = NSA_Q

    def kvcols(i):
        return w_in[:, kv0 + i * 2 * hd:kv0 + (i + 1) * 2 * hd]

    g0 = kv0 + 6 * 2 * hd
    n_gates = 3 * NSA_HEADS
    wt = jnp.concatenate([w_in[:, :NSA_Q], kvcols(2), kvcols(4), kvcols(3), kvcols(5),
                          w_in[:, g0:g0 + n_gates], jnp.zeros((d, 32 - n_gates), F32)], axis=1).T.astype(BF16)
    zpad = jnp.zeros((d, LANES - hd), F32)
    cmp_cols = []
    for i in (0, 1):
        for g in range(NSA_KV_HEADS):
            cmp_cols += [w_in[:, kv0 + i * 2 * hd + g * hd:kv0 + i * 2 * hd + (g + 1) * hd], zpad]
    ws = jnp.concatenate(cmp_cols + [w_in[:, g0 + n_gates:]], axis=1).astype(BF16)
    half = hd // 2
    tok = lambda i: (i, 0)
    tok_t = lambda i: (0, i)
    return pl.pallas_call(
        functools.partial(_even_proj_kernel, per_b=per_b),
        grid=(t // tm,),
        in_specs=[pl.BlockSpec((tm, d), tok),
                  pl.BlockSpec((EVEN_T_ROWS, d), lambda i: (0, 0)),
                  pl.BlockSpec((d, EVEN_S_COLS), lambda i: (0, 0)),
                  pl.BlockSpec((half, tm), tok_t),
                  pl.BlockSpec((half, tm), tok_t),
                  pl.BlockSpec((SGU_GROUPS, SGU_CH), lambda i: (0, 0)),
                  pl.BlockSpec((SGU_GROUPS, SGU_CH), lambda i: (0, 0)),
                  pl.BlockSpec((SGU_GROUPS, SGU_CHUNK, SGU_CHUNK), lambda i: (0, 0, 0)),
                  pl.BlockSpec((SGU_CHUNK, SGU_GROUPS), lambda i: (0, 0))],
        out_specs=[pl.BlockSpec((NSA_Q, tm), tok_t),
                   pl.BlockSpec((NSA_Q, tm), tok_t),
                   pl.BlockSpec((tm, 2 * LANES), tok),
                   pl.BlockSpec((tm, LANES), tok),
                   pl.BlockSpec((tm // NSA_KT, LANES, NSA_KT), lambda i: (i, 0, 0)),
                   pl.BlockSpec((tm // LANES, LANES, LANES), lambda i: (i, 0, 0)),
                   pl.BlockSpec((32, tm), tok_t),
                   pl.BlockSpec((4, tm, hd), lambda i: (0, i, 0)),
                   pl.BlockSpec((tm, SGU_W), tok)],
        out_shape=[jax.ShapeDtypeStruct((NSA_Q, t), BF16),
                   jax.ShapeDtypeStruct((NSA_Q, t), BF16),
                   jax.ShapeDtypeStruct((t, 2 * LANES), BF16),
                   jax.ShapeDtypeStruct((t, LANES), BF16),
                   jax.ShapeDtypeStruct((t // NSA_KT, LANES, NSA_KT), BF16),
                   jax.ShapeDtypeStruct((t // LANES, LANES, LANES), BF16),
                   jax.ShapeDtypeStruct((32, t), F32),
                   jax.ShapeDtypeStruct((4, t, hd), F32),
                   jax.ShapeDtypeStruct((t, SGU_W), BF16)],
        compiler_params=_params("parallel"),
        name="even_proj",
    )(x, wt, ws, cos_t, sin_t, sgu_ln_g, sgu_ln_b, sgu_w, sgu_b.T)


def _compress_kernel(kvc_ref, pos_ref, w1_ref, w2_ref, kc_ref, vct_ref):
    ncp = kc_ref.shape[1]
    assert CMP_BLOCK == 2 * CMP_STRIDE
    for i in range(2):
        outs = []
        for g in range(NSA_KV_HEADS):
            first = jnp.zeros((ncp, CMP_HIDDEN), F32)
            second = jnp.zeros((ncp, CMP_HIDDEN), F32)
            for l in range(CMP_STRIDE):
                rows = kvc_ref[i * NSA_KV_HEADS + g, pl.ds(l, ncp, stride=CMP_STRIDE), :]
                first = first + _dot((rows + pos_ref[i, l:l + 1, :]).astype(BF16), w1_ref[i, l])
                l2 = l + CMP_STRIDE
                second = second + _dot((rows + pos_ref[i, l2:l2 + 1, :]).astype(BF16), w1_ref[i, l2])
            hid = _gelu(first + pltpu.roll(second, ncp - 1, 0))
            outs.append(_dot(hid.astype(BF16), w2_ref[i]))
        if i == 0:
            kc_ref[0] = jnp.concatenate(outs, axis=1).astype(BF16)
        else:
            vct_ref[0] = jnp.concatenate(outs, axis=1).T.astype(BF16)


def nsa_compress(kvc, cmp_pos, cmp_w1, cmp_w2, batch, seq):
    hd = HEAD_DIM
    ncp = seq // CMP_STRIDE
    w1 = cmp_w1.reshape(2, CMP_BLOCK, hd, CMP_HIDDEN).astype(BF16)
    return pl.pallas_call(
        _compress_kernel,
        grid=(batch,),
        in_specs=[pl.BlockSpec((4, seq, hd), lambda b: (0, b, 0)),
                  pl.BlockSpec((2, CMP_BLOCK, hd), lambda b: (0, 0, 0)),
                  pl.BlockSpec((2, CMP_BLOCK, hd, CMP_HIDDEN), lambda b: (0, 0, 0, 0)),
                  pl.BlockSpec((2, CMP_HIDDEN, hd), lambda b: (0, 0, 0))],
        out_specs=[pl.BlockSpec((1, ncp, 2 * hd), lambda b: (b, 0, 0)),
                   pl.BlockSpec((1, 2 * hd, ncp), lambda b: (b, 0, 0))],
        out_shape=[jax.ShapeDtypeStruct((batch, ncp, 2 * hd), BF16),
                   jax.ShapeDtypeStruct((batch, 2 * hd, ncp), BF16)],
        compiler_params=_params("parallel"),
        name="nsa_compress",
    )(kvc, cmp_pos, w1, cmp_w2.astype(BF16))


def _softmax2_cols(s, mask):
    sm = jnp.where(mask, s, NEG_INF)
    m = jnp.max(sm, axis=0, keepdims=True)
    e = jnp.exp2(sm - m)
    return jnp.where(mask, e / jnp.sum(e, axis=0, keepdims=True), 0.0)


def _nsa_kernel(qt_ref, qrt_ref, gt_ref, kc_ref, vct_ref, mct_ref, ks_ref, kw_ref, vs_ref, vw_ref, o_ref,
                s_scr, q_scr, m_scr, l_scr, acc_scr, oc_scr, ow_scr, *, n_top):
    tq = qt_ref.shape[1]
    hg = NSA_GROUP
    hd = HEAD_DIM
    groups = range(NSA_KV_HEADS)
    w = hg * tq
    ncp = kc_ref.shape[1]
    n_slc = mct_ref.shape[0]
    s0 = pl.program_id(1) * tq
    t_lane = s0 + lax.rem(lax.broadcasted_iota(I32, (1, w), 1), tq)
    t_q = s0 + lax.broadcasted_iota(I32, (1, tq), 1)
    zeros_g = jnp.zeros((hd, w), BF16)
    n_win = WINDOW // LANES + tq // LANES
    wt0 = jnp.maximum(s0 // LANES - WINDOW // LANES, 0)
    vrows = [slice(g * hd, (g + 1) * hd) for g in groups]

    def grp(ref, g):
        rows = jnp.concatenate([ref[(g * hg + h) * hd:(g * hg + h + 1) * hd, :] for h in range(hg)], axis=1)
        return jnp.concatenate([rows, zeros_g] if g == 0 else [zeros_g, rows], axis=0)

    q_rot = [grp(qrt_ref, g) for g in groups]
    kw0 = pl.multiple_of(wt0 * LANES, LANES)
    k_win = kw_ref[pl.ds(kw0, n_win * LANES), :]
    s_cmp = [_dot(kc_ref[0], grp(qt_ref, g)) for g in groups]
    s_win = [_dot(k_win, q_rot[g]) for g in groups]

    c_end = lax.broadcasted_iota(I32, (ncp, w), 0) * CMP_STRIDE + (CMP_BLOCK - 1)
    p_cmp = [_softmax2_cols(s_cmp[g], c_end <= t_lane) for g in groups]
    imp = []
    for g in groups:
        oc_scr[g] = _dot(vct_ref[0, vrows[g], :], p_cmp[g].astype(BF16))
        psum = p_cmp[g][:, 0:tq]
        for h in range(1, hg):
            psum = psum + p_cmp[g][:, h * tq:(h + 1) * tq]
        imp.append(_dot(mct_ref[...], psum.astype(BF16)))

    dist = t_lane - (kw0 + lax.broadcasted_iota(I32, (n_win * LANES, w), 0))
    in_window = (dist >= 0) & (dist < WINDOW)
    for g in groups:
        sm = jnp.where(in_window, s_win[g], NEG_INF)
        e = jnp.exp2(sm - jnp.max(sm, axis=0, keepdims=True))
        den = jnp.sum(e, axis=0, keepdims=True)
        eb = e.astype(BF16)
        ow = _dot(vw_ref[wt0, vrows[g], :], eb[0:LANES])
        for j in range(1, n_win):
            ow = ow + _dot(vw_ref[wt0 + j, vrows[g], :], eb[j * LANES:(j + 1) * LANES])
        ow_scr[g] = ow / den

    j_blk = lax.broadcasted_iota(I32, (n_slc, tq), 0)
    cur = t_q // SLC_BLOCK
    forced = (j_blk == 0) | (j_blk == cur) | (j_blk == cur - 1)
    valid = j_blk <= cur
    score = [jnp.where(forced, SLC_FORCED_SCORE, jnp.where(valid, imp[g], -1.0)) for g in groups]
    nv = n_slc // 8
    sblk = [[score[g][8 * v:8 * v + 8] for v in range(nv)] for g in groups]
    rank = [[jnp.zeros((8, tq), F32) for _ in range(nv)] for g in groups]
    sub = lax.broadcasted_iota(I32, (8, tq), 0)
    for k in range(n_slc):
        kv_ = k // 8
        for g in groups:
            sk = score[g][k:k + 1, :]
            for v in range(nv):
                ge = jnp.where(sk >= sblk[g][v], 1.0, 0.0)
                gt = jnp.where(sk > sblk[g][v], 1.0, 0.0)
                if v > kv_:
                    beats = ge
                elif v < kv_:
                    beats = gt
                else:
                    beats = jnp.where(sub > k - 8 * kv_, ge, gt)
                rank[g][v] = rank[g][v] + beats
    q_aug = []
    for g in groups:
        bias = jnp.where((jnp.concatenate(rank[g], axis=0) < n_top) & valid, 0.0, NEG_INF)
        bias = jnp.concatenate([bias] * hg, axis=1)
        if n_slc < 2 * hd:
            bias = jnp.concatenate([bias, jnp.zeros((2 * hd - n_slc, w), F32)], axis=0)
        q_aug.append(jnp.concatenate([q_rot[g], bias.astype(BF16)], axis=0))

    for g in groups:
        q_scr[g] = q_aug[g]
        m_scr[g] = jnp.full((1, w), NEG_INF, F32)
        l_scr[g] = jnp.zeros((1, w), F32)
        acc_scr[g] = jnp.zeros((hd, w), F32)

    def slc_scores(kt, slot):
        k_tile = ks_ref[pl.ds(pl.multiple_of(kt * NSA_KT, NSA_KT), NSA_KT), :]
        for g in groups:
            s_scr[slot, g] = _dot(k_tile, q_scr[g])

    def slc_update(kt, slot, diagonal):
        s = [s_scr[slot, g] for g in groups]
        if diagonal:
            causal = kt * NSA_KT + lax.broadcasted_iota(I32, (NSA_KT, w), 0) <= t_lane
            s = [jnp.where(causal, s[g], NEG_INF) for g in groups]
        m_old = [m_scr[g] for g in groups]
        m_new = [jnp.maximum(m_old[g], jnp.max(s[g], axis=0, keepdims=True)) for g in groups]
        pp = [jnp.exp2(s[g] - m_new[g]) for g in groups]
        for g in groups:
            alpha = jnp.exp2(m_old[g] - m_new[g])
            l_scr[g] = l_scr[g] * alpha + jnp.sum(pp[g], axis=0, keepdims=True)
            acc_scr[g] = acc_scr[g] * alpha + _dot(vs_ref[kt, vrows[g], :], pp[g].astype(BF16))
            m_scr[g] = m_new[g]

    n_full = s0 // NSA_KT

    def slc_pair(p, c):
        a = 2 * p
        slc_scores(a + 1, 1)
        slc_update(a, 0, False)
        slc_scores(a + 2, 0)
        slc_update(a + 1, 1, False)
        return c

    slc_scores(0, 0)
    lax.fori_loop(0, n_full // 2, slc_pair, 0)
    odd = lax.rem(n_full, 2) == 1

    @pl.when(odd)
    def _():
        slc_scores(n_full, 1)
        slc_update(n_full - 1, 0, False)
        slc_update(n_full, 1, True)

    @pl.when(jnp.logical_not(odd))
    def _():
        slc_update(n_full, 0, True)

    o_slc = [acc_scr[g] / l_scr[g] for g in groups]

    heads = []
    for g in groups:
        for h in range(hg):
            r = (g * hg + h) * 3
            cols = slice(h * tq, (h + 1) * tq)
            heads.append(gt_ref[r:r + 1, :] * oc_scr[g, :, cols] + gt_ref[r + 1:r + 2, :] * o_slc[g][:, cols]
                         + gt_ref[r + 2:r + 3, :] * ow_scr[g, :, cols])
    o_ref[...] = jnp.concatenate(heads, axis=0).T.astype(o_ref.dtype)


def _cmp_to_slc_t(seq):
    ncp = seq // CMP_STRIDE
    ns = seq // SLC_BLOCK
    cs = np.arange(ncp)[None, :] * CMP_STRIDE
    ss = np.arange(ns)[:, None] * SLC_BLOCK
    ov = np.clip(np.minimum(cs + CMP_BLOCK, ss + SLC_BLOCK) - np.maximum(cs, ss), 0, None) / CMP_BLOCK
    ov[:, ncp - 1] = 0.0
    return jnp.asarray(ov, BF16)


def nsa_mixer(qt, qrt, gt, kc, vct, ks, kw, vs, vw, batch, seq):
    tq = NSA_TQ
    nq = seq // tq
    ncp = seq // CMP_STRIDE
    n_slc = seq // SLC_BLOCK
    assert n_slc <= 2 * HEAD_DIM and n_slc % 8 == 0 and seq >= WINDOW + tq
    w = NSA_GROUP * tq
    col = lambda b, i: (0, b * nq + i)
    return pl.pallas_call(
        functools.partial(_nsa_kernel, n_top=min(SLC_TOP, n_slc)),
        grid=(batch, nq),
        in_specs=[pl.BlockSpec((NSA_Q, tq), col),
                  pl.BlockSpec((NSA_Q, tq), col),
                  pl.BlockSpec((32, tq), col),
                  pl.BlockSpec((1, ncp, 2 * HEAD_DIM), lambda b, i: (b, 0, 0)),
                  pl.BlockSpec((1, 2 * HEAD_DIM, ncp), lambda b, i: (b, 0, 0)),
                  pl.BlockSpec((n_slc, ncp), lambda b, i: (0, 0)),
                  pl.BlockSpec((seq, 2 * LANES), lambda b, i: (b, 0)),
                  pl.BlockSpec((seq, LANES), lambda b, i: (b, 0)),
                  pl.BlockSpec((seq // NSA_KT, LANES, NSA_KT), lambda b, i: (b, 0, 0)),
                  pl.BlockSpec((seq // LANES, LANES, LANES), lambda b, i: (b, 0, 0))],
        out_specs=pl.BlockSpec((tq, NSA_Q), lambda b, i: (b * nq + i, 0)),
        out_shape=jax.ShapeDtypeStruct((batch * seq, NSA_Q), BF16),
        scratch_shapes=[pltpu.VMEM((2, NSA_KV_HEADS, NSA_KT, w), F32),
                        pltpu.VMEM((NSA_KV_HEADS, 4 * HEAD_DIM, w), BF16),
                        pltpu.VMEM((NSA_KV_HEADS, 1, w), F32),
                        pltpu.VMEM((NSA_KV_HEADS, 1, w), F32),
                        pltpu.VMEM((NSA_KV_HEADS, HEAD_DIM, w), F32),
                        pltpu.VMEM((NSA_KV_HEADS, HEAD_DIM, w), F32),
                        pltpu.VMEM((NSA_KV_HEADS, HEAD_DIM, w), F32)],
        compiler_params=_params("parallel", "parallel"),
        name="nsa_mixer",
    )(qt, qrt, gt, kc, vct, _cmp_to_slc_t(seq), ks, kw, vs, vw)


def kernel(x, mem, positions, w_in_even, nsa_cmp_pos, nsa_cmp_w1, nsa_cmp_w2, sgu_ln_g, sgu_ln_b, sgu_w, sgu_b, w_out_even, w_in_odd, hgrn_lb_logits, hgrn_norm_g, conv_w, conv_b, w_out_odd, xattn_w_q, xattn_w_kv, xattn_w_o, ln_g, ln_b, router_w, router_b, expert_w_gu, expert_b_gu, expert_w_dn, expert_b_dn):
    batch, seq, d = x.shape
    t = batch * seq
    cos_t, sin_t = rope_tables_t(positions)
    mem2 = mem.reshape(-1, d)
    xf = x.reshape(t, d)
    for layer in range(DEPTH):
        j = layer // 2
        if layer % 2 == 0:
            qt, qrt, ks, kw, vs, vw, gt, kvc, o_b = even_proj(xf, w_in_even[j], cos_t, sin_t, sgu_ln_g[j], sgu_ln_b[j],
                                                              sgu_w[j], sgu_b[j], seq)
            kc, vct = nsa_compress(kvc, nsa_cmp_pos[j], nsa_cmp_w1[j], nsa_cmp_w2[j], batch, seq)
            o_a = nsa_mixer(qt, qrt, gt, kc, vct, ks, kw, vs, vw, batch, seq)
            w_out = w_out_even[j]
        else:
            proj, o_b = odd_proj(xf, w_in_odd[j], conv_w[j], conv_b[j], batch, seq)
            o_a = hgrn2_mixer(proj, hgrn_lb_logits, hgrn_norm_g[j], layer, batch, seq)
            w_out = w_out_odd[j]
        xf = outproj_ln(o_a, o_b, xf, w_out, ln_g[layer, 0], ln_b[layer, 0])
        kv = matmul(mem2, xattn_w_kv[layer].astype(BF16), BF16, mem.shape[1]).reshape(batch, mem.shape[1], 2 * d)
        xf = xattn_ln(xf, kv, xattn_w_q[layer], xattn_w_o[layer], ln_g[layer, 1], ln_b[layer, 1], seq)
        xf = moe_ln(xf, router_w[layer], router_b[layer], expert_w_gu, expert_b_gu, expert_w_dn, expert_b_dn,
                    ln_g[layer, 2], ln_b[layer, 2], layer)
    return xf.reshape(batch, seq, d)
```
